```python
import math
import jax, jax.numpy as jnp
from jax import lax
import numpy as np

D_MODEL = 2048
BATCH = 32
SEQ = 256
DEPTH = 2
DEC_BATCH = 8
DEC_SEQ = 1024
PAST_LEN = 512

GRID_W = 64
HEAD_DIM = 64
BLK = 128
A_HEADS = 8
A_KV_HEADS = 2
A_GROUP = A_HEADS // A_KV_HEADS
WIN = 128
DIFF_HEADS = 4
DIFF_V_DIM = 2 * HEAD_DIM
NA_HEADS = 8
NA_ROWS = 8
NA_COLS = 16
F_GROUPS = 4
F_GROUP_W = 128
N_BRANCH = 4
BR_W = A_HEADS * HEAD_DIM
IN_W = A_HEADS * HEAD_DIM + 2 * A_KV_HEADS * HEAD_DIM + 2 * DIFF_HEADS * 2 * HEAD_DIM + DIFF_HEADS * DIFF_V_DIM + 3 * NA_HEADS * HEAD_DIM + F_GROUPS * F_GROUP_W
N_GROUPS = 4
EXP_PER_GROUP = 4
N_EXPERTS = N_GROUPS * EXP_PER_GROUP
TOP_K = 2
D_EXPERT = 256
ROPE_BASE = 10000.0
EPS = 1e-6
NEG = -1e30

kernel_name = 'hybrid_diffusion_prefix_trunk_step'

F32 = jnp.float32


def rmsnorm(x, g):
    xf = x.astype(F32)
    y = xf * lax.rsqrt(jnp.mean(xf * xf, -1, keepdims=True) + EPS)
    return (y * g.astype(F32)).astype(x.dtype)


def modulation(cond, w_mod, b_mod):
    m = jax.nn.silu(cond) @ w_mod + b_mod
    return jnp.split(m[:, None, :], 6, axis=-1)


def project_in(h, w_in):
    b, L = h.shape[:2]
    sizes = (A_HEADS * HEAD_DIM, A_KV_HEADS * HEAD_DIM, A_KV_HEADS * HEAD_DIM,
             DIFF_HEADS * 2 * HEAD_DIM, DIFF_HEADS * 2 * HEAD_DIM, DIFF_HEADS * DIFF_V_DIM,
             NA_HEADS * HEAD_DIM, NA_HEADS * HEAD_DIM, NA_HEADS * HEAD_DIM, F_GROUPS * F_GROUP_W)
    splits = np.cumsum(sizes)[:-1].tolist()
    a_q, a_k, a_v, b_q, b_k, b_v, c_q, c_k, c_v, f = jnp.split(h @ w_in, splits, axis=-1)
    return (a_q.reshape(b, L, A_KV_HEADS, A_GROUP, HEAD_DIM),
            a_k.reshape(b, L, A_KV_HEADS, HEAD_DIM), a_v.reshape(b, L, A_KV_HEADS, HEAD_DIM),
            b_q.reshape(b, L, DIFF_HEADS, 2, HEAD_DIM), b_k.reshape(b, L, DIFF_HEADS, 2, HEAD_DIM),
            b_v.reshape(b, L, DIFF_HEADS, DIFF_V_DIM),
            c_q.reshape(b, L, NA_HEADS, HEAD_DIM), c_k.reshape(b, L, NA_HEADS, HEAD_DIM),
            c_v.reshape(b, L, NA_HEADS, HEAD_DIM), f.reshape(b, L, F_GROUPS, F_GROUP_W))


def rope_2d_tables(L):
    nf = HEAD_DIM // 4
    t = jnp.arange(L)
    inv = ROPE_BASE ** (-jnp.arange(nf, dtype=F32) / nf)
    pos = jnp.stack([t // GRID_W, t % GRID_W], -1).astype(F32)
    ang = pos[:, :, None] * inv
    return jnp.cos(ang), jnp.sin(ang)


def rope_2d(x, cos, sin):
    shp = x.shape
    nf = HEAD_DIM // 4
    xr = x.reshape(*shp[:-1], 2, 2, nf)
    ex = (1,) * (len(shp) - 3)
    c = cos.reshape(1, shp[1], *ex, 2, nf).astype(x.dtype)
    s = sin.reshape(1, shp[1], *ex, 2, nf).astype(x.dtype)
    x1 = xr[..., 0, :]
    x2 = xr[..., 1, :]
    return jnp.stack([x1 * c - x2 * s, x2 * c + x1 * s], axis=-2).reshape(shp)


def map_blocks(fn, q, blk):
    b, L = q.shape[:2]
    nb = L // blk
    qb = jnp.moveaxis(q.reshape(b, nb, blk, *q.shape[2:]), 1, 0)
    out = lax.map(lambda a: fn(a[0], a[1]), (qb, jnp.arange(nb)))
    return jnp.moveaxis(out, 0, 1).reshape(b, L, *out.shape[3:])


def sink_softmax(s, sink):
    sk = jnp.broadcast_to(sink.astype(F32).reshape(1, A_KV_HEADS, A_GROUP, 1, 1), s.shape[:-1] + (1,))
    return jax.nn.softmax(jnp.concatenate([s, sk], -1), -1)[..., :-1]


def sink_gqa_context(q, k, v, sink):
    scale = HEAD_DIM ** -0.5

    def blk(qb, i):
        s = jnp.einsum('bqkgd,bskd->bkgqs', qb, k, preferred_element_type=F32) * scale
        p = sink_softmax(s, sink).astype(v.dtype)
        return jnp.einsum('bkgqs,bskd->bqkgd', p, v)

    return map_blocks(blk, q, BLK)


def sink_gqa_latent(q, k, v, kc, vc, sink):
    b, L = k.shape[:2]
    lc = kc.shape[1]
    scale = HEAD_DIM ** -0.5
    pad = ((0, 0), (WIN, WIN), (0, 0), (0, 0))
    kp = jnp.pad(k, pad)
    vp = jnp.pad(v, pad)
    band = BLK + 2 * WIN
    qi = jnp.arange(BLK)[:, None]
    kj = jnp.arange(band)[None, :]

    def blk(qb, i):
        start = i * BLK
        kb = lax.dynamic_slice_in_dim(kp, start, band, axis=1)
        vb = lax.dynamic_slice_in_dim(vp, start, band, axis=1)
        qpos = start + qi
        kpos = start - WIN + kj
        valid = (kpos >= 0) & (kpos < L) & (jnp.abs(qpos - kpos) <= WIN)
        s_c = jnp.einsum('bqkgd,bskd->bkgqs', qb, kc, preferred_element_type=F32) * scale
        s_l = jnp.einsum('bqkgd,bskd->bkgqs', qb, kb, preferred_element_type=F32) * scale
        s_l = jnp.where(valid, s_l, NEG)
        p = sink_softmax(jnp.concatenate([s_c, s_l], -1), sink).astype(v.dtype)
        return (jnp.einsum('bkgqs,bskd->bqkgd', p[..., :lc], vc)
                + jnp.einsum('bkgqs,bskd->bqkgd', p[..., lc:], vb))

    return map_blocks(blk, q, BLK)


def diff_lambda_full(lam, lam_init):
    lam = lam.astype(F32)
    return jnp.exp(jnp.sum(lam[0] * lam[1])) - jnp.exp(jnp.sum(lam[2] * lam[3])) + lam_init


def diff_attention(q, k, v, lam_full):
    scale = HEAD_DIM ** -0.5

    def blk(qb, i):
        s = jnp.einsum('bqhmd,bshmd->bhmqs', qb, k, preferred_element_type=F32) * scale
        p = jax.nn.softmax(s, -1)
        w = (p[:, :, 0] - lam_full * p[:, :, 1]).astype(v.dtype)
        return jnp.einsum('bhqs,bshe->bqhe', w, v)

    return map_blocks(blk, q, BLK)


def dense_attention(q, k, v):
    scale = HEAD_DIM ** -0.5

    def blk(qb, i):
        s = jnp.einsum('bqhd,bshd->bhqs', qb, k, preferred_element_type=F32) * scale
        p = jax.nn.softmax(s, -1).astype(v.dtype)
        return jnp.einsum('bhqs,bshd->bqhd', p, v)

    return map_blocks(blk, q, BLK)


def na_latent(q, k, v, kc, vc, rpb):
    b, L = q.shape[:2]
    rows = L // GRID_W
    wr = min(NA_ROWS, rows)
    lc = kc.shape[1]
    scale = HEAD_DIM ** -0.5
    kg = k.reshape(b, rows, GRID_W, NA_HEADS, HEAD_DIM)
    vg = v.reshape(b, rows, GRID_W, NA_HEADS, HEAD_DIM)
    qcol = jnp.arange(GRID_W)
    cs = jnp.clip(qcol - NA_COLS // 2, 0, GRID_W - NA_COLS)
    kcol = jnp.arange(GRID_W)
    col_ok = (kcol[None, :] >= cs[:, None]) & (kcol[None, :] < cs[:, None] + NA_COLS)
    mask = jnp.broadcast_to(col_ok[:, None, :], (GRID_W, wr, GRID_W)).reshape(GRID_W, wr * GRID_W)
    dc = jnp.clip(kcol[None, :] - qcol[:, None] + NA_COLS - 1, 0, 2 * NA_COLS - 2)

    def blk(qb, r):
        rs = jnp.clip(r - wr // 2, 0, rows - wr)
        kb = lax.dynamic_slice_in_dim(kg, rs, wr, axis=1).reshape(b, wr * GRID_W, NA_HEADS, HEAD_DIM)
        vb = lax.dynamic_slice_in_dim(vg, rs, wr, axis=1).reshape(b, wr * GRID_W, NA_HEADS, HEAD_DIM)
        dr = rs - r + jnp.arange(wr) + NA_ROWS - 1
        bias = rpb[:, dr[None, :, None], dc[:, None, :]].reshape(NA_HEADS, GRID_W, wr * GRID_W)
        s_c = jnp.einsum('bqhd,bshd->bhqs', qb, kc, preferred_element_type=F32) * scale
        s_l = jnp.einsum('bqhd,bshd->bhqs', qb, kb, preferred_element_type=F32) * scale + bias.astype(F32)
        s_l = jnp.where(mask, s_l, NEG)
        p = jax.nn.softmax(jnp.concatenate([s_c, s_l], -1), -1).astype(v.dtype)
        return (jnp.einsum('bhqs,bshd->bqhd', p[..., :lc], vc)
                + jnp.einsum('bhqs,bshd->bqhd', p[..., lc:], vb))

    return map_blocks(blk, q, GRID_W)


def fourier_mix(f):
    y = jnp.fft.fft2(f.astype(F32), axes=(1, 3), norm='ortho')
    return jnp.real(y).astype(f.dtype)


def merge_branches(h, outs, w_branch, w_gate, b_gate, w_out):
    b, L = h.shape[:2]
    br = jnp.stack([o.reshape(b, L, BR_W) for o in outs], axis=2)
    proj = jnp.einsum('blnc,ncd->blnd', br, w_branch)
    gate = jax.nn.sigmoid(h @ w_gate + b_gate).reshape(b, L, N_BRANCH, D_MODEL)
    return jnp.einsum('blnd,blnd->bld', gate, proj) @ w_out


def hier_moe(h, w_rg, b_rg, w_re, b_re, w_eg, w_eu, w_ed):
    shp = h.shape
    t = h.reshape(-1, D_MODEL)
    tf = t.astype(F32)
    pg = jax.nn.softmax(tf @ w_rg.astype(F32) + b_rg.astype(F32), -1)
    pg_top, g_idx = lax.top_k(pg, 1)
    le = (tf @ w_re.astype(F32) + b_re.astype(F32)).reshape(-1, N_GROUPS, EXP_PER_GROUP)
    le_sel = jnp.einsum('tg,tge->te', jax.nn.one_hot(g_idx[:, 0], N_GROUPS, dtype=F32), le)
    pe_top, e_idx = lax.top_k(jax.nn.softmax(le_sel, -1), TOP_K)
    w = pe_top / jnp.sum(pe_top, -1, keepdims=True) * pg_top
    eid = g_idx * EXP_PER_GROUP + e_idx
    combine = jnp.einsum('tk,tke->te', w, jax.nn.one_hot(eid, N_EXPERTS, dtype=F32))
    hid = jax.nn.silu(jnp.einsum('td,edf->tef', t, w_eg)) * jnp.einsum('td,edf->tef', t, w_eu)
    out = jnp.einsum('tef,efd->td', hid * combine[:, :, None].astype(hid.dtype), w_ed)
    return out.reshape(shp)


def layer(x, cond, lp, lam_init, cache, rope):
    sh1, sc1, g1, sh2, sc2, g2 = modulation(cond, lp['w_mod'], lp['b_mod'])
    h = rmsnorm(x, lp['norm1_g']) * (1 + sc1) + sh1
    a_q, a_k, a_v, b_q, b_k, b_v, c_q, c_k, c_v, f = project_in(h, lp['w_in'])
    lam_full = diff_lambda_full(lp['diff_lambda'], lam_init)
    if cache is None:
        o_a = sink_gqa_context(a_q, a_k, a_v, lp['sink'])
        o_b = diff_attention(b_q, b_k, b_v, lam_full)
        o_c = dense_attention(c_q, c_k, c_v)
        ctx = (a_k, a_v, b_k, b_v, c_k, c_v)
    else:
        cos, sin = rope
        ak_c, av_c, bk_c, bv_c, ck_c, cv_c = cache
        o_a = sink_gqa_latent(rope_2d(a_q, cos, sin), rope_2d(a_k, cos, sin), a_v, ak_c, av_c, lp['sink'])
        k_all = jnp.concatenate([bk_c, rope_2d(b_k, cos, sin)], axis=1)
        v_all = jnp.concatenate([bv_c, b_v], axis=1)
        o_b = diff_attention(rope_2d(b_q, cos, sin), k_all, v_all, lam_full)
        o_c = na_latent(c_q, c_k, c_v, ck_c, cv_c, lp['na_rpb'])
        ctx = None
    o_b = rmsnorm(o_b, lp['diff_sub_g']) * (1.0 - lam_init)
    o_f = fourier_mix(f)
    x = x + g1 * merge_branches(h, (o_a, o_b, o_c, o_f), lp['w_branch'], lp['w_gate'], lp['b_gate'], lp['w_out'])
    h2 = rmsnorm(x, lp['norm2_g']) * (1 + sc2) + sh2
    x = x + g2 * hier_moe(h2, lp['w_rg'], lp['b_rg'], lp['w_re'], lp['b_re'], lp['w_eg'], lp['w_eu'], lp['w_ed'])
    return x, ctx


def setup_inputs(seed: int = 0) -> dict:
    key = jax.random.key(seed)
    ks = iter(jax.random.split(key, 40))

    def nrm(shape, scale=1.0):
        return jax.random.normal(next(ks), shape, F32) * scale

    D = D_MODEL
    return {
        'x_prompt': nrm((BATCH, SEQ, D)),
        'x_sample': nrm((DEC_BATCH, DEC_SEQ, D)),
        'cache_attn_k': nrm((DEC_BATCH, DEPTH, PAST_LEN, A_KV_HEADS, HEAD_DIM)),
        'cache_attn_v': nrm((DEC_BATCH, DEPTH, PAST_LEN, A_KV_HEADS, HEAD_DIM)),
        'cache_diff_k': nrm((DEC_BATCH, DEPTH, PAST_LEN, DIFF_HEADS, 2, HEAD_DIM)),
        'cache_diff_v': nrm((DEC_BATCH, DEPTH, PAST_LEN, DIFF_HEADS, DIFF_V_DIM)),
        'cache_na_k': nrm((DEC_BATCH, DEPTH, PAST_LEN, NA_HEADS, HEAD_DIM)),
        'cache_na_v': nrm((DEC_BATCH, DEPTH, PAST_LEN, NA_HEADS, HEAD_DIM)),
        'c': nrm((DEC_BATCH, D)),
        'c_ctx': nrm((D,)),
        'w_mod': nrm((DEPTH, D, 6 * D), D ** -0.5),
        'b_mod': nrm((DEPTH, 6 * D), 0.01),
        'norm1_g': 1.0 + nrm((DEPTH, D), 0.05),
        'w_in': nrm((DEPTH, D, IN_W), D ** -0.5),
        'attn_sink': nrm((DEPTH, A_HEADS)),
        'diff_lambda': nrm((DEPTH, 4, HEAD_DIM), 0.1),
        'diff_sub_g': 1.0 + nrm((DEPTH, DIFF_V_DIM), 0.05),
        'na_rpb': nrm((DEPTH, NA_HEADS, 2 * NA_ROWS - 1, 2 * NA_COLS - 1), 0.1),
        'w_branch': nrm((DEPTH, N_BRANCH, BR_W, D), BR_W ** -0.5),
        'w_gate': nrm((DEPTH, D, N_BRANCH * D), D ** -0.5),
        'b_gate': nrm((DEPTH, N_BRANCH * D), 0.01),
        'w_out': nrm((DEPTH, D, D), D ** -0.5),
        'norm2_g': 1.0 + nrm((DEPTH, D), 0.05),
        'w_router_group': nrm((DEPTH, D, N_GROUPS), D ** -0.5),
        'b_router_group': nrm((DEPTH, N_GROUPS), 0.01),
        'w_router_expert': nrm((DEPTH, D, N_EXPERTS), D ** -0.5),
        'b_router_expert': nrm((DEPTH, N_EXPERTS), 0.01),
        'w_exp_gate': nrm((DEPTH, N_EXPERTS, D, D_EXPERT), D ** -0.5),
        'w_exp_up': nrm((DEPTH, N_EXPERTS, D, D_EXPERT), D ** -0.5),
        'w_exp_down': nrm((DEPTH, N_EXPERTS, D_EXPERT, D), D_EXPERT ** -0.5),
        'final_g': 1.0 + nrm((D,), 0.05),
    }


def reference(x_prompt, x_sample, cache_attn_k, cache_attn_v, cache_diff_k, cache_diff_v, cache_na_k, cache_na_v,
              c, c_ctx, w_mod, b_mod, norm1_g, w_in, attn_sink, diff_lambda, diff_sub_g, na_rpb, w_branch, w_gate,
              b_gate, w_out, norm2_g, w_router_group, b_router_group, w_router_expert, b_router_expert,
              w_exp_gate, w_exp_up, w_exp_down, final_g):
    xp = x_prompt
    xs = x_sample
    rope = rope_2d_tables(x_sample.shape[1])
    new_ak, new_av, new_bk, new_bv, new_ck, new_cv = [], [], [], [], [], []
    for l in range(DEPTH):
        lp = {'w_mod': w_mod[l], 'b_mod': b_mod[l], 'norm1_g': norm1_g[l], 'w_in': w_in[l],
              'sink': attn_sink[l], 'diff_lambda': diff_lambda[l], 'diff_sub_g': diff_sub_g[l],
              'na_rpb': na_rpb[l], 'w_branch': w_branch[l], 'w_gate': w_gate[l], 'b_gate': b_gate[l],
              'w_out': w_out[l], 'norm2_g': norm2_g[l], 'w_rg': w_router_group[l], 'b_rg': b_router_group[l],
              'w_re': w_router_expert[l], 'b_re': b_router_expert[l], 'w_eg': w_exp_gate[l],
              'w_eu': w_exp_up[l], 'w_ed': w_exp_down[l]}
        lam_init = 0.8 - 0.6 * math.exp(-0.3 * l)
        xp, ctx = layer(xp, c_ctx[None, :], lp, lam_init, None, None)
        new_ak.append(ctx[0])
        new_av.append(ctx[1])
        new_bk.append(ctx[2])
        new_bv.append(ctx[3])
        new_ck.append(ctx[4])
        new_cv.append(ctx[5])
        cache_l = (cache_attn_k[:, l], cache_attn_v[:, l], cache_diff_k[:, l], cache_diff_v[:, l],
                   cache_na_k[:, l], cache_na_v[:, l])
        xs, _ = layer(xs, c, lp, lam_init, cache_l, rope)
    y_prompt = rmsnorm(xp, final_g)
    y_sample = rmsnorm(xs, final_g)
    new_attn_k = jnp.stack(new_ak, axis=1)
    new_attn_v = jnp.stack(new_av, axis=1)
    new_diff_k = jnp.stack(new_bk, axis=1)
    new_diff_v = jnp.stack(new_bv, axis=1)
    new_na_k = jnp.stack(new_ck, axis=1)
    new_na_v = jnp.stack(new_cv, axis=1)
    return (y_prompt, y_sample, new_attn_k, new_attn_v, new_diff_k, new_diff_v, new_na_k, new_na_v)
```

```python
import functools
import math

import jax
import jax.numpy as jnp
import numpy as np
from jax import lax
from jax.experimental import pallas as pl
from jax.experimental.pallas import tpu as pltpu

F32 = jnp.float32
BF16 = jnp.bfloat16

D_MODEL = 2048
DEPTH = 2
GRID_W = 64
HEAD_DIM = 64
A_HEADS = 8
A_KV_HEADS = 2
A_GROUP = A_HEADS // A_KV_HEADS
WIN = 128
DIFF_HEADS = 4
DIFF_V_DIM = 2 * HEAD_DIM
NA_HEADS = 8
NA_ROWS = 8
NA_COLS = 16
F_GROUPS = 4
F_GROUP_W = 128
N_BRANCH = 4
BR_W = A_HEADS * HEAD_DIM
N_GROUPS = 4
EXP_PER_GROUP = 4
N_EXPERTS = N_GROUPS * EXP_PER_GROUP
D_EXPERT = 256
ROPE_BASE = 10000.0
EPS = 1e-6
NEG = -1e30
SCALE = HEAD_DIM ** -0.5

_IN_GROUPS = (
    ("a_q", 0, 512), ("a_k", 512, 128), ("a_v", 640, 128),
    ("b_q", 768, 512), ("b_k", 1280, 512), ("b_v", 1792, 512),
    ("c_q", 2304, 512), ("c_k", 2816, 512), ("c_v", 3328, 512),
    ("f", 3840, 512),
)
IN_W = 4352
_Q_GROUPS = ("a_q", "b_q", "c_q")
_ROPED = ("a_q", "a_k", "b_q", "b_k")
_CTX_F32 = ("a_k", "a_v", "b_k", "b_v", "c_k", "c_v")

LANES = 128
VMEM_LIMIT = 56 * 1024 * 1024

TM_PROJ = 512
TQ = 128
TN_MERGE = 256
TN_MOD = 1024
ROUTER_W = 128


def _cparams(sem):
    return pltpu.CompilerParams(dimension_semantics=sem, vmem_limit_bytes=VMEM_LIMIT)


def _sigmoid(x):
    return 1.0 / (1.0 + jnp.exp(-x))


def _norm_mod(x, g, shift, scale):
    ms = jnp.mean(x * x, axis=-1, keepdims=True)
    return (x * lax.rsqrt(ms + EPS) * g) * (1.0 + scale) + shift


def _mod_kernel(c_ref, w_ref, b_ref, o_ref):
    c = c_ref[...]
    s = (c * _sigmoid(c)).astype(BF16)
    o_ref[0] = jnp.dot(s, w_ref[0].astype(BF16), preferred_element_type=F32) + b_ref[0]


def _modulation(cond, w_mod, b_mod):
    n = cond.shape[0]
    out_w = w_mod.shape[-1]
    return pl.pallas_call(
        _mod_kernel,
        out_shape=jax.ShapeDtypeStruct((DEPTH, n, out_w), F32),
        grid=(DEPTH, out_w // TN_MOD),
        in_specs=[
            pl.BlockSpec((n, D_MODEL), lambda l, j: (0, 0)),
            pl.BlockSpec((1, D_MODEL, TN_MOD), lambda l, j: (l, 0, j)),
            pl.BlockSpec((1, 1, TN_MOD), lambda l, j: (l, 0, j)),
        ],
        out_specs=pl.BlockSpec((1, n, TN_MOD), lambda l, j: (l, 0, j)),
        compiler_params=_cparams(("parallel", "parallel")),
        name="modulation",
    )(cond, w_mod, b_mod.reshape(DEPTH, 1, out_w))


def _rope128(y, cos, sin):
    lane = lax.broadcasted_iota(jnp.int32, y.shape, 1)
    first = (lane % 32) < 16
    partner = jnp.where(first, pltpu.roll(y, LANES - 16, 1), pltpu.roll(y, 16, 1))
    return y * cos + partner * sin


def _inproj_kernel(*refs, latent):
    if latent:
        x_ref, mod_ref, g_ref, w_ref, cos_ref, sin_ref = refs[:6]
        out_refs = refs[6:]
    else:
        x_ref, mod_ref, g_ref, w_ref = refs[:4]
        out_refs = refs[4:]
    h = _norm_mod(x_ref[...], g_ref[...], mod_ref[0, 0:1, :], mod_ref[0, 1:2, :]).astype(BF16)
    for (name, start, width), o_ref in zip(_IN_GROUPS, out_refs):
        y = jnp.dot(h, w_ref[:, start:start + width], preferred_element_type=F32)
        if name in _Q_GROUPS:
            y = y * SCALE
        if latent and name in _ROPED:
            cos = cos_ref[...]
            sin = sin_ref[...]
            for c0 in range(0, width, LANES):
                o_ref[:, c0:c0 + LANES] = _rope128(y[:, c0:c0 + LANES], cos, sin).astype(o_ref.dtype)
        else:
            o_ref[...] = y.astype(o_ref.dtype)


def _inproj(x2d, mod, g, w_in_bf, tokens_per_mod, rope=None):
    t = x2d.shape[0]
    tm = TM_PROJ
    latent = rope is not None
    in_specs = [
        pl.BlockSpec((tm, D_MODEL), lambda i: (i, 0)),
        pl.BlockSpec((1, 6, D_MODEL), lambda i: ((i * tm) // tokens_per_mod, 0, 0)),
        pl.BlockSpec((1, D_MODEL), lambda i: (0, 0)),
        pl.BlockSpec((D_MODEL, IN_W), lambda i: (0, 0), pipeline_mode=pl.Buffered(1)),
    ]
    args = [x2d, mod, g.reshape(1, D_MODEL), w_in_bf]
    if latent:
        seq = rope[0].shape[0]
        nblk = seq // tm
        in_specs += [pl.BlockSpec((tm, LANES), lambda i: (i % nblk, 0))] * 2
        args += list(rope)
    out_shape, out_specs = [], []
    for name, _, width in _IN_GROUPS:
        dt = F32 if (not latent and name in _CTX_F32) else BF16
        out_shape.append(jax.ShapeDtypeStruct((t, width), dt))
        out_specs.append(pl.BlockSpec((tm, width), lambda i: (i, 0)))
    outs = pl.pallas_call(
        functools.partial(_inproj_kernel, latent=latent),
        out_shape=out_shape,
        grid=(t // tm,),
        in_specs=in_specs,
        out_specs=out_specs,
        compiler_params=_cparams(("parallel",)),
        name="inproj_latent" if latent else "inproj_context",
    )(*args)
    return dict(zip([n for n, _, _ in _IN_GROUPS], outs))


def _attend(q, segs, sink_col=None):
    scores = []
    for k, _, bias in segs:
        s = lax.dot_general(q, k, (((1,), (1,)), ((), ())), preferred_element_type=F32)
        if bias is not None:
            s = s + bias
        scores.append(s)
    m = scores[0].max(axis=-1, keepdims=True)
    for s in scores[1:]:
        m = jnp.maximum(m, s.max(axis=-1, keepdims=True))
    if sink_col is not None:
        m = jnp.maximum(m, sink_col)
    acc = None
    den = None
    for s, (_, v, _) in zip(scores, segs):
        p = jnp.exp(s - m)
        ps = p.sum(axis=-1, keepdims=True)
        o = jnp.dot(p.astype(BF16), v, preferred_element_type=F32)
        acc = o if acc is None else acc + o
        den = ps if den is None else den + ps
    if sink_col is not None:
        den = den + jnp.exp(sink_col - m)
    return acc, den


def _head(x, h, width=HEAD_DIM):
    return x[:, h * width:(h + 1) * width]


def _sink_attn_kernel(*refs, latent, tq, seq):
    if latent:
        sink_ref, q_ref, k_ref, v_ref, kc_ref, vc_ref, o_ref = refs
    else:
        sink_ref, q_ref, k_ref, v_ref, o_ref = refs
    q_all = q_ref[...]
    if latent:
        band = tq + 2 * WIN
        qb = pl.program_id(1)
        start = jnp.clip(qb * tq - WIN, 0, seq - band)
        start = pl.multiple_of(start, TQ)
        k_loc = k_ref[pl.ds(start, band), :]
        v_loc = v_ref[pl.ds(start, band), :]
        k_ctx = kc_ref[...].astype(BF16)
        v_ctx = vc_ref[...].astype(BF16)
        rows = A_GROUP * tq
        qpos = qb * tq + lax.broadcasted_iota(jnp.int32, (rows, band), 0) % tq
        kpos = start + lax.broadcasted_iota(jnp.int32, (rows, band), 1)
        bias = jnp.where(jnp.abs(qpos - kpos) <= WIN, 0.0, NEG).astype(F32)
    else:
        k_loc = k_ref[...].astype(BF16)
        v_loc = v_ref[...].astype(BF16)
    outs = [None] * A_HEADS
    for kv in range(A_KV_HEADS):
        heads = [kv * A_GROUP + g for g in range(A_GROUP)]
        q = jnp.concatenate([_head(q_all, h) for h in heads], axis=0)
        sink_col = jnp.concatenate(
            [jnp.full((tq, 1), sink_ref[0, h], F32) for h in heads], axis=0)
        if latent:
            segs = [(_head(k_ctx, kv), _head(v_ctx, kv), None),
                    (_head(k_loc, kv), _head(v_loc, kv), bias)]
        else:
            segs = [(_head(k_loc, kv), _head(v_loc, kv), None)]
        acc, den = _attend(q, segs, sink_col)
        o = acc / den
        for g, h in enumerate(heads):
            outs[h] = o[g * tq:(g + 1) * tq]
    o_ref[...] = jnp.concatenate(outs, axis=1).astype(o_ref.dtype)


def _sink_attn(sink, q, k, v, *, nb, seq, cache=None, layer=0):
    latent = cache is not None
    tq = TQ if latent else seq
    nq = seq // tq
    kw = A_KV_HEADS * HEAD_DIM
    in_specs = [
        pl.BlockSpec(memory_space=pltpu.SMEM),
        pl.BlockSpec((tq, BR_W), lambda b, i: (b * nq + i, 0)),
        pl.BlockSpec((seq, kw), lambda b, i: (b, 0)),
        pl.BlockSpec((seq, kw), lambda b, i: (b, 0)),
    ]
    args = [sink.reshape(1, A_HEADS), q, k, v]
    if latent:
        past = cache[0].shape[2]
        in_specs += [pl.BlockSpec((None, None, past, kw), lambda b, i: (b, layer, 0, 0))] * 2
        args += list(cache)
    return pl.pallas_call(
        functools.partial(_sink_attn_kernel, latent=latent, tq=tq, seq=seq),
        out_shape=jax.ShapeDtypeStruct(q.shape, BF16),
        grid=(nb, nq),
        in_specs=in_specs,
        out_specs=pl.BlockSpec((tq, BR_W), lambda b, i: (b * nq + i, 0)),
        compiler_params=_cparams(("parallel", "arbitrary")),
        name="sink_attn_latent" if latent else "sink_attn_context",
    )(*args)


def _diff_attn_kernel(*refs, latent, lam_init):
    if latent:
        lam_ref, g_ref, q_ref, k_ref, v_ref, kc_ref, vc_ref, o_ref = refs
    else:
        lam_ref, g_ref, q_ref, k_ref, v_ref, o_ref = refs
    lam = lam_ref[...]
    lam_a = jnp.sum(lam[0:1] * lam[1:2], axis=-1, keepdims=True)
    lam_b = jnp.sum(lam[2:3] * lam[3:4], axis=-1, keepdims=True)
    lam_full = jnp.exp(lam_a) - jnp.exp(lam_b) + lam_init
    q_all = q_ref[...]
    tq = q_all.shape[0]
    k_segs = [k_ref[...].astype(BF16)]
    v_segs = [v_ref[...].astype(BF16)]
    if latent:
        k_segs = [kc_ref[...].astype(BF16)] + k_segs
        v_segs = [vc_ref[...].astype(BF16)] + v_segs
    outs = []
    for h in range(DIFF_HEADS):
        vs = [_head(v, h, DIFF_V_DIM) for v in v_segs]
        res = []
        for sub in range(2):
            hh = 2 * h + sub
            segs = [(_head(k, hh), v, None) for k, v in zip(k_segs, vs)]
            acc, den = _attend(_head(q_all, hh), segs)
            res.append(acc / den)
        o = res[0] - lam_full * res[1]
        ms = jnp.mean(o * o, axis=-1, keepdims=True)
        outs.append(o * lax.rsqrt(ms + EPS) * g_ref[...] * (1.0 - lam_init))
    o_ref[...] = jnp.concatenate(outs, axis=1).astype(o_ref.dtype)


def _diff_attn(lam, sub_g, q, k, v, *, nb, seq, lam_init, cache=None, layer=0):
    latent = cache is not None
    tq = TQ if latent else seq
    nq = seq // tq
    kw = DIFF_HEADS * 2 * HEAD_DIM
    in_specs = [
        pl.BlockSpec((4, HEAD_DIM), lambda b, i: (0, 0)),
        pl.BlockSpec((1, DIFF_V_DIM), lambda b, i: (0, 0)),
        pl.BlockSpec((tq, BR_W), lambda b, i: (b * nq + i, 0)),
        pl.BlockSpec((seq, kw), lambda b, i: (b, 0)),
        pl.BlockSpec((seq, kw), lambda b, i: (b, 0)),
    ]
    args = [lam, sub_g.reshape(1, DIFF_V_DIM), q, k, v]
    if latent:
        past = cache[0].shape[2]
        in_specs += [pl.BlockSpec((None, None, past, kw), lambda b, i: (b, layer, 0, 0))] * 2
        args += list(cache)
    return pl.pallas_call(
        functools.partial(_diff_attn_kernel, latent=latent, lam_init=lam_init),
        out_shape=jax.ShapeDtypeStruct(q.shape, BF16),
        grid=(nb, nq),
        in_specs=in_specs,
        out_specs=pl.BlockSpec((tq, BR_W), lambda b, i: (b * nq + i, 0)),
        compiler_params=_cparams(("parallel", "arbitrary")),
        name="diff_attn_latent" if latent else "diff_attn_context",
    )(*args)


def _dense_attn_kernel(q_ref, k_ref, v_ref, o_ref):
    q_all = q_ref[...]
    k = k_ref[...].astype(BF16)
    v = v_ref[...].astype(BF16)
    outs = []
    for h in range(NA_HEADS):
        acc, den = _attend(_head(q_all, h), [(_head(k, h), _head(v, h), None)])
        outs.append(acc / den)
    o_ref[...] = jnp.concatenate(outs, axis=1).astype(o_ref.dtype)


def _dense_attn(q, k, v, *, nb, seq):
    spec = pl.BlockSpec((seq, BR_W), lambda b: (b, 0))
    return pl.pallas_call(
        _dense_attn_kernel,
        out_shape=jax.ShapeDtypeStruct(q.shape, BF16),
        grid=(nb,),
        in_specs=[spec, spec, spec],
        out_specs=spec,
        compiler_params=_cparams(("parallel",)),
        name="dense_attn_context",
    )(q, k, v)


def _na_window_start(r, rows):
    return jnp.clip(r - NA_ROWS // 2, 0, rows - NA_ROWS)


def _na_attn_kernel(q_ref, k_ref, v_ref, kc_ref, vc_ref, bias_ref, o_ref, *, rows):
    r = pl.program_id(1)
    start = pl.multiple_of(_na_window_start(r, rows) * GRID_W, GRID_W)
    band = NA_ROWS * GRID_W
    q_all = q_ref[...]
    k_loc = k_ref[pl.ds(start, band), :]
    v_loc = v_ref[pl.ds(start, band), :]
    k_ctx = kc_ref[...].astype(BF16)
    v_ctx = vc_ref[...].astype(BF16)
    outs = []
    for h in range(NA_HEADS):
        segs = [(_head(k_ctx, h), _head(v_ctx, h), None),
                (_head(k_loc, h), _head(v_loc, h), bias_ref[h, 0])]
        acc, den = _attend(_head(q_all, h), segs)
        outs.append(acc / den)
    o_ref[...] = jnp.concatenate(outs, axis=1).astype(o_ref.dtype)


def _na_attn(q, k, v, cache, bias_tab, *, nb, seq, layer):
    rows = seq // GRID_W
    band = NA_ROWS * GRID_W
    past = cache[0].shape[2]

    def bias_index(b, r):
        return (0, _na_window_start(r, rows) - r + NA_ROWS - 1, 0, 0)

    return pl.pallas_call(
        functools.partial(_na_attn_kernel, rows=rows),
        out_shape=jax.ShapeDtypeStruct(q.shape, BF16),
        grid=(nb, rows),
        in_specs=[
            pl.BlockSpec((GRID_W, BR_W), lambda b, r: (b * rows + r, 0)),
            pl.BlockSpec((seq, BR_W), lambda b, r: (b, 0)),
            pl.BlockSpec((seq, BR_W), lambda b, r: (b, 0)),
            pl.BlockSpec((None, None, past, BR_W), lambda b, r: (b, layer, 0, 0)),
            pl.BlockSpec((None, None, past, BR_W), lambda b, r: (b, layer, 0, 0)),
            pl.BlockSpec((NA_HEADS, 1, GRID_W, band), bias_index),
        ],
        out_specs=pl.BlockSpec((GRID_W, BR_W), lambda b, r: (b * rows + r, 0)),
        compiler_params=_cparams(("parallel", "arbitrary")),
        name="na_attn_latent",
    )(q, k, v, cache[0], cache[1], bias_tab)


def _rpb_expand_kernel(rpb_ref, onehot_ref, o_ref):
    o_ref[...] = jnp.dot(rpb_ref[...], onehot_ref[...], preferred_element_type=F32,
                         precision=lax.Precision.HIGHEST)


def _na_bias_table(rpb):
    n_dr, n_dc = 2 * NA_ROWS - 1, 2 * NA_COLS - 1
    qcol = np.arange(GRID_W)
    kcol = np.arange(GRID_W)
    cs = np.clip(qcol - NA_COLS // 2, 0, GRID_W - NA_COLS)
    col_ok = (kcol[None, :] >= cs[:, None]) & (kcol[None, :] < cs[:, None] + NA_COLS)
    dc = np.clip(kcol[None, :] - qcol[:, None] + NA_COLS - 1, 0, n_dc - 1)
    onehot = np.zeros((32, GRID_W * GRID_W), np.float32)
    onehot[dc.reshape(-1), np.arange(GRID_W * GRID_W)] = 1.0
    rpb2 = jnp.pad(rpb.reshape(NA_HEADS * n_dr, n_dc), ((0, 0), (0, 32 - n_dc)))
    full = pl.pallas_call(
        _rpb_expand_kernel,
        out_shape=jax.ShapeDtypeStruct((NA_HEADS * n_dr, GRID_W * GRID_W), F32),
        name="rpb_expand",
    )(rpb2, jnp.asarray(onehot))
    full = full.reshape(NA_HEADS, n_dr, GRID_W, GRID_W)
    full = full + jnp.asarray(np.where(col_ok, 0.0, NEG).astype(np.float32))
    tab = jnp.stack([full[:, o:o + NA_ROWS] for o in range(NA_ROWS)], axis=1)
    tab = jnp.transpose(tab, (0, 1, 3, 2, 4))
    return tab.reshape(NA_HEADS, NA_ROWS, GRID_W, NA_ROWS * GRID_W)


def _dft_tables(seq):
    pos = np.arange(seq)
    ang = 2.0 * np.pi * ((pos[:, None] * pos[None, :]) % seq) / seq
    left = np.concatenate([np.cos(ang), -np.sin(ang)], axis=1).astype(np.float32)
    ch = np.arange(F_GROUP_W)
    angw = 2.0 * np.pi * ((ch[:, None] * ch[None, :]) % F_GROUP_W) / F_GROUP_W
    right = np.concatenate([np.cos(angw), np.sin(angw)], axis=1).astype(np.float32)
    return jnp.asarray(left).astype(BF16), jnp.asarray(right).astype(BF16)


def _fourier_kernel(f_ref, left_ref, right_ref, o_ref, stack_ref, *, seq):
    f = f_ref[...]
    right = right_ref[...]
    for g in range(F_GROUPS):
        t = jnp.dot(_head(f, g, F_GROUP_W), right, preferred_element_type=F32).astype(BF16)
        stack_ref[0:seq, g * F_GROUP_W:(g + 1) * F_GROUP_W] = t[:, :F_GROUP_W]
        stack_ref[seq:2 * seq, g * F_GROUP_W:(g + 1) * F_GROUP_W] = t[:, F_GROUP_W:]
    y = jnp.dot(left_ref[...], stack_ref[...], preferred_element_type=F32)
    o_ref[...] = (y * (1.0 / math.sqrt(seq * F_GROUP_W))).astype(o_ref.dtype)


def _fourier(f, *, nb, seq):
    left, right = _dft_tables(seq)
    w = F_GROUPS * F_GROUP_W
    return pl.pallas_call(
        functools.partial(_fourier_kernel, seq=seq),
        out_shape=jax.ShapeDtypeStruct(f.shape, BF16),
        grid=(nb,),
        in_specs=[
            pl.BlockSpec((seq, w), lambda b: (b, 0)),
            pl.BlockSpec((seq, 2 * seq), lambda b: (0, 0)),
            pl.BlockSpec((F_GROUP_W, 2 * F_GROUP_W), lambda b: (0, 0)),
        ],
        out_specs=pl.BlockSpec((seq, w), lambda b: (b, 0)),
        scratch_shapes=[pltpu.VMEM((2 * seq, w), BF16)],
        compiler_params=_cparams(("parallel",)),
        name=f"fourier_{seq}",
    )(f, left, right)


def _merge_kernel(x_ref, mod_ref, g_ref, oa_ref, ob_ref, oc_ref, of_ref,
                  wbr_ref, wg_ref, bg_ref, wo_ref, o_ref, h_ref, acc_ref):
    j = pl.program_id(1)
    tn = TN_MERGE

    @pl.when(j == 0)
    def _():
        h_ref[...] = _norm_mod(x_ref[...], g_ref[...], mod_ref[0, 0:1, :],
                               mod_ref[0, 1:2, :]).astype(BF16)
        acc_ref[...] = jnp.zeros_like(acc_ref)

    gate = _sigmoid(jnp.dot(h_ref[...], wg_ref[...], preferred_element_type=F32) + bg_ref[0])
    mix = None
    for n, br_ref in enumerate((oa_ref, ob_ref, oc_ref, of_ref)):
        proj = jnp.dot(br_ref[...], wbr_ref[n], preferred_element_type=F32)
        term = gate[:, n * tn:(n + 1) * tn] * proj
        mix = term if mix is None else mix + term
    acc_ref[...] += jnp.dot(mix.astype(BF16), wo_ref[...], preferred_element_type=F32)

    @pl.when(j == pl.num_programs(1) - 1)
    def _():
        o_ref[...] = x_ref[...] + mod_ref[0, 2:3, :] * acc_ref[...]


def _merge(x2d, mod, g, branches, w_branch_bf, w_gate_bf, b_gate_r, w_out_bf, tokens_per_mod):
    t = x2d.shape[0]
    tm, tn = TM_PROJ, TN_MERGE
    nch = D_MODEL // tn
    br_spec = pl.BlockSpec((tm, BR_W), lambda i, j: (i, 0))
    return pl.pallas_call(
        _merge_kernel,
        out_shape=jax.ShapeDtypeStruct((t, D_MODEL), F32),
        grid=(t // tm, nch),
        in_specs=[
            pl.BlockSpec((tm, D_MODEL), lambda i, j: (i, 0)),
            pl.BlockSpec((1, 6, D_MODEL), lambda i, j: ((i * tm) // tokens_per_mod, 0, 0)),
            pl.BlockSpec((1, D_MODEL), lambda i, j: (0, 0)),
            br_spec, br_spec, br_spec, br_spec,
            pl.BlockSpec((N_BRANCH, BR_W, tn), lambda i, j: (0, 0, j)),
            pl.BlockSpec((D_MODEL, N_BRANCH * tn), lambda i, j: (0, j)),
            pl.BlockSpec((1, 1, N_BRANCH * tn), lambda i, j: (j, 0, 0)),
            pl.BlockSpec((tn, D_MODEL), lambda i, j: (j, 0)),
        ],
        out_specs=pl.BlockSpec((tm, D_MODEL), lambda i, j: (i, 0)),
        scratch_shapes=[pltpu.VMEM((tm, D_MODEL), BF16), pltpu.VMEM((tm, D_MODEL), F32)],
        compiler_params=_cparams(("parallel", "arbitrary")),
        name="merge",
    )(x2d, mod, g.reshape(1, D_MODEL), *branches, w_branch_bf, w_gate_bf, b_gate_r, w_out_bf)


def _route(logits):
    lane = lax.broadcasted_iota(jnp.int32, logits.shape, 1)
    ninf = -jnp.inf
    lg = jnp.where(lane < N_GROUPS, logits, ninf)
    mg = lg.max(axis=-1, keepdims=True)
    pg_top = 1.0 / jnp.exp(lg - mg).sum(axis=-1, keepdims=True)
    g_idx = jnp.where(lg == mg, lane, ROUTER_W).min(axis=-1, keepdims=True)
    lo = N_GROUPS + EXP_PER_GROUP * g_idx
    le = jnp.where((lane >= lo) & (lane < lo + EXP_PER_GROUP), logits, ninf)
    m1 = le.max(axis=-1, keepdims=True)
    e1 = jnp.where(le == m1, lane, ROUTER_W).min(axis=-1, keepdims=True)
    le2 = jnp.where(lane == e1, ninf, le)
    m2 = le2.max(axis=-1, keepdims=True)
    e2 = jnp.where(le2 == m2, lane, ROUTER_W).min(axis=-1, keepdims=True)
    se = jnp.exp(le - m1).sum(axis=-1, keepdims=True)
    pe1 = 1.0 / se
    pe2 = jnp.exp(m2 - m1) / se
    tot = pe1 + pe2
    w1 = pe1 / tot * pg_top
    w2 = pe2 / tot * pg_top
    return jnp.where(lane == e1, w1, 0.0) + jnp.where(lane == e2, w2, 0.0)


def _moe_kernel(x_ref, mod_ref, g_ref, wr_ref, br_ref, wgu_ref, wd_ref, fg_ref,
                o_ref, h_ref, comb_ref, acc_ref, *, final):
    e = pl.program_id(1)

    @pl.when(e == 0)
    def _():
        h = _norm_mod(x_ref[...], g_ref[...], mod_ref[0, 3:4, :], mod_ref[0, 4:5, :])
        h_ref[...] = h.astype(BF16)
        logits = jnp.dot(h, wr_ref[...], preferred_element_type=F32,
                         precision=lax.Precision.HIGHEST) + br_ref[...]
        comb_ref[...] = _route(logits)
        acc_ref[...] = jnp.zeros_like(acc_ref)

    lane = lax.broadcasted_iota(jnp.int32, comb_ref.shape, 1)
    cw = jnp.where(lane == N_GROUPS + e, comb_ref[...], 0.0).sum(axis=-1, keepdims=True)
    gu = jnp.dot(h_ref[...], wgu_ref[0], preferred_element_type=F32)
    gate = gu[:, :D_EXPERT]
    hid = (gate * _sigmoid(gate)) * gu[:, D_EXPERT:] * cw
    acc_ref[...] += jnp.dot(hid.astype(BF16), wd_ref[0], preferred_element_type=F32)

    @pl.when(e == pl.num_programs(1) - 1)
    def _():
        y = x_ref[...] + mod_ref[0, 5:6, :] * acc_ref[...]
        if final:
            ms = jnp.mean(y * y, axis=-1, keepdims=True)
            y = y * lax.rsqrt(ms + EPS) * fg_ref[...]
        o_ref[...] = y


def _moe(x2d, mod, g, w_router, b_router, w_gu_bf, w_d_bf, final_g, tokens_per_mod, final):
    t = x2d.shape[0]
    tm = TM_PROJ
    return pl.pallas_call(
        functools.partial(_moe_kernel, final=final),
        out_shape=jax.ShapeDtypeStruct((t, D_MODEL), F32),
        grid=(t // tm, N_EXPERTS),
        in_specs=[
            pl.BlockSpec((tm, D_MODEL), lambda i, e: (i, 0)),
            pl.BlockSpec((1, 6, D_MODEL), lambda i, e: ((i * tm) // tokens_per_mod, 0, 0)),
            pl.BlockSpec((1, D_MODEL), lambda i, e: (0, 0)),
            pl.BlockSpec((D_MODEL, ROUTER_W), lambda i, e: (0, 0)),
            pl.BlockSpec((1, ROUTER_W), lambda i, e: (0, 0)),
            pl.BlockSpec((1, D_MODEL, 2 * D_EXPERT), lambda i, e: (e, 0, 0)),
            pl.BlockSpec((1, D_EXPERT, D_MODEL), lambda i, e: (e, 0, 0)),
            pl.BlockSpec((1, D_MODEL), lambda i, e: (0, 0)),
        ],
        out_specs=pl.BlockSpec((tm, D_MODEL), lambda i, e: (i, 0)),
        scratch_shapes=[pltpu.VMEM((tm, D_MODEL), BF16), pltpu.VMEM((tm, ROUTER_W), F32),
                        pltpu.VMEM((tm, D_MODEL), F32)],
        compiler_params=_cparams(("parallel", "arbitrary")),
        name="moe_final" if final else "moe",
    )(x2d, mod, g.reshape(1, D_MODEL), w_router, b_router, w_gu_bf, w_d_bf,
      final_g.reshape(1, D_MODEL))


def _rope_tables(seq):
    nf = HEAD_DIM // 4
    t = jnp.arange(seq)
    inv = ROPE_BASE ** (-jnp.arange(nf, dtype=F32) / nf)
    pos = jnp.stack([t // GRID_W, t % GRID_W], -1).astype(F32)
    ang = pos[:, :, None] * inv
    cos, sin = jnp.cos(ang), jnp.sin(ang)
    cos64 = jnp.concatenate([cos[:, 0], cos[:, 0], cos[:, 1], cos[:, 1]], axis=-1)
    sin64 = jnp.concatenate([-sin[:, 0], sin[:, 0], -sin[:, 1], sin[:, 1]], axis=-1)
    return jnp.tile(cos64, (1, 2)), jnp.tile(sin64, (1, 2))


def _layer_weights(l, w_in, w_branch, w_gate, b_gate, w_out, w_router_group, b_router_group,
                   w_router_expert, b_router_expert, w_exp_gate, w_exp_up, w_exp_down):
    nch = D_MODEL // TN_MERGE
    wg = w_gate[l].reshape(D_MODEL, N_BRANCH, nch, TN_MERGE)
    wg = jnp.transpose(wg, (0, 2, 1, 3)).reshape(D_MODEL, N_BRANCH * D_MODEL).astype(BF16)
    bg = b_gate[l].reshape(N_BRANCH, nch, TN_MERGE)
    bg = jnp.transpose(bg, (1, 0, 2)).reshape(nch, 1, N_BRANCH * TN_MERGE)
    pad = ROUTER_W - N_GROUPS - N_EXPERTS
    w_router = jnp.concatenate(
        [w_router_group[l], w_router_expert[l], jnp.zeros((D_MODEL, pad), F32)], axis=1)
    b_router = jnp.concatenate(
        [b_router_group[l], b_router_expert[l], jnp.zeros((pad,), F32)]).reshape(1, ROUTER_W)
    return dict(
        w_in=w_in[l].astype(BF16),
        w_branch=w_branch[l].astype(BF16),
        w_gate=wg, b_gate=bg,
        w_out=w_out[l].astype(BF16),
        w_router=w_router, b_router=b_router,
        w_gu=jnp.concatenate([w_exp_gate[l], w_exp_up[l]], axis=-1).astype(BF16),
        w_d=w_exp_down[l].astype(BF16),
    )


def kernel(x_prompt, x_sample, cache_attn_k, cache_attn_v, cache_diff_k, cache_diff_v, cache_na_k, cache_na_v, c, c_ctx, w_mod, b_mod, norm1_g, w_in, attn_sink, diff_lambda, diff_sub_g, na_rpb, w_branch, w_gate, b_gate, w_out, norm2_g, w_router_group, b_router_group, w_router_expert, b_router_expert, w_exp_gate, w_exp_up, w_exp_down, final_g):
    nbp, seq_p, _ = x_prompt.shape
    nbs, seq_s, _ = x_sample.shape
    past = cache_attn_k.shape[2]
    xp = x_prompt.reshape(nbp * seq_p, D_MODEL)
    xs = x_sample.reshape(nbs * seq_s, D_MODEL)

    cond = jnp.concatenate([c_ctx[None, :], c], axis=0)
    n_cond = -(-cond.shape[0] // 8) * 8
    cond = jnp.pad(cond, ((0, n_cond - cond.shape[0]), (0, 0)))
    mods = _modulation(cond, w_mod, b_mod).reshape(DEPTH, n_cond, 6, D_MODEL)

    rope = _rope_tables(seq_s)
    cache_a = (cache_attn_k.reshape(nbs, DEPTH, past, -1), cache_attn_v.reshape(nbs, DEPTH, past, -1))
    cache_b = (cache_diff_k.reshape(nbs, DEPTH, past, -1), cache_diff_v.reshape(nbs, DEPTH, past, -1))
    cache_c = (cache_na_k.reshape(nbs, DEPTH, past, -1), cache_na_v.reshape(nbs, DEPTH, past, -1))

    new_kv = {name: [] for name in _CTX_F32}
    for l in range(DEPTH):
        lw = _layer_weights(l, w_in, w_branch, w_gate, b_gate, w_out, w_router_group,
                            b_router_group, w_router_expert, b_router_expert,
                            w_exp_gate, w_exp_up, w_exp_down)
        lam_init = 0.8 - 0.6 * math.exp(-0.3 * l)
        mod_p = mods[l, 0:1]
        mod_s = mods[l, 1:1 + nbs]
        final = l == DEPTH - 1

        pr = _inproj(xp, mod_p, norm1_g[l], lw["w_in"], xp.shape[0])
        for name in _CTX_F32:
            new_kv[name].append(pr[name])
        o_a = _sink_attn(attn_sink[l], pr["a_q"], pr["a_k"], pr["a_v"], nb=nbp, seq=seq_p)
        o_b = _diff_attn(diff_lambda[l], diff_sub_g[l], pr["b_q"], pr["b_k"], pr["b_v"],
                         nb=nbp, seq=seq_p, lam_init=lam_init)
        o_c = _dense_attn(pr["c_q"], pr["c_k"], pr["c_v"], nb=nbp, seq=seq_p)
        o_f = _fourier(pr["f"], nb=nbp, seq=seq_p)
        xp = _merge(xp, mod_p, norm1_g[l], (o_a, o_b, o_c, o_f), lw["w_branch"], lw["w_gate"],
                    lw["b_gate"], lw["w_out"], xp.shape[0])
        xp = _moe(xp, mod_p, norm2_g[l], lw["w_router"], lw["b_router"], lw["w_gu"], lw["w_d"],
                  final_g, xp.shape[0], final)

        pr = _inproj(xs, mod_s, norm1_g[l], lw["w_in"], seq_s, rope=rope)
        o_a = _sink_attn(attn_sink[l], pr["a_q"], pr["a_k"], pr["a_v"], nb=nbs, seq=seq_s,
                         cache=cache_a, layer=l)
        o_b = _diff_attn(diff_lambda[l], diff_sub_g[l], pr["b_q"], pr["b_k"], pr["b_v"],
                         nb=nbs, seq=seq_s, lam_init=lam_init, cache=cache_b, layer=l)
        o_c = _na_attn(pr["c_q"], pr["c_k"], pr["c_v"], cache_c, _na_bias_table(na_rpb[l]),
                       nb=nbs, seq=seq_s, layer=l)
        o_f = _fourier(pr["f"], nb=nbs, seq=seq_s)
        xs = _merge(xs, mod_s, norm1_g[l], (o_a, o_b, o_c, o_f), lw["w_branch"], lw["w_gate"],
                    lw["b_gate"], lw["w_out"], seq_s)
        xs = _moe(xs, mod_s, norm2_g[l], lw["w_router"], lw["b_router"], lw["w_gu"], lw["w_d"],
                  final_g, seq_s, final)

    def stack(name, *tail):
        return jnp.stack([a.reshape(nbp, seq_p, *tail) for a in new_kv[name]], axis=1)

    return (
        xp.reshape(x_prompt.shape),
        xs.reshape(x_sample.shape),
        stack("a_k", A_KV_HEADS, HEAD_DIM),
        stack("a_v", A_KV_HEADS, HEAD_DIM),
        stack("b_k", DIFF_HEADS, 2, HEAD_DIM),
        stack("b_v", DIFF_HEADS, DIFF_V_DIM),
        stack("c_k", NA_HEADS, HEAD_DIM),
        stack("c_v", NA_HEADS, HEAD_DIM),
    )
```

```python
import functools
import math

import jax
import jax.numpy as jnp
import numpy as np
from jax import lax
from jax.experimental import pallas as pl
from jax.experimental.pallas import tpu as pltpu

F32 = jnp.float32
BF16 = jnp.bfloat16

D_MODEL = 2048
DEPTH = 2
GRID_W = 64
HEAD_DIM = 64
A_HEADS = 8
A_KV_HEADS = 2
A_GROUP = A_HEADS // A_KV_HEADS
WIN = 128
DIFF_HEADS = 4
DIFF_V_DIM = 2 * HEAD_DIM
NA_HEADS = 8
NA_ROWS = 8
NA_COLS = 16
F_GROUPS = 4
F_GROUP_W = 128
N_BRANCH = 4
BR_W = A_HEADS * HEAD_DIM
N_GROUPS = 4
EXP_PER_GROUP = 4
N_EXPERTS = N_GROUPS * EXP_PER_GROUP
D_EXPERT = 256
ROPE_BASE = 10000.0
EPS = 1e-6
NEG = -1e30
SCALE = HEAD_DIM ** -0.5

_IN_GROUPS = (
    ("a_q", 0, 512), ("a_k", 512, 128), ("a_v", 640, 128),
    ("b_q", 768, 512), ("b_k", 1280, 512), ("b_v", 1792, 512),
    ("c_q", 2304, 512), ("c_k", 2816, 512), ("c_v", 3328, 512),
    ("f", 3840, 512),
)
IN_W = 4352
_Q_GROUPS = ("a_q", "b_q", "c_q")
_ROPED = ("a_q", "a_k", "b_q", "b_k")
_CTX_F32 = ("a_k", "a_v", "b_k", "b_v", "c_k", "c_v")
_CTX_SEQ_MINOR = ("a_k", "a_v", "b_k", "c_k", "c_v")

LANES = 128
VMEM_LIMIT = 56 * 1024 * 1024

TM_PROJ = 512
TQ_SINK = 256
TQ_DIFF = 512
NA_QROWS = 4
NA_BAND = NA_ROWS + NA_QROWS - 1
TN_MERGE = 256
TN_MOD = 1024
ROUTER_W = 128

_NT = (((1,), (1,)), ((), ()))


def _cparams(sem):
    return pltpu.CompilerParams(dimension_semantics=sem, vmem_limit_bytes=VMEM_LIMIT)


def _sigmoid(x):
    return 1.0 / (1.0 + jnp.exp(-x))


def _norm_mod(x, g, shift, scale):
    ms = jnp.mean(x * x, axis=-1, keepdims=True)
    return (x * lax.rsqrt(ms + EPS) * g) * (1.0 + scale) + shift


def _mod_kernel(c_ref, w_ref, b_ref, o_ref):
    c = c_ref[...]
    s = (c * _sigmoid(c)).astype(BF16)
    o_ref[0] = jnp.dot(s, w_ref[0].astype(BF16), preferred_element_type=F32) + b_ref[0]


def _modulation(cond, w_mod, b_mod):
    n = cond.shape[0]
    out_w = w_mod.shape[-1]
    return pl.pallas_call(
        _mod_kernel,
        out_shape=jax.ShapeDtypeStruct((DEPTH, n, out_w), F32),
        grid=(DEPTH, out_w // TN_MOD),
        in_specs=[
            pl.BlockSpec((n, D_MODEL), lambda l, j: (0, 0)),
            pl.BlockSpec((1, D_MODEL, TN_MOD), lambda l, j: (l, 0, j)),
            pl.BlockSpec((1, 1, TN_MOD), lambda l, j: (l, 0, j)),
        ],
        out_specs=pl.BlockSpec((1, n, TN_MOD), lambda l, j: (l, 0, j)),
        compiler_params=_cparams(("parallel", "parallel")),
        name="modulation",
    )(cond, w_mod, b_mod.reshape(DEPTH, 1, out_w))


def _rope128(y, cos, sin):
    lane = lax.broadcasted_iota(jnp.int32, y.shape, 1)
    first = (lane % 32) < 16
    partner = jnp.where(first, pltpu.roll(y, LANES - 16, 1), pltpu.roll(y, 16, 1))
    return y * cos + partner * sin


def _inproj_kernel(*refs, latent, n_alias, seq):
    if latent:
        x_ref, mod_ref, g_ref, w_ref, cos_ref, sin_ref = refs[:6]
        out_refs = refs[6:]
    else:
        x_ref, mod_ref, g_ref, w_ref = refs[:4]
        out_refs = refs[4 + n_alias:]
    h = _norm_mod(x_ref[...], g_ref[...], mod_ref[0, 0:1, :], mod_ref[0, 1:2, :]).astype(BF16)
    for (name, start, width), o_ref in zip(_IN_GROUPS, out_refs):
        y = jnp.dot(h, w_ref[:, start:start + width], preferred_element_type=F32)
        if name in _Q_GROUPS:
            y = y * SCALE
        if latent and name in _ROPED:
            cos = cos_ref[...]
            sin = sin_ref[...]
            for c0 in range(0, width, LANES):
                o_ref[:, c0:c0 + LANES] = _rope128(y[:, c0:c0 + LANES], cos, sin).astype(o_ref.dtype)
        elif not latent and name in _CTX_F32:
            for bb in range(o_ref.shape[0]):
                yb = y[bb * seq:(bb + 1) * seq]
                o_ref[bb] = yb.T if name in _CTX_SEQ_MINOR else yb
        else:
            o_ref[...] = y.astype(o_ref.dtype)


def _inproj(x2d, mod, g, w_in_bf, tokens_per_mod, *, rope=None, seq=None, layer=0, prev_kv=None):
    t = x2d.shape[0]
    tm = TM_PROJ
    latent = rope is not None
    in_specs = [
        pl.BlockSpec((tm, D_MODEL), lambda i: (i, 0)),
        pl.BlockSpec((1, 6, D_MODEL), lambda i: ((i * tm) // tokens_per_mod, 0, 0)),
        pl.BlockSpec((1, D_MODEL), lambda i: (0, 0)),
        pl.BlockSpec((D_MODEL, IN_W), lambda i: (0, 0), pipeline_mode=pl.Buffered(1)),
    ]
    args = [x2d, mod, g.reshape(1, D_MODEL), w_in_bf]
    aliases = {}
    n_alias = 0
    if latent:
        nblk = rope[0].shape[0] // tm
        in_specs += [pl.BlockSpec((tm, LANES), lambda i: (i % nblk, 0))] * 2
        args += list(rope)
    elif prev_kv is not None:
        n_alias = len(_CTX_F32)
        in_specs += [pl.BlockSpec(memory_space=pl.ANY)] * n_alias
        args += [prev_kv[name] for name in _CTX_F32]
    out_shape, out_specs = [], []
    tb = tm // seq if not latent else None
    for oi, (name, _, width) in enumerate(_IN_GROUPS):
        if not latent and name in _CTX_F32:
            blk = (width, seq) if name in _CTX_SEQ_MINOR else (seq, width)
            out_shape.append(jax.ShapeDtypeStruct((t // seq, DEPTH) + blk, F32))
            out_specs.append(pl.BlockSpec((tb, None) + blk, lambda i: (i, layer, 0, 0)))
            if prev_kv is not None:
                aliases[4 + _CTX_F32.index(name)] = oi
        else:
            out_shape.append(jax.ShapeDtypeStruct((t, width), BF16))
            out_specs.append(pl.BlockSpec((tm, width), lambda i: (i, 0)))
    outs = pl.pallas_call(
        functools.partial(_inproj_kernel, latent=latent, n_alias=n_alias, seq=seq),
        out_shape=out_shape,
        grid=(t // tm,),
        in_specs=in_specs,
        out_specs=out_specs,
        input_output_aliases=aliases,
        compiler_params=_cparams(("parallel",)),
        name="inproj_latent" if latent else "inproj_context",
    )(*args)
    return dict(zip([n for n, _, _ in _IN_GROUPS], outs))


def _attend(q, segs, sink_col=None):
    scores = []
    for k, _, bias, k_seq_minor, _ in segs:
        if k_seq_minor:
            s = jnp.dot(q, k, preferred_element_type=F32)
        else:
            s = lax.dot_general(q, k, _NT, preferred_element_type=F32)
        if bias is not None:
            s = s + bias
        scores.append(s)
    m = scores[0].max(axis=-1, keepdims=True)
    for s in scores[1:]:
        m = jnp.maximum(m, s.max(axis=-1, keepdims=True))
    if sink_col is not None:
        m = jnp.maximum(m, sink_col)
    acc = None
    den = None
    for s, (_, v, _, _, v_seq_minor) in zip(scores, segs):
        p = jnp.exp(s - m)
        ps = p.sum(axis=-1, keepdims=True)
        if v_seq_minor:
            o = lax.dot_general(p.astype(BF16), v, _NT, preferred_element_type=F32)
        else:
            o = jnp.dot(p.astype(BF16), v, preferred_element_type=F32)
        acc = o if acc is None else acc + o
        den = ps if den is None else den + ps
    if sink_col is not None:
        den = den + jnp.exp(sink_col - m)
    return acc, den


def _head(x, h, width=HEAD_DIM):
    return x[:, h * width:(h + 1) * width]


def _head_rows(x, h, width=HEAD_DIM):
    return x[h * width:(h + 1) * width, :]


def _kv_spec(width, seq, layer):
    return pl.BlockSpec((None, None, width, seq), lambda b, i: (b, layer, 0, 0))


def _sink_attn_kernel(*refs, latent, tq, seq):
    if latent:
        sink_ref, q_ref, k_ref, v_ref, kc_ref, vc_ref, o_ref = refs
    else:
        sink_ref, q_ref, kc_ref, vc_ref, o_ref = refs
    q_all = q_ref[...]
    k_ctx = kc_ref[...].astype(BF16)
    v_ctx = vc_ref[...].astype(BF16)
    if latent:
        band = tq + 2 * WIN
        qb = pl.program_id(1)
        start = pl.multiple_of(jnp.clip(qb * tq - WIN, 0, seq - band), WIN)
        k_loc = k_ref[pl.ds(start, band), :]
        v_loc = v_ref[pl.ds(start, band), :]
        rows = A_GROUP * tq
        qpos = qb * tq + lax.broadcasted_iota(jnp.int32, (rows, band), 0) % tq
        kpos = start + lax.broadcasted_iota(jnp.int32, (rows, band), 1)
        bias = jnp.where(jnp.abs(qpos - kpos) <= WIN, 0.0, NEG).astype(F32)
    outs = [None] * A_HEADS
    for kv in range(A_KV_HEADS):
        heads = [kv * A_GROUP + g for g in range(A_GROUP)]
        q = jnp.concatenate([_head(q_all, h) for h in heads], axis=0)
        sink_col = jnp.concatenate(
            [jnp.full((tq, 1), sink_ref[0, h], F32) for h in heads], axis=0)
        segs = [(_head_rows(k_ctx, kv), _head_rows(v_ctx, kv), None, True, True)]
        if latent:
            segs.append((_head(k_loc, kv), _head(v_loc, kv), bias, False, False))
        acc, den = _attend(q, segs, sink_col)
        o = acc / den
        for g, h in enumerate(heads):
            outs[h] = o[g * tq:(g + 1) * tq]
    o_ref[...] = jnp.concatenate(outs, axis=1).astype(o_ref.dtype)


def _sink_attn(sink, q, kv_ctx, *, nb, seq, layer, kv_loc=None):
    latent = kv_loc is not None
    tq = TQ_SINK if latent else seq
    nq = seq // tq
    kw = A_KV_HEADS * HEAD_DIM
    s_ctx = kv_ctx[0].shape[-1]
    in_specs = [
        pl.BlockSpec(memory_space=pltpu.SMEM),
        pl.BlockSpec((tq, BR_W), lambda b, i: (b * nq + i, 0)),
    ]
    args = [sink.reshape(1, A_HEADS), q]
    if latent:
        in_specs += [pl.BlockSpec((seq, kw), lambda b, i: (b, 0))] * 2
        args += list(kv_loc)
    in_specs += [_kv_spec(kw, s_ctx, layer)] * 2
    args += list(kv_ctx)
    return pl.pallas_call(
        functools.partial(_sink_attn_kernel, latent=latent, tq=tq, seq=seq),
        out_shape=jax.ShapeDtypeStruct(q.shape, BF16),
        grid=(nb, nq),
        in_specs=in_specs,
        out_specs=pl.BlockSpec((tq, BR_W), lambda b, i: (b * nq + i, 0)),
        compiler_params=_cparams(("parallel", "arbitrary")),
        name="sink_attn_latent" if latent else "sink_attn_context",
    )(*args)


def _diff_attn_kernel(*refs, latent, lam_init):
    if latent:
        lam_ref, g_ref, q_ref, k_ref, v_ref, kc_ref, vc_ref, o_ref = refs
    else:
        lam_ref, g_ref, q_ref, kc_ref, vc_ref, o_ref = refs
    lam = lam_ref[...]
    lam_a = jnp.sum(lam[0:1] * lam[1:2], axis=-1, keepdims=True)
    lam_b = jnp.sum(lam[2:3] * lam[3:4], axis=-1, keepdims=True)
    lam_full = jnp.exp(lam_a) - jnp.exp(lam_b) + lam_init
    q_all = q_ref[...]
    k_ctx = kc_ref[...].astype(BF16)
    v_ctx = vc_ref[...].astype(BF16)
    if latent:
        k_loc = k_ref[...]
        v_loc = v_ref[...]
    outs = []
    for h in range(DIFF_HEADS):
        res = []
        for sub in range(2):
            hh = 2 * h + sub
            segs = [(_head_rows(k_ctx, hh), _head(v_ctx, h, DIFF_V_DIM), None, True, False)]
            if latent:
                segs.append((_head(k_loc, hh), _head(v_loc, h, DIFF_V_DIM), None, False, False))
            acc, den = _attend(_head(q_all, hh), segs)
            res.append(acc / den)
        o = res[0] - lam_full * res[1]
        ms = jnp.mean(o * o, axis=-1, keepdims=True)
        outs.append(o * lax.rsqrt(ms + EPS) * g_ref[...] * (1.0 - lam_init))
    o_ref[...] = jnp.concatenate(outs, axis=1).astype(o_ref.dtype)


def _diff_attn(lam, sub_g, q, kv_ctx, *, nb, seq, lam_init, layer, kv_loc=None):
    latent = kv_loc is not None
    tq = TQ_DIFF if latent else seq
    nq = seq // tq
    kw = DIFF_HEADS * 2 * HEAD_DIM
    s_ctx = kv_ctx[0].shape[-1]
    in_specs = [
        pl.BlockSpec((4, HEAD_DIM), lambda b, i: (0, 0)),
        pl.BlockSpec((1, DIFF_V_DIM), lambda b, i: (0, 0)),
        pl.BlockSpec((tq, BR_W), lambda b, i: (b * nq + i, 0)),
    ]
    args = [lam, sub_g.reshape(1, DIFF_V_DIM), q]
    if latent:
        in_specs += [pl.BlockSpec((seq, kw), lambda b, i: (b, 0))] * 2
        args += list(kv_loc)
    in_specs += [_kv_spec(kw, s_ctx, layer), _kv_spec(s_ctx, kw, layer)]
    args += list(kv_ctx)
    return pl.pallas_call(
        functools.partial(_diff_attn_kernel, latent=latent, lam_init=lam_init),
        out_shape=jax.ShapeDtypeStruct(q.shape, BF16),
        grid=(nb, nq),
        in_specs=in_specs,
        out_specs=pl.BlockSpec((tq, BR_W), lambda b, i: (b * nq + i, 0)),
        compiler_params=_cparams(("parallel", "arbitrary")),
        name="diff_attn_latent" if latent else "diff_attn_context",
    )(*args)


def _dense_attn_kernel(q_ref, k_ref, v_ref, o_ref):
    q_all = q_ref[...]
    k = k_ref[...].astype(BF16)
    v = v_ref[...].astype(BF16)
    outs = []
    for h in range(NA_HEADS):
        acc, den = _attend(_head(q_all, h), [(_head_rows(k, h), _head_rows(v, h), None, True, True)])
        outs.append(acc / den)
    o_ref[...] = jnp.concatenate(outs, axis=1).astype(o_ref.dtype)


def _dense_attn(q, kv_ctx, *, nb, seq, layer):
    return pl.pallas_call(
        _dense_attn_kernel,
        out_shape=jax.ShapeDtypeStruct(q.shape, BF16),
        grid=(nb, 1),
        in_specs=[pl.BlockSpec((seq, BR_W), lambda b, i: (b, 0)),
                  _kv_spec(BR_W, seq, layer), _kv_spec(BR_W, seq, layer)],
        out_specs=pl.BlockSpec((seq, BR_W), lambda b, i: (b, 0)),
        compiler_params=_cparams(("parallel", "arbitrary")),
        name="dense_attn_context",
    )(q, *kv_ctx)


def _na_window_start(r, rows):
    return jnp.clip(r - NA_ROWS // 2, 0, rows - NA_ROWS)


def _na_band_start(step, rows):
    return jnp.clip(step * NA_QROWS - NA_ROWS // 2, 0, rows - NA_BAND)


def _na_attn_kernel(q_ref, k_ref, v_ref, kc_ref, vc_ref, bias_ref, o_ref, *, rows):
    step = pl.program_id(0)
    start = pl.multiple_of(_na_band_start(step, rows) * GRID_W, GRID_W)
    band = NA_BAND * GRID_W
    q_all = q_ref[...]
    k_loc = k_ref[pl.ds(start, band), :]
    v_loc = v_ref[pl.ds(start, band), :]
    k_ctx = kc_ref[...].astype(BF16)
    v_ctx = vc_ref[...].astype(BF16)
    outs = []
    for h in range(NA_HEADS):
        segs = [(_head_rows(k_ctx, h), _head_rows(v_ctx, h), None, True, True),
                (_head(k_loc, h), _head(v_loc, h), bias_ref[h, 0], False, False)]
        acc, den = _attend(_head(q_all, h), segs)
        outs.append(acc / den)
    o_ref[...] = jnp.concatenate(outs, axis=1).astype(o_ref.dtype)


def _na_attn(q, kv_loc, kv_ctx, bias_tab, *, nb, seq, layer):
    rows = seq // GRID_W
    nsteps = rows // NA_QROWS
    tq = NA_QROWS * GRID_W
    band = NA_BAND * GRID_W
    past = kv_ctx[0].shape[-1]
    ctx_spec = pl.BlockSpec((None, None, BR_W, past), lambda s, b: (b, layer, 0, 0))
    return pl.pallas_call(
        functools.partial(_na_attn_kernel, rows=rows),
        out_shape=jax.ShapeDtypeStruct(q.shape, BF16),
        grid=(nsteps, nb),
        in_specs=[
            pl.BlockSpec((tq, BR_W), lambda s, b: (b * nsteps + s, 0)),
            pl.BlockSpec((seq, BR_W), lambda s, b: (b, 0)),
            pl.BlockSpec((seq, BR_W), lambda s, b: (b, 0)),
            ctx_spec, ctx_spec,
            pl.BlockSpec((NA_HEADS, 1, tq, band), lambda s, b: (0, s, 0, 0)),
        ],
        out_specs=pl.BlockSpec((tq, BR_W), lambda s, b: (b * nsteps + s, 0)),
        compiler_params=_cparams(("arbitrary", "arbitrary")),
        name="na_attn_latent",
    )(q, *kv_loc, *kv_ctx, bias_tab)


def _rpb_expand_kernel(rpb_ref, onehot_ref, o_ref):
    o_ref[...] = jnp.dot(rpb_ref[...], onehot_ref[...], preferred_element_type=F32,
                         precision=lax.Precision.HIGHEST)


def _na_bias_table(rpb, rows):
    n_dr, n_dc = 2 * NA_ROWS - 1, 2 * NA_COLS - 1
    qcol = np.arange(GRID_W)
    kcol = np.arange(GRID_W)
    cs = np.clip(qcol - NA_COLS // 2, 0, GRID_W - NA_COLS)
    col_ok = (kcol[None, :] >= cs[:, None]) & (kcol[None, :] < cs[:, None] + NA_COLS)
    dc = np.clip(kcol[None, :] - qcol[:, None] + NA_COLS - 1, 0, n_dc - 1)
    onehot = np.zeros((32, GRID_W * GRID_W), np.float32)
    onehot[dc.reshape(-1), np.arange(GRID_W * GRID_W)] = 1.0
    rpb2 = jnp.pad(rpb.reshape(NA_HEADS * n_dr, n_dc), ((0, 0), (0, 32 - n_dc)))
    full = pl.pallas_call(
        _rpb_expand_kernel,
        out_shape=jax.ShapeDtypeStruct((NA_HEADS * n_dr, GRID_W * GRID_W), F32),
        name="rpb_expand",
    )(rpb2, jnp.asarray(onehot))
    full = full.reshape(NA_HEADS, n_dr, GRID_W, GRID_W)
    full = full + jnp.asarray(np.where(col_ok, 0.0, NEG).astype(np.float32))

    nsteps = rows // NA_QROWS
    idx = np.zeros((nsteps, NA_QROWS, NA_BAND), np.int32)
    row_ok = np.zeros((nsteps, NA_QROWS, NA_BAND), bool)
    for s in range(nsteps):
        us = int(np.clip(s * NA_QROWS - NA_ROWS // 2, 0, rows - NA_BAND))
        for ri in range(NA_QROWS):
            r = s * NA_QROWS + ri
            rs = int(np.clip(r - NA_ROWS // 2, 0, rows - NA_ROWS))
            assert us <= rs and rs + NA_ROWS <= us + NA_BAND
            for j in range(NA_BAND):
                row_ok[s, ri, j] = rs <= us + j < rs + NA_ROWS
                idx[s, ri, j] = np.clip(us + j - r + NA_ROWS - 1, 0, n_dr - 1)
    tab = jnp.take(full, jnp.asarray(idx.reshape(-1)), axis=1)
    tab = tab.reshape(NA_HEADS, nsteps, NA_QROWS, NA_BAND, GRID_W, GRID_W)
    tab = jnp.where(jnp.asarray(row_ok)[None, :, :, :, None, None], tab, NEG)
    tab = jnp.transpose(tab, (0, 1, 2, 4, 3, 5))
    return tab.reshape(NA_HEADS, nsteps, NA_QROWS * GRID_W, NA_BAND * GRID_W)


def _dft_tables(seq):
    pos = np.arange(seq)
    ang = 2.0 * np.pi * ((pos[:, None] * pos[None, :]) % seq) / seq
    left = np.concatenate([np.cos(ang), -np.sin(ang)], axis=1).astype(np.float32)
    ch = np.arange(F_GROUP_W)
    angw = 2.0 * np.pi * ((ch[:, None] * ch[None, :]) % F_GROUP_W) / F_GROUP_W
    right = np.concatenate([np.cos(angw), np.sin(angw)], axis=1).astype(np.float32)
    return jnp.asarray(left).astype(BF16), jnp.asarray(right).astype(BF16)


def _fourier_kernel(f_ref, left_ref, right_ref, o_ref, stack_ref, *, seq):
    f = f_ref[...]
    right = right_ref[...]
    for g in range(F_GROUPS):
        t = jnp.dot(_head(f, g, F_GROUP_W), right, preferred_element_type=F32).astype(BF16)
        stack_ref[0:seq, g * F_GROUP_W:(g + 1) * F_GROUP_W] = t[:, :F_GROUP_W]
        stack_ref[seq:2 * seq, g * F_GROUP_W:(g + 1) * F_GROUP_W] = t[:, F_GROUP_W:]
    y = jnp.dot(left_ref[...], stack_ref[...], preferred_element_type=F32)
    o_ref[...] = (y * (1.0 / math.sqrt(seq * F_GROUP_W))).astype(o_ref.dtype)


def _fourier(f, *, nb, seq):
    left, right = _dft_tables(seq)
    w = F_GROUPS * F_GROUP_W
    return pl.pallas_call(
        functools.partial(_fourier_kernel, seq=seq),
        out_shape=jax.ShapeDtypeStruct(f.shape, BF16),
        grid=(nb,),
        in_specs=[
            pl.BlockSpec((seq, w), lambda b: (b, 0)),
            pl.BlockSpec((seq, 2 * seq), lambda b: (0, 0)),
            pl.BlockSpec((F_GROUP_W, 2 * F_GROUP_W), lambda b: (0, 0)),
        ],
        out_specs=pl.BlockSpec((seq, w), lambda b: (b, 0)),
        scratch_shapes=[pltpu.VMEM((2 * seq, w), BF16)],
        compiler_params=_cparams(("parallel",)),
        name=f"fourier_{seq}",
    )(f, left, right)


def _merge_kernel(x_ref, mod_ref, g_ref, *refs):
    br_refs = refs[0:N_BRANCH]
    wbr_ref = refs[N_BRANCH]
    wg_refs = refs[N_BRANCH + 1:2 * N_BRANCH + 1]
    bg_refs = refs[2 * N_BRANCH + 1:3 * N_BRANCH + 1]
    wo_ref, o_ref, h_ref, acc_ref = refs[3 * N_BRANCH + 1:]
    j = pl.program_id(1)

    @pl.when(j == 0)
    def _():
        h_ref[...] = _norm_mod(x_ref[...], g_ref[...], mod_ref[0, 0:1, :],
                               mod_ref[0, 1:2, :]).astype(BF16)
        acc_ref[...] = jnp.zeros_like(acc_ref)

    h = h_ref[...]
    mix = None
    for n in range(N_BRANCH):
        gate = _sigmoid(jnp.dot(h, wg_refs[n][...], preferred_element_type=F32) + bg_refs[n][0])
        proj = jnp.dot(br_refs[n][...], wbr_ref[n], preferred_element_type=F32)
        term = gate * proj
        mix = term if mix is None else mix + term
    acc_ref[...] += jnp.dot(mix.astype(BF16), wo_ref[...], preferred_element_type=F32)

    @pl.when(j == pl.num_programs(1) - 1)
    def _():
        o_ref[...] = x_ref[...] + mod_ref[0, 2:3, :] * acc_ref[...]


def _merge(x2d, mod, g, branches, w_branch_bf, w_gate_bf, b_gate, w_out_bf, tokens_per_mod):
    t = x2d.shape[0]
    tm, tn = TM_PROJ, TN_MERGE
    nch = D_MODEL // tn
    br_spec = pl.BlockSpec((tm, BR_W), lambda i, j: (i, 0))
    wg_specs = [pl.BlockSpec((D_MODEL, tn), functools.partial(lambda i, j, n: (0, n * nch + j), n=n))
                for n in range(N_BRANCH)]
    bg_specs = [pl.BlockSpec((1, 1, tn), functools.partial(lambda i, j, n: (n * nch + j, 0, 0), n=n))
                for n in range(N_BRANCH)]
    bg = b_gate.reshape(N_BRANCH * nch, 1, tn)
    return pl.pallas_call(
        _merge_kernel,
        out_shape=jax.ShapeDtypeStruct((t, D_MODEL), F32),
        grid=(t // tm, nch),
        in_specs=[
            pl.BlockSpec((tm, D_MODEL), lambda i, j: (i, 0)),
            pl.BlockSpec((1, 6, D_MODEL), lambda i, j: ((i * tm) // tokens_per_mod, 0, 0)),
            pl.BlockSpec((1, D_MODEL), lambda i, j: (0, 0)),
            br_spec, br_spec, br_spec, br_spec,
            pl.BlockSpec((N_BRANCH, BR_W, tn), lambda i, j: (0, 0, j)),
            *wg_specs, *bg_specs,
            pl.BlockSpec((tn, D_MODEL), lambda i, j: (j, 0)),
        ],
        out_specs=pl.BlockSpec((tm, D_MODEL), lambda i, j: (i, 0)),
        scratch_shapes=[pltpu.VMEM((tm, D_MODEL), BF16), pltpu.VMEM((tm, D_MODEL), F32)],
        compiler_params=_cparams(("parallel", "arbitrary")),
        name="merge",
    )(x2d, mod, g.reshape(1, D_MODEL), *branches, w_branch_bf,
      *([w_gate_bf] * N_BRANCH), *([bg] * N_BRANCH), w_out_bf)


def _route(logits):
    lane = lax.broadcasted_iota(jnp.int32, logits.shape, 1)
    ninf = -jnp.inf
    lg = jnp.where(lane < N_GROUPS, logits, ninf)
    mg = lg.max(axis=-1, keepdims=True)
    pg_top = 1.0 / jnp.exp(lg - mg).sum(axis=-1, keepdims=True)
    g_idx = jnp.where(lg == mg, lane, ROUTER_W).min(axis=-1, keepdims=True)
    lo = N_GROUPS + EXP_PER_GROUP * g_idx
    le = jnp.where((lane >= lo) & (lane < lo + EXP_PER_GROUP), logits, ninf)
    m1 = le.max(axis=-1, keepdims=True)
    e1 = jnp.where(le == m1, lane, ROUTER_W).min(axis=-1, keepdims=True)
    le2 = jnp.where(lane == e1, ninf, le)
    m2 = le2.max(axis=-1, keepdims=True)
    e2 = jnp.where(le2 == m2, lane, ROUTER_W).min(axis=-1, keepdims=True)
    se = jnp.exp(le - m1).sum(axis=-1, keepdims=True)
    pe1 = 1.0 / se
    pe2 = jnp.exp(m2 - m1) / se
    tot = pe1 + pe2
    w1 = pe1 / tot * pg_top
    w2 = pe2 / tot * pg_top
    return jnp.where(lane == e1, w1, 0.0) + jnp.where(lane == e2, w2, 0.0)


def _moe_kernel(x_ref, mod_ref, g_ref, wr_ref, br_ref, wg_ref, wu_ref, wd_ref, fg_ref,
                o_ref, h_ref, comb_ref, acc_ref, *, final):
    e = pl.program_id(1)

    @pl.when(e == 0)
    def _():
        h = _norm_mod(x_ref[...], g_ref[...], mod_ref[0, 3:4, :], mod_ref[0, 4:5, :])
        h_ref[...] = h.astype(BF16)
        logits = jnp.dot(h, wr_ref[...], preferred_element_type=F32,
                         precision=lax.Precision.HIGHEST) + br_ref[...]
        comb_ref[...] = _route(logits)
        acc_ref[...] = jnp.zeros_like(acc_ref)

    lane = lax.broadcasted_iota(jnp.int32, comb_ref.shape, 1)
    cw = jnp.where(lane == N_GROUPS + e, comb_ref[...], 0.0).sum(axis=-1, keepdims=True)
    h = h_ref[...]
    gate = jnp.dot(h, wg_ref[0], preferred_element_type=F32)
    up = jnp.dot(h, wu_ref[0], preferred_element_type=F32)
    hid = (gate * _sigmoid(gate)) * up * cw
    acc_ref[...] += jnp.dot(hid.astype(BF16), wd_ref[0], preferred_element_type=F32)

    @pl.when(e == pl.num_programs(1) - 1)
    def _():
        y = x_ref[...] + mod_ref[0, 5:6, :] * acc_ref[...]
        if final:
            ms = jnp.mean(y * y, axis=-1, keepdims=True)
            y = y * lax.rsqrt(ms + EPS) * fg_ref[...]
        o_ref[...] = y


def _moe(x2d, mod, g, w_router, b_router, w_g_bf, w_u_bf, w_d_bf, final_g, tokens_per_mod, final):
    t = x2d.shape[0]
    tm = TM_PROJ
    return pl.pallas_call(
        functools.partial(_moe_kernel, final=final),
        out_shape=jax.ShapeDtypeStruct((t, D_MODEL), F32),
        grid=(t // tm, N_EXPERTS),
        in_specs=[
            pl.BlockSpec((tm, D_MODEL), lambda i, e: (i, 0)),
            pl.BlockSpec((1, 6, D_MODEL), lambda i, e: ((i * tm) // tokens_per_mod, 0, 0)),
            pl.BlockSpec((1, D_MODEL), lambda i, e: (0, 0)),
            pl.BlockSpec((D_MODEL, ROUTER_W), lambda i, e: (0, 0)),
            pl.BlockSpec((1, ROUTER_W), lambda i, e: (0, 0)),
            pl.BlockSpec((1, D_MODEL, D_EXPERT), lambda i, e: (e, 0, 0)),
            pl.BlockSpec((1, D_MODEL, D_EXPERT), lambda i, e: (e, 0, 0)),
            pl.BlockSpec((1, D_EXPERT, D_MODEL), lambda i, e: (e, 0, 0)),
            pl.BlockSpec((1, D_MODEL), lambda i, e: (0, 0)),
        ],
        out_specs=pl.BlockSpec((tm, D_MODEL), lambda i, e: (i, 0)),
        scratch_shapes=[pltpu.VMEM((tm, D_MODEL), BF16), pltpu.VMEM((tm, ROUTER_W), F32),
                        pltpu.VMEM((tm, D_MODEL), F32)],
        compiler_params=_cparams(("parallel", "arbitrary")),
        name="moe_final" if final else "moe",
    )(x2d, mod, g.reshape(1, D_MODEL), w_router, b_router, w_g_bf, w_u_bf, w_d_bf,
      final_g.reshape(1, D_MODEL))


def _rope_tables(seq):
    nf = HEAD_DIM // 4
    t = jnp.arange(seq)
    inv = ROPE_BASE ** (-jnp.arange(nf, dtype=F32) / nf)
    pos = jnp.stack([t // GRID_W, t % GRID_W], -1).astype(F32)
    ang = pos[:, :, None] * inv
    cos, sin = jnp.cos(ang), jnp.sin(ang)
    cos64 = jnp.concatenate([cos[:, 0], cos[:, 0], cos[:, 1], cos[:, 1]], axis=-1)
    sin64 = jnp.concatenate([-sin[:, 0], sin[:, 0], -sin[:, 1], sin[:, 1]], axis=-1)
    return jnp.tile(cos64, (1, 2)), jnp.tile(sin64, (1, 2))


def _seq_minor_view(cache):
    nb, depth, seq = cache.shape[:3]
    nd = cache.ndim
    return jnp.transpose(cache, (0, 1, *range(3, nd), 2)).reshape(nb, depth, -1, seq)


def _seq_major_view(arr, head_dims):
    nb, depth, _, seq = arr.shape
    nd = 3 + len(head_dims)
    arr = arr.reshape(nb, depth, *head_dims, seq)
    return jnp.transpose(arr, (0, 1, nd - 1, *range(2, nd - 1)))


def kernel(x_prompt, x_sample, cache_attn_k, cache_attn_v, cache_diff_k, cache_diff_v, cache_na_k, cache_na_v, c, c_ctx, w_mod, b_mod, norm1_g, w_in, attn_sink, diff_lambda, diff_sub_g, na_rpb, w_branch, w_gate, b_gate, w_out, norm2_g, w_router_group, b_router_group, w_router_expert, b_router_expert, w_exp_gate, w_exp_up, w_exp_down, final_g):
    nbp, seq_p, _ = x_prompt.shape
    nbs, seq_s, _ = x_sample.shape
    past = cache_attn_k.shape[2]
    xp = x_prompt.reshape(nbp * seq_p, D_MODEL)
    xs = x_sample.reshape(nbs * seq_s, D_MODEL)

    cond = jnp.concatenate([c_ctx[None, :], c], axis=0)
    n_cond = -(-cond.shape[0] // 8) * 8
    cond = jnp.pad(cond, ((0, n_cond - cond.shape[0]), (0, 0)))
    mods = _modulation(cond, w_mod, b_mod).reshape(DEPTH, n_cond, 6, D_MODEL)

    rope = _rope_tables(seq_s)
    cache_a = (_seq_minor_view(cache_attn_k), _seq_minor_view(cache_attn_v))
    cache_b = (_seq_minor_view(cache_diff_k), cache_diff_v.reshape(nbs, DEPTH, past, -1))
    cache_c = (_seq_minor_view(cache_na_k), _seq_minor_view(cache_na_v))

    w_in_bf = w_in.astype(BF16)
    w_branch_bf = w_branch.astype(BF16)
    w_gate_bf = w_gate.astype(BF16)
    w_out_bf = w_out.astype(BF16)
    w_eg_bf = w_exp_gate.astype(BF16)
    w_eu_bf = w_exp_up.astype(BF16)
    w_ed_bf = w_exp_down.astype(BF16)
    pad = ROUTER_W - N_GROUPS - N_EXPERTS
    w_router = jnp.concatenate(
        [w_router_group, w_router_expert, jnp.zeros((DEPTH, D_MODEL, pad), F32)], axis=-1)
    b_router = jnp.concatenate(
        [b_router_group, b_router_expert, jnp.zeros((DEPTH, pad), F32)], axis=-1)

    kv = None
    for l in range(DEPTH):
        lam_init = 0.8 - 0.6 * math.exp(-0.3 * l)
        mod_p = mods[l, 0:1]
        mod_s = mods[l, 1:1 + nbs]
        final = l == DEPTH - 1
        moe_w = (w_router[l], b_router[l].reshape(1, ROUTER_W), w_eg_bf[l], w_eu_bf[l], w_ed_bf[l])
        merge_w = (w_branch_bf[l], w_gate_bf[l], b_gate[l], w_out_bf[l])

        pr = _inproj(xp, mod_p, norm1_g[l], w_in_bf[l], xp.shape[0], seq=seq_p, layer=l, prev_kv=kv)
        kv = {name: pr[name] for name in _CTX_F32}
        o_a = _sink_attn(attn_sink[l], pr["a_q"], (kv["a_k"], kv["a_v"]), nb=nbp, seq=seq_p, layer=l)
        o_b = _diff_attn(diff_lambda[l], diff_sub_g[l], pr["b_q"], (kv["b_k"], kv["b_v"]),
                         nb=nbp, seq=seq_p, lam_init=lam_init, layer=l)
        o_c = _dense_attn(pr["c_q"], (kv["c_k"], kv["c_v"]), nb=nbp, seq=seq_p, layer=l)
        o_f = _fourier(pr["f"], nb=nbp, seq=seq_p)
        xp = _merge(xp, mod_p, norm1_g[l], (o_a, o_b, o_c, o_f), *merge_w, xp.shape[0])
        xp = _moe(xp, mod_p, norm2_g[l], *moe_w, final_g, xp.shape[0], final)

        pr = _inproj(xs, mod_s, norm1_g[l], w_in_bf[l], seq_s, rope=rope)
        o_a = _sink_attn(attn_sink[l], pr["a_q"], cache_a, nb=nbs, seq=seq_s, layer=l,
                         kv_loc=(pr["a_k"], pr["a_v"]))
        o_b = _diff_attn(diff_lambda[l], diff_sub_g[l], pr["b_q"], cache_b, nb=nbs, seq=seq_s,
                         lam_init=lam_init, layer=l, kv_loc=(pr["b_k"], pr["b_v"]))
        o_c = _na_attn(pr["c_q"], (pr["c_k"], pr["c_v"]), cache_c,
                       _na_bias_table(na_rpb[l], seq_s // GRID_W), nb=nbs, seq=seq_s, layer=l)
        o_f = _fourier(pr["f"], nb=nbs, seq=seq_s)
        xs = _merge(xs, mod_s, norm1_g[l], (o_a, o_b, o_c, o_f), *merge_w, seq_s)
        xs = _moe(xs, mod_s, norm2_g[l], *moe_w, final_g, seq_s, final)

    return (
        xp.reshape(x_prompt.shape),
        xs.reshape(x_sample.shape),
        _seq_major_view(kv["a_k"], (A_KV_HEADS, HEAD_DIM)),
        _seq_major_view(kv["a_v"], (A_KV_HEADS, HEAD_DIM)),
        _seq_major_view(kv["b_k"], (DIFF_HEADS, 2, HEAD_DIM)),
        kv["b_v"].reshape(nbp, DEPTH, seq_p, DIFF_HEADS, DIFF_V_DIM),
        _seq_major_view(kv["c_k"], (NA_HEADS, HEAD_DIM)),
        _seq_major_view(kv["c_v"], (NA_HEADS, HEAD_DIM)),
    )
```

```python
import functools
import math

import jax
import jax.numpy as jnp
import numpy as np
from jax import lax
from jax.experimental import pallas as pl
from jax.experimental.pallas import tpu as pltpu

F32 = jnp.float32
BF16 = jnp.bfloat16

D_MODEL = 2048
DEPTH = 2
GRID_W = 64
HEAD_DIM = 64
A_HEADS = 8
A_KV_HEADS = 2
A_GROUP = A_HEADS // A_KV_HEADS
WIN = 128
DIFF_HEADS = 4
DIFF_V_DIM = 2 * HEAD_DIM
NA_HEADS = 8
NA_ROWS = 8
NA_COLS = 16
F_GROUPS = 4
F_GROUP_W = 128
N_BRANCH = 4
BR_W = A_HEADS * HEAD_DIM
N_GROUPS = 4
EXP_PER_GROUP = 4
N_EXPERTS = N_GROUPS * EXP_PER_GROUP
D_EXPERT = 256
ROPE_BASE = 10000.0
EPS = 1e-6
NEG = -1e30
SCALE = HEAD_DIM ** -0.5

_IN_GROUPS = (
    ("a_q", 0, 512), ("a_k", 512, 128), ("a_v", 640, 128),
    ("b_q", 768, 512), ("b_k", 1280, 512), ("b_v", 1792, 512),
    ("c_q", 2304, 512), ("c_k", 2816, 512), ("c_v", 3328, 512),
    ("f", 3840, 512),
)
IN_W = 4352
_Q_GROUPS = ("a_q", "b_q", "c_q")
_ROPED = ("a_q", "a_k", "b_q", "b_k")
_CTX_F32 = ("a_k", "a_v", "b_k", "b_v", "c_k", "c_v")
_CTX_SEQ_MINOR = ("a_k", "a_v", "b_k", "c_k", "c_v")

LANES = 128
VMEM_LIMIT = 56 * 1024 * 1024

TM_PROJ = 512
TQ_SINK = 256
TQ_DIFF = 512
NA_QROWS = 4
NA_BAND = NA_ROWS + NA_QROWS - 1
TN_MERGE = 256
TN_MOD = 1024
ROUTER_W = 128

_NT = (((1,), (1,)), ((), ()))


def _cparams(sem):
    return pltpu.CompilerParams(dimension_semantics=sem, vmem_limit_bytes=VMEM_LIMIT)


def _sigmoid(x):
    return 1.0 / (1.0 + jnp.exp(-x))


def _norm_mod(x, g, shift, scale):
    ms = jnp.mean(x * x, axis=-1, keepdims=True)
    return (x * lax.rsqrt(ms + EPS) * g) * (1.0 + scale) + shift


def _mod_kernel(c_ref, w_ref, b_ref, o_ref):
    c = c_ref[...]
    s = (c * _sigmoid(c)).astype(BF16)
    o_ref[0] = jnp.dot(s, w_ref[0].astype(BF16), preferred_element_type=F32) + b_ref[0]


def _modulation(cond, w_mod, b_mod):
    n = cond.shape[0]
    out_w = w_mod.shape[-1]
    return pl.pallas_call(
        _mod_kernel,
        out_shape=jax.ShapeDtypeStruct((DEPTH, n, out_w), F32),
        grid=(DEPTH, out_w // TN_MOD),
        in_specs=[
            pl.BlockSpec((n, D_MODEL), lambda l, j: (0, 0)),
            pl.BlockSpec((1, D_MODEL, TN_MOD), lambda l, j: (l, 0, j)),
            pl.BlockSpec((1, 1, TN_MOD), lambda l, j: (l, 0, j)),
        ],
        out_specs=pl.BlockSpec((1, n, TN_MOD), lambda l, j: (l, 0, j)),
        compiler_params=_cparams(("parallel", "parallel")),
        name="modulation",
    )(cond, w_mod, b_mod.reshape(DEPTH, 1, out_w))


def _rope128(y, cos, sin):
    lane = lax.broadcasted_iota(jnp.int32, y.shape, 1)
    first = (lane % 32) < 16
    partner = jnp.where(first, pltpu.roll(y, LANES - 16, 1), pltpu.roll(y, 16, 1))
    return y * cos + partner * sin


def _inproj_kernel(*refs, latent, n_alias, seq):
    if latent:
        x_ref, mod_ref, g_ref, w_ref, cos_ref, sin_ref = refs[:6]
        out_refs = refs[6:]
    else:
        x_ref, mod_ref, g_ref, w_ref = refs[:4]
        out_refs = refs[4 + n_alias:]
    h = _norm_mod(x_ref[...], g_ref[...], mod_ref[0, 0:1, :], mod_ref[0, 1:2, :]).astype(BF16)
    for (name, start, width), o_ref in zip(_IN_GROUPS, out_refs):
        y = jnp.dot(h, w_ref[:, start:start + width], preferred_element_type=F32)
        if name in _Q_GROUPS:
            y = y * SCALE
        if latent and name in _ROPED:
            cos = cos_ref[...]
            sin = sin_ref[...]
            for c0 in range(0, width, LANES):
                o_ref[:, c0:c0 + LANES] = _rope128(y[:, c0:c0 + LANES], cos, sin).astype(o_ref.dtype)
        elif not latent and name in _CTX_F32:
            for bb in range(o_ref.shape[0]):
                yb = y[bb * seq:(bb + 1) * seq]
                if name in _CTX_SEQ_MINOR:
                    o_ref[bb] = yb.T
                else:
                    for hd in range(DIFF_HEADS):
                        o_ref[bb, :, hd, :] = _head(yb, hd, DIFF_V_DIM)
        else:
            o_ref[...] = y.astype(o_ref.dtype)


def _inproj(x2d, mod, g, w_in_bf, tokens_per_mod, *, rope=None, seq=None, layer=0, prev_kv=None):
    t = x2d.shape[0]
    tm = TM_PROJ
    latent = rope is not None
    in_specs = [
        pl.BlockSpec((tm, D_MODEL), lambda i: (i, 0)),
        pl.BlockSpec((1, 6, D_MODEL), lambda i: ((i * tm) // tokens_per_mod, 0, 0)),
        pl.BlockSpec((1, D_MODEL), lambda i: (0, 0)),
        pl.BlockSpec((None, D_MODEL, IN_W), lambda i: (layer, 0, 0), pipeline_mode=pl.Buffered(1)),
    ]
    args = [x2d, mod, g.reshape(1, D_MODEL), w_in_bf]
    aliases = {}
    n_alias = 0
    if latent:
        nblk = rope[0].shape[0] // tm
        in_specs += [pl.BlockSpec((tm, LANES), lambda i: (i % nblk, 0))] * 2
        args += list(rope)
    elif prev_kv is not None:
        n_alias = len(_CTX_F32)
        in_specs += [pl.BlockSpec(memory_space=pl.ANY)] * n_alias
        args += [prev_kv[name] for name in _CTX_F32]
    out_shape, out_specs = [], []
    tb = tm // seq if not latent else None
    for oi, (name, _, width) in enumerate(_IN_GROUPS):
        if not latent and name in _CTX_F32:
            blk = (width, seq) if name in _CTX_SEQ_MINOR else (seq, DIFF_HEADS, DIFF_V_DIM)
            out_shape.append(jax.ShapeDtypeStruct((t // seq, DEPTH) + blk, F32))
            out_specs.append(pl.BlockSpec((tb, None) + blk,
                                          lambda i, nd=len(blk): (i, layer) + (0,) * nd))
            if prev_kv is not None:
                aliases[4 + _CTX_F32.index(name)] = oi
        else:
            out_shape.append(jax.ShapeDtypeStruct((t, width), BF16))
            out_specs.append(pl.BlockSpec((tm, width), lambda i: (i, 0)))
    outs = pl.pallas_call(
        functools.partial(_inproj_kernel, latent=latent, n_alias=n_alias, seq=seq),
        out_shape=out_shape,
        grid=(t // tm,),
        in_specs=in_specs,
        out_specs=out_specs,
        input_output_aliases=aliases,
        compiler_params=_cparams(("parallel",)),
        name="inproj_latent" if latent else "inproj_context",
    )(*args)
    return dict(zip([n for n, _, _ in _IN_GROUPS], outs))


def _attend(q, segs, sink_col=None):
    scores = []
    for k, _, bias, k_seq_minor, _ in segs:
        if k_seq_minor:
            s = jnp.dot(q, k, preferred_element_type=F32)
        else:
            s = lax.dot_general(q, k, _NT, preferred_element_type=F32)
        if bias is not None:
            s = s + bias
        scores.append(s)
    m = scores[0].max(axis=-1, keepdims=True)
    for s in scores[1:]:
        m = jnp.maximum(m, s.max(axis=-1, keepdims=True))
    if sink_col is not None:
        m = jnp.maximum(m, sink_col)
    acc = None
    den = None
    for s, (_, v, _, _, v_seq_minor) in zip(scores, segs):
        p = jnp.exp(s - m)
        ps = p.sum(axis=-1, keepdims=True)
        if v_seq_minor:
            o = lax.dot_general(p.astype(BF16), v, _NT, preferred_element_type=F32)
        else:
            o = jnp.dot(p.astype(BF16), v, preferred_element_type=F32)
        acc = o if acc is None else acc + o
        den = ps if den is None else den + ps
    if sink_col is not None:
        den = den + jnp.exp(sink_col - m)
    return acc, den


def _head(x, h, width=HEAD_DIM):
    return x[:, h * width:(h + 1) * width]


def _head_rows(x, h, width=HEAD_DIM):
    return x[h * width:(h + 1) * width, :]


def _kv_spec(width, seq, layer):
    return pl.BlockSpec((None, None, width, seq), lambda b, i: (b, layer, 0, 0))


def _sink_attn_kernel(*refs, latent, tq, seq):
    if latent:
        sink_ref, q_ref, k_ref, v_ref, kc_ref, vc_ref, o_ref = refs
    else:
        sink_ref, q_ref, kc_ref, vc_ref, o_ref = refs
    q_all = q_ref[...]
    k_ctx = kc_ref[...].astype(BF16)
    v_ctx = vc_ref[...].astype(BF16)
    if latent:
        band = tq + 2 * WIN
        qb = pl.program_id(1)
        start = pl.multiple_of(jnp.clip(qb * tq - WIN, 0, seq - band), WIN)
        k_loc = k_ref[pl.ds(start, band), :]
        v_loc = v_ref[pl.ds(start, band), :]
        rows = A_GROUP * tq
        qpos = qb * tq + lax.broadcasted_iota(jnp.int32, (rows, band), 0) % tq
        kpos = start + lax.broadcasted_iota(jnp.int32, (rows, band), 1)
        bias = jnp.where(jnp.abs(qpos - kpos) <= WIN, 0.0, NEG).astype(F32)
    outs = [None] * A_HEADS
    for kv in range(A_KV_HEADS):
        heads = [kv * A_GROUP + g for g in range(A_GROUP)]
        q = jnp.concatenate([_head(q_all, h) for h in heads], axis=0)
        sink_col = jnp.concatenate(
            [jnp.full((tq, 1), sink_ref[0, h], F32) for h in heads], axis=0)
        segs = [(_head_rows(k_ctx, kv), _head_rows(v_ctx, kv), None, True, True)]
        if latent:
            segs.append((_head(k_loc, kv), _head(v_loc, kv), bias, False, False))
        acc, den = _attend(q, segs, sink_col)
        o = acc / den
        for g, h in enumerate(heads):
            outs[h] = o[g * tq:(g + 1) * tq]
    o_ref[...] = jnp.concatenate(outs, axis=1).astype(o_ref.dtype)


def _sink_attn(sink, q, kv_ctx, *, nb, seq, layer, kv_loc=None):
    latent = kv_loc is not None
    tq = TQ_SINK if latent else seq
    nq = seq // tq
    kw = A_KV_HEADS * HEAD_DIM
    s_ctx = kv_ctx[0].shape[-1]
    in_specs = [
        pl.BlockSpec(memory_space=pltpu.SMEM),
        pl.BlockSpec((tq, BR_W), lambda b, i: (b * nq + i, 0)),
    ]
    args = [sink.reshape(1, A_HEADS), q]
    if latent:
        in_specs += [pl.BlockSpec((seq, kw), lambda b, i: (b, 0))] * 2
        args += list(kv_loc)
    in_specs += [_kv_spec(kw, s_ctx, layer)] * 2
    args += list(kv_ctx)
    return pl.pallas_call(
        functools.partial(_sink_attn_kernel, latent=latent, tq=tq, seq=seq),
        out_shape=jax.ShapeDtypeStruct(q.shape, BF16),
        grid=(nb, nq),
        in_specs=in_specs,
        out_specs=pl.BlockSpec((tq, BR_W), lambda b, i: (b * nq + i, 0)),
        compiler_params=_cparams(("parallel", "arbitrary")),
        name="sink_attn_latent" if latent else "sink_attn_context",
    )(*args)


def _diff_attn_kernel(*refs, latent, lam_init):
    if latent:
        lam_ref, g_ref, q_ref, k_ref, v_ref, kc_ref, vc_ref, o_ref = refs
    else:
        lam_ref, g_ref, q_ref, kc_ref, vc_ref, o_ref = refs
    lam = lam_ref[...]
    lam_a = jnp.sum(lam[0:1] * lam[1:2], axis=-1, keepdims=True)
    lam_b = jnp.sum(lam[2:3] * lam[3:4], axis=-1, keepdims=True)
    lam_full = jnp.exp(lam_a) - jnp.exp(lam_b) + lam_init
    q_all = q_ref[...]
    k_ctx = kc_ref[...].astype(BF16)
    if latent:
        k_loc = k_ref[...]
        v_loc = v_ref[...]
    outs = []
    for h in range(DIFF_HEADS):
        v_ctx = vc_ref[:, h, :].astype(BF16)
        res = []
        for sub in range(2):
            hh = 2 * h + sub
            segs = [(_head_rows(k_ctx, hh), v_ctx, None, True, False)]
            if latent:
                segs.append((_head(k_loc, hh), _head(v_loc, h, DIFF_V_DIM), None, False, False))
            acc, den = _attend(_head(q_all, hh), segs)
            res.append(acc / den)
        o = res[0] - lam_full * res[1]
        ms = jnp.mean(o * o, axis=-1, keepdims=True)
        outs.append(o * lax.rsqrt(ms + EPS) * g_ref[...] * (1.0 - lam_init))
    o_ref[...] = jnp.concatenate(outs, axis=1).astype(o_ref.dtype)


def _diff_attn(lam, sub_g, q, kv_ctx, *, nb, seq, lam_init, layer, kv_loc=None):
    latent = kv_loc is not None
    tq = TQ_DIFF if latent else seq
    nq = seq // tq
    kw = DIFF_HEADS * 2 * HEAD_DIM
    s_ctx = kv_ctx[0].shape[-1]
    in_specs = [
        pl.BlockSpec((4, HEAD_DIM), lambda b, i: (0, 0)),
        pl.BlockSpec((1, DIFF_V_DIM), lambda b, i: (0, 0)),
        pl.BlockSpec((tq, BR_W), lambda b, i: (b * nq + i, 0)),
    ]
    args = [lam, sub_g.reshape(1, DIFF_V_DIM), q]
    if latent:
        in_specs += [pl.BlockSpec((seq, kw), lambda b, i: (b, 0))] * 2
        args += list(kv_loc)
    in_specs += [_kv_spec(kw, s_ctx, layer),
                 pl.BlockSpec((None, None, s_ctx, DIFF_HEADS, DIFF_V_DIM),
                              lambda b, i: (b, layer, 0, 0, 0))]
    args += list(kv_ctx)
    return pl.pallas_call(
        functools.partial(_diff_attn_kernel, latent=latent, lam_init=lam_init),
        out_shape=jax.ShapeDtypeStruct(q.shape, BF16),
        grid=(nb, nq),
        in_specs=in_specs,
        out_specs=pl.BlockSpec((tq, BR_W), lambda b, i: (b * nq + i, 0)),
        compiler_params=_cparams(("parallel", "arbitrary")),
        name="diff_attn_latent" if latent else "diff_attn_context",
    )(*args)


def _dense_attn_kernel(q_ref, k_ref, v_ref, o_ref):
    q_all = q_ref[...]
    k = k_ref[...].astype(BF16)
    v = v_ref[...].astype(BF16)
    outs = []
    for h in range(NA_HEADS):
        acc, den = _attend(_head(q_all, h), [(_head_rows(k, h), _head_rows(v, h), None, True, True)])
        outs.append(acc / den)
    o_ref[...] = jnp.concatenate(outs, axis=1).astype(o_ref.dtype)


def _dense_attn(q, kv_ctx, *, nb, seq, layer):
    return pl.pallas_call(
        _dense_attn_kernel,
        out_shape=jax.ShapeDtypeStruct(q.shape, BF16),
        grid=(nb, 1),
        in_specs=[pl.BlockSpec((seq, BR_W), lambda b, i: (b, 0)),
                  _kv_spec(BR_W, seq, layer), _kv_spec(BR_W, seq, layer)],
        out_specs=pl.BlockSpec((seq, BR_W), lambda b, i: (b, 0)),
        compiler_params=_cparams(("parallel", "arbitrary")),
        name="dense_attn_context",
    )(q, *kv_ctx)


def _na_window_start(r, rows):
    return jnp.clip(r - NA_ROWS // 2, 0, rows - NA_ROWS)


def _na_band_start(step, rows):
    return jnp.clip(step * NA_QROWS - NA_ROWS // 2, 0, rows - NA_BAND)


def _na_attn_kernel(q_ref, k_ref, v_ref, kc_ref, vc_ref, tab_ref, o_ref, bias_ref, *, rows):
    step = pl.program_id(0)
    band_row = _na_band_start(step, rows)

    @pl.when(pl.program_id(1) == 0)
    def _():
        for ri in range(NA_QROWS):
            r = step * NA_QROWS + ri
            rs = _na_window_start(r, rows)
            for h in range(NA_HEADS):
                blocks = []
                for j in range(NA_BAND):
                    kr = band_row + j
                    ok = (kr >= rs) & (kr < rs + NA_ROWS)
                    dr = jnp.clip(kr - r + NA_ROWS - 1, 0, 2 * NA_ROWS - 2)
                    blocks.append(jnp.where(ok, tab_ref[h, dr], NEG))
                bias_ref[h, ri * GRID_W:(ri + 1) * GRID_W, :] = jnp.concatenate(blocks, axis=1)

    start = pl.multiple_of(band_row * GRID_W, GRID_W)
    band = NA_BAND * GRID_W
    q_all = q_ref[...]
    k_loc = k_ref[pl.ds(start, band), :]
    v_loc = v_ref[pl.ds(start, band), :]
    k_ctx = kc_ref[...].astype(BF16)
    v_ctx = vc_ref[...].astype(BF16)
    outs = []
    for h in range(NA_HEADS):
        segs = [(_head_rows(k_ctx, h), _head_rows(v_ctx, h), None, True, True),
                (_head(k_loc, h), _head(v_loc, h), bias_ref[h], False, False)]
        acc, den = _attend(_head(q_all, h), segs)
        outs.append(acc / den)
    o_ref[...] = jnp.concatenate(outs, axis=1).astype(o_ref.dtype)


def _na_attn(q, kv_loc, kv_ctx, rpb_tab, *, nb, seq, layer):
    rows = seq // GRID_W
    nsteps = rows // NA_QROWS
    tq = NA_QROWS * GRID_W
    band = NA_BAND * GRID_W
    past = kv_ctx[0].shape[-1]
    for s in range(nsteps):
        us = min(max(s * NA_QROWS - NA_ROWS // 2, 0), rows - NA_BAND)
        for r in range(s * NA_QROWS, (s + 1) * NA_QROWS):
            rs = min(max(r - NA_ROWS // 2, 0), rows - NA_ROWS)
            assert us <= rs and rs + NA_ROWS <= us + NA_BAND
    ctx_spec = pl.BlockSpec((None, None, BR_W, past), lambda s, b: (b, layer, 0, 0))
    return pl.pallas_call(
        functools.partial(_na_attn_kernel, rows=rows),
        out_shape=jax.ShapeDtypeStruct(q.shape, BF16),
        grid=(nsteps, nb),
        in_specs=[
            pl.BlockSpec((tq, BR_W), lambda s, b: (b * nsteps + s, 0)),
            pl.BlockSpec((seq, BR_W), lambda s, b: (b, 0)),
            pl.BlockSpec((seq, BR_W), lambda s, b: (b, 0)),
            ctx_spec, ctx_spec,
            pl.BlockSpec(rpb_tab.shape, lambda s, b: (0, 0, 0, 0)),
        ],
        out_specs=pl.BlockSpec((tq, BR_W), lambda s, b: (b * nsteps + s, 0)),
        scratch_shapes=[pltpu.VMEM((NA_HEADS, tq, band), F32)],
        compiler_params=_cparams(("arbitrary", "arbitrary")),
        name="na_attn_latent",
    )(q, *kv_loc, *kv_ctx, rpb_tab)


def _rpb_expand_kernel(rpb_ref, onehot_ref, o_ref):
    o_ref[...] = jnp.dot(rpb_ref[...], onehot_ref[...], preferred_element_type=F32,
                         precision=lax.Precision.HIGHEST)


def _na_rpb_table(rpb):
    n_dr, n_dc = 2 * NA_ROWS - 1, 2 * NA_COLS - 1
    qcol = np.arange(GRID_W)
    kcol = np.arange(GRID_W)
    cs = np.clip(qcol - NA_COLS // 2, 0, GRID_W - NA_COLS)
    col_ok = (kcol[None, :] >= cs[:, None]) & (kcol[None, :] < cs[:, None] + NA_COLS)
    dc = np.clip(kcol[None, :] - qcol[:, None] + NA_COLS - 1, 0, n_dc - 1)
    onehot = np.zeros((32, GRID_W * GRID_W), np.float32)
    onehot[dc.reshape(-1), np.arange(GRID_W * GRID_W)] = 1.0
    rpb2 = jnp.pad(rpb.reshape(NA_HEADS * n_dr, n_dc), ((0, 0), (0, 32 - n_dc)))
    full = pl.pallas_call(
        _rpb_expand_kernel,
        out_shape=jax.ShapeDtypeStruct((NA_HEADS * n_dr, GRID_W * GRID_W), F32),
        name="rpb_expand",
    )(rpb2, jnp.asarray(onehot))
    full = full.reshape(NA_HEADS, n_dr, GRID_W, GRID_W)
    return full + jnp.asarray(np.where(col_ok, 0.0, NEG).astype(np.float32))


def _dft_tables(seq):
    pos = np.arange(seq)
    ang = 2.0 * np.pi * ((pos[:, None] * pos[None, :]) % seq) / seq
    left = np.concatenate([np.cos(ang), -np.sin(ang)], axis=1).astype(np.float32)
    ch = np.arange(F_GROUP_W)
    angw = 2.0 * np.pi * ((ch[:, None] * ch[None, :]) % F_GROUP_W) / F_GROUP_W
    right = np.concatenate([np.cos(angw), np.sin(angw)], axis=1).astype(np.float32)
    return jnp.asarray(left).astype(BF16), jnp.asarray(right).astype(BF16)


def _fourier_kernel(f_ref, left_ref, right_ref, o_ref, stack_ref, *, seq):
    f = f_ref[...]
    right = right_ref[...]
    for g in range(F_GROUPS):
        t = jnp.dot(_head(f, g, F_GROUP_W), right, preferred_element_type=F32).astype(BF16)
        stack_ref[0:seq, g * F_GROUP_W:(g + 1) * F_GROUP_W] = t[:, :F_GROUP_W]
        stack_ref[seq:2 * seq, g * F_GROUP_W:(g + 1) * F_GROUP_W] = t[:, F_GROUP_W:]
    y = jnp.dot(left_ref[...], stack_ref[...], preferred_element_type=F32)
    o_ref[...] = (y * (1.0 / math.sqrt(seq * F_GROUP_W))).astype(o_ref.dtype)


def _fourier(f, *, nb, seq):
    left, right = _dft_tables(seq)
    w = F_GROUPS * F_GROUP_W
    return pl.pallas_call(
        functools.partial(_fourier_kernel, seq=seq),
        out_shape=jax.ShapeDtypeStruct(f.shape, BF16),
        grid=(nb,),
        in_specs=[
            pl.BlockSpec((seq, w), lambda b: (b, 0)),
            pl.BlockSpec((seq, 2 * seq), lambda b: (0, 0)),
            pl.BlockSpec((F_GROUP_W, 2 * F_GROUP_W), lambda b: (0, 0)),
        ],
        out_specs=pl.BlockSpec((seq, w), lambda b: (b, 0)),
        scratch_shapes=[pltpu.VMEM((2 * seq, w), BF16)],
        compiler_params=_cparams(("parallel",)),
        name=f"fourier_{seq}",
    )(f, left, right)


def _merge_kernel(x_ref, mod_ref, g_ref, *refs):
    br_refs = refs[0:N_BRANCH]
    wbr_ref = refs[N_BRANCH]
    wg_refs = refs[N_BRANCH + 1:2 * N_BRANCH + 1]
    bg_refs = refs[2 * N_BRANCH + 1:3 * N_BRANCH + 1]
    wo_ref, o_ref, h_ref, acc_ref = refs[3 * N_BRANCH + 1:]
    j = pl.program_id(1)

    @pl.when(j == 0)
    def _():
        h_ref[...] = _norm_mod(x_ref[...], g_ref[...], mod_ref[0, 0:1, :],
                               mod_ref[0, 1:2, :]).astype(BF16)
        acc_ref[...] = jnp.zeros_like(acc_ref)

    h = h_ref[...]
    mix = None
    for n in range(N_BRANCH):
        gate = _sigmoid(jnp.dot(h, wg_refs[n][...], preferred_element_type=F32) + bg_refs[n][0])
        proj = jnp.dot(br_refs[n][...], wbr_ref[n], preferred_element_type=F32)
        term = gate * proj
        mix = term if mix is None else mix + term
    acc_ref[...] += jnp.dot(mix.astype(BF16), wo_ref[...], preferred_element_type=F32)

    @pl.when(j == pl.num_programs(1) - 1)
    def _():
        o_ref[...] = x_ref[...] + mod_ref[0, 2:3, :] * acc_ref[...]


def _merge(x2d, mod, g, branches, w_branch_bf, w_gate_bf, b_gate, w_out_bf, tokens_per_mod, layer):
    t = x2d.shape[0]
    tm, tn = TM_PROJ, TN_MERGE
    nch = D_MODEL // tn
    br_spec = pl.BlockSpec((tm, BR_W), lambda i, j: (i, 0))
    wg_specs = [pl.BlockSpec((None, D_MODEL, tn),
                             functools.partial(lambda i, j, n: (layer, 0, n * nch + j), n=n))
                for n in range(N_BRANCH)]
    bg_specs = [pl.BlockSpec((1, 1, tn), functools.partial(lambda i, j, n: (n * nch + j, 0, 0), n=n))
                for n in range(N_BRANCH)]
    bg = b_gate.reshape(N_BRANCH * nch, 1, tn)
    return pl.pallas_call(
        _merge_kernel,
        out_shape=jax.ShapeDtypeStruct((t, D_MODEL), F32),
        grid=(t // tm, nch),
        in_specs=[
            pl.BlockSpec((tm, D_MODEL), lambda i, j: (i, 0)),
            pl.BlockSpec((1, 6, D_MODEL), lambda i, j: ((i * tm) // tokens_per_mod, 0, 0)),
            pl.BlockSpec((1, D_MODEL), lambda i, j: (0, 0)),
            br_spec, br_spec, br_spec, br_spec,
            pl.BlockSpec((None, N_BRANCH, BR_W, tn), lambda i, j: (layer, 0, 0, j)),
            *wg_specs, *bg_specs,
            pl.BlockSpec((None, tn, D_MODEL), lambda i, j: (layer, j, 0)),
        ],
        out_specs=pl.BlockSpec((tm, D_MODEL), lambda i, j: (i, 0)),
        scratch_shapes=[pltpu.VMEM((tm, D_MODEL), BF16), pltpu.VMEM((tm, D_MODEL), F32)],
        compiler_params=_cparams(("parallel", "arbitrary")),
        name="merge",
    )(x2d, mod, g.reshape(1, D_MODEL), *branches, w_branch_bf,
      *([w_gate_bf] * N_BRANCH), *([bg] * N_BRANCH), w_out_bf)


def _route(logits):
    lane = lax.broadcasted_iota(jnp.int32, logits.shape, 1)
    ninf = -jnp.inf
    lg = jnp.where(lane < N_GROUPS, logits, ninf)
    mg = lg.max(axis=-1, keepdims=True)
    pg_top = 1.0 / jnp.exp(lg - mg).sum(axis=-1, keepdims=True)
    g_idx = jnp.where(lg == mg, lane, ROUTER_W).min(axis=-1, keepdims=True)
    lo = N_GROUPS + EXP_PER_GROUP * g_idx
    le = jnp.where((lane >= lo) & (lane < lo + EXP_PER_GROUP), logits, ninf)
    m1 = le.max(axis=-1, keepdims=True)
    e1 = jnp.where(le == m1, lane, ROUTER_W).min(axis=-1, keepdims=True)
    le2 = jnp.where(lane == e1, ninf, le)
    m2 = le2.max(axis=-1, keepdims=True)
    e2 = jnp.where(le2 == m2, lane, ROUTER_W).min(axis=-1, keepdims=True)
    se = jnp.exp(le - m1).sum(axis=-1, keepdims=True)
    pe1 = 1.0 / se
    pe2 = jnp.exp(m2 - m1) / se
    tot = pe1 + pe2
    w1 = pe1 / tot * pg_top
    w2 = pe2 / tot * pg_top
    return jnp.where(lane == e1, w1, 0.0) + jnp.where(lane == e2, w2, 0.0)


def _moe_kernel(x_ref, mod_ref, g_ref, wr_ref, br_ref, wg_ref, wu_ref, wd_ref, fg_ref,
                o_ref, h_ref, comb_ref, acc_ref, *, final):
    e = pl.program_id(1)

    @pl.when(e == 0)
    def _():
        h = _norm_mod(x_ref[...], g_ref[...], mod_ref[0, 3:4, :], mod_ref[0, 4:5, :])
        h_hi = h.astype(BF16)
        h_ref[...] = h_hi
        h_lo = (h - h_hi.astype(F32)).astype(BF16)
        w = wr_ref[...]
        w_hi = w.astype(BF16)
        w_lo = (w - w_hi.astype(F32)).astype(BF16)
        logits = (jnp.dot(h_hi, w_hi, preferred_element_type=F32)
                  + jnp.dot(h_lo, w_hi, preferred_element_type=F32)
                  + jnp.dot(h_hi, w_lo, preferred_element_type=F32)) + br_ref[...]
        comb_ref[...] = _route(logits)
        acc_ref[...] = jnp.zeros_like(acc_ref)

    lane = lax.broadcasted_iota(jnp.int32, comb_ref.shape, 1)
    cw = jnp.where(lane == N_GROUPS + e, comb_ref[...], 0.0).sum(axis=-1, keepdims=True)
    h = h_ref[...]
    gate = jnp.dot(h, wg_ref[...], preferred_element_type=F32)
    up = jnp.dot(h, wu_ref[...], preferred_element_type=F32)
    hid = (gate * _sigmoid(gate)) * up * cw
    acc_ref[...] += jnp.dot(hid.astype(BF16), wd_ref[...], preferred_element_type=F32)

    @pl.when(e == pl.num_programs(1) - 1)
    def _():
        y = x_ref[...] + mod_ref[0, 5:6, :] * acc_ref[...]
        if final:
            ms = jnp.mean(y * y, axis=-1, keepdims=True)
            y = y * lax.rsqrt(ms + EPS) * fg_ref[...]
        o_ref[...] = y


def _moe(x2d, mod, g, w_router, b_router, w_g_bf, w_u_bf, w_d_bf, final_g, tokens_per_mod, final,
         layer):
    t = x2d.shape[0]
    tm = TM_PROJ
    return pl.pallas_call(
        functools.partial(_moe_kernel, final=final),
        out_shape=jax.ShapeDtypeStruct((t, D_MODEL), F32),
        grid=(t // tm, N_EXPERTS),
        in_specs=[
            pl.BlockSpec((tm, D_MODEL), lambda i, e: (i, 0)),
            pl.BlockSpec((1, 6, D_MODEL), lambda i, e: ((i * tm) // tokens_per_mod, 0, 0)),
            pl.BlockSpec((1, D_MODEL), lambda i, e: (0, 0)),
            pl.BlockSpec((D_MODEL, ROUTER_W), lambda i, e: (0, 0)),
            pl.BlockSpec((1, ROUTER_W), lambda i, e: (0, 0)),
            pl.BlockSpec((None, None, D_MODEL, D_EXPERT), lambda i, e: (layer, e, 0, 0)),
            pl.BlockSpec((None, None, D_MODEL, D_EXPERT), lambda i, e: (layer, e, 0, 0)),
            pl.BlockSpec((None, None, D_EXPERT, D_MODEL), lambda i, e: (layer, e, 0, 0)),
            pl.BlockSpec((1, D_MODEL), lambda i, e: (0, 0)),
        ],
        out_specs=pl.BlockSpec((tm, D_MODEL), lambda i, e: (i, 0)),
        scratch_shapes=[pltpu.VMEM((tm, D_MODEL), BF16), pltpu.VMEM((tm, ROUTER_W), F32),
                        pltpu.VMEM((tm, D_MODEL), F32)],
        compiler_params=_cparams(("parallel", "arbitrary")),
        name="moe_final" if final else "moe",
    )(x2d, mod, g.reshape(1, D_MODEL), w_router, b_router, w_g_bf, w_u_bf, w_d_bf,
      final_g.reshape(1, D_MODEL))


def _rope_tables(seq):
    nf = HEAD_DIM // 4
    t = jnp.arange(seq)
    inv = ROPE_BASE ** (-jnp.arange(nf, dtype=F32) / nf)
    pos = jnp.stack([t // GRID_W, t % GRID_W], -1).astype(F32)
    ang = pos[:, :, None] * inv
    cos, sin = jnp.cos(ang), jnp.sin(ang)
    cos64 = jnp.concatenate([cos[:, 0], cos[:, 0], cos[:, 1], cos[:, 1]], axis=-1)
    sin64 = jnp.concatenate([-sin[:, 0], sin[:, 0], -sin[:, 1], sin[:, 1]], axis=-1)
    return jnp.tile(cos64, (1, 2)), jnp.tile(sin64, (1, 2))


def _seq_minor_view(cache):
    nb, depth, seq = cache.shape[:3]
    nd = cache.ndim
    return jnp.transpose(cache, (0, 1, *range(3, nd), 2)).reshape(nb, depth, -1, seq)


def _seq_major_view(arr, head_dims):
    nb, depth, _, seq = arr.shape
    nd = 3 + len(head_dims)
    arr = arr.reshape(nb, depth, *head_dims, seq)
    return jnp.transpose(arr, (0, 1, nd - 1, *range(2, nd - 1)))


def kernel(x_prompt, x_sample, cache_attn_k, cache_attn_v, cache_diff_k, cache_diff_v, cache_na_k, cache_na_v, c, c_ctx, w_mod, b_mod, norm1_g, w_in, attn_sink, diff_lambda, diff_sub_g, na_rpb, w_branch, w_gate, b_gate, w_out, norm2_g, w_router_group, b_router_group, w_router_expert, b_router_expert, w_exp_gate, w_exp_up, w_exp_down, final_g):
    nbp, seq_p, _ = x_prompt.shape
    nbs, seq_s, _ = x_sample.shape
    xp = x_prompt.reshape(nbp * seq_p, D_MODEL)
    xs = x_sample.reshape(nbs * seq_s, D_MODEL)

    cond = jnp.concatenate([c_ctx[None, :], c], axis=0)
    n_cond = -(-cond.shape[0] // 8) * 8
    cond = jnp.pad(cond, ((0, n_cond - cond.shape[0]), (0, 0)))
    mods = _modulation(cond, w_mod, b_mod).reshape(DEPTH, n_cond, 6, D_MODEL)

    rope = _rope_tables(seq_s)
    cache_a = (_seq_minor_view(cache_attn_k), _seq_minor_view(cache_attn_v))
    cache_b = (_seq_minor_view(cache_diff_k), cache_diff_v)
    cache_c = (_seq_minor_view(cache_na_k), _seq_minor_view(cache_na_v))

    w_in_bf = w_in.astype(BF16)
    w_branch_bf = w_branch.astype(BF16)
    w_gate_bf = w_gate.astype(BF16)
    w_out_bf = w_out.astype(BF16)
    w_eg_bf = w_exp_gate.astype(BF16)
    w_eu_bf = w_exp_up.astype(BF16)
    w_ed_bf = w_exp_down.astype(BF16)
    pad = ROUTER_W - N_GROUPS - N_EXPERTS
    w_router = jnp.concatenate(
        [w_router_group, w_router_expert, jnp.zeros((DEPTH, D_MODEL, pad), F32)], axis=-1)
    b_router = jnp.concatenate(
        [b_router_group, b_router_expert, jnp.zeros((DEPTH, pad), F32)], axis=-1)

    kv = None
    for l in range(DEPTH):
        lam_init = 0.8 - 0.6 * math.exp(-0.3 * l)
        mod_p = mods[l, 0:1]
        mod_s = mods[l, 1:1 + nbs]
        final = l == DEPTH - 1
        moe_w = (w_router[l], b_router[l].reshape(1, ROUTER_W), w_eg_bf, w_eu_bf, w_ed_bf)
        merge_w = (w_branch_bf, w_gate_bf, b_gate[l], w_out_bf)

        pr = _inproj(xp, mod_p, norm1_g[l], w_in_bf, xp.shape[0], seq=seq_p, layer=l, prev_kv=kv)
        kv = {name: pr[name] for name in _CTX_F32}
        o_a = _sink_attn(attn_sink[l], pr["a_q"], (kv["a_k"], kv["a_v"]), nb=nbp, seq=seq_p, layer=l)
        o_b = _diff_attn(diff_lambda[l], diff_sub_g[l], pr["b_q"], (kv["b_k"], kv["b_v"]),
                         nb=nbp, seq=seq_p, lam_init=lam_init, layer=l)
        o_c = _dense_attn(pr["c_q"], (kv["c_k"], kv["c_v"]), nb=nbp, seq=seq_p, layer=l)
        o_f = _fourier(pr["f"], nb=nbp, seq=seq_p)
        xp = _merge(xp, mod_p, norm1_g[l], (o_a, o_b, o_c, o_f), *merge_w, xp.shape[0], l)
        xp = _moe(xp, mod_p, norm2_g[l], *moe_w, final_g, xp.shape[0], final, l)

        pr = _inproj(xs, mod_s, norm1_g[l], w_in_bf, seq_s, rope=rope, layer=l)
        o_a = _sink_attn(attn_sink[l], pr["a_q"], cache_a, nb=nbs, seq=seq_s, layer=l,
                         kv_loc=(pr["a_k"], pr["a_v"]))
        o_b = _diff_attn(diff_lambda[l], diff_sub_g[l], pr["b_q"], cache_b, nb=nbs, seq=seq_s,
                         lam_init=lam_init, layer=l, kv_loc=(pr["b_k"], pr["b_v"]))
        o_c = _na_attn(pr["c_q"], (pr["c_k"], pr["c_v"]), cache_c,
                       _na_rpb_table(na_rpb[l]), nb=nbs, seq=seq_s, layer=l)
        o_f = _fourier(pr["f"], nb=nbs, seq=seq_s)
        xs = _merge(xs, mod_s, norm1_g[l], (o_a, o_b, o_c, o_f), *merge_w, seq_s, l)
        xs = _moe(xs, mod_s, norm2_g[l], *moe_w, final_g, seq_s, final, l)

    return (
        xp.reshape(x_prompt.shape),
        xs.reshape(x_sample.shape),
        _seq_major_view(kv["a_k"], (A_KV_HEADS, HEAD_DIM)),
        _seq_major_view(kv["a_v"], (A_KV_HEADS, HEAD_DIM)),
        _seq_major_view(kv["b_k"], (DIFF_HEADS, 2, HEAD_DIM)),
        kv["b_v"],
        _seq_major_view(kv["c_k"], (NA_HEADS, HEAD_DIM)),
        _seq_major_view(kv["c_v"], (NA_HEADS, HEAD_DIM)),
    )
```

```python
import functools
import math

import jax
import jax.numpy as jnp
import numpy as np
from jax import lax
from jax.experimental import pallas as pl
from jax.experimental.pallas import tpu as pltpu

F32 = jnp.float32
BF16 = jnp.bfloat16

D_MODEL = 2048
DEPTH = 2
GRID_W = 64
HEAD_DIM = 64
A_HEADS = 8
A_KV_HEADS = 2
A_GROUP = A_HEADS // A_KV_HEADS
WIN = 128
DIFF_HEADS = 4
DIFF_V_DIM = 2 * HEAD_DIM
NA_HEADS = 8
NA_ROWS = 8
NA_COLS = 16
F_GROUPS = 4
F_GROUP_W = 128
N_BRANCH = 4
BR_W = A_HEADS * HEAD_DIM
N_GROUPS = 4
EXP_PER_GROUP = 4
N_EXPERTS = N_GROUPS * EXP_PER_GROUP
D_EXPERT = 256
ROPE_BASE = 10000.0
EPS = 1e-6
NEG = -1e30
SCALE = HEAD_DIM ** -0.5

_IN_GROUPS = (
    ("a_q", 0, 512), ("a_k", 512, 128), ("a_v", 640, 128),
    ("b_q", 768, 512), ("b_k", 1280, 512), ("b_v", 1792, 512),
    ("c_q", 2304, 512), ("c_k", 2816, 512), ("c_v", 3328, 512),
    ("f", 3840, 512),
)
IN_W = 4352
_Q_GROUPS = ("a_q", "b_q", "c_q")
_ROPED = ("a_q", "a_k", "b_q", "b_k")
_CTX_F32 = ("a_k", "a_v", "b_k", "b_v", "c_k", "c_v")
_CTX_SEQ_MINOR = ("a_k", "a_v", "b_k", "c_k", "c_v")

LANES = 128
VMEM_LIMIT = 56 * 1024 * 1024

TM_PROJ = 512
TQ_SINK = 256
TQ_DIFF = 512
NA_QROWS = 4
NA_BAND = NA_ROWS + NA_QROWS - 1
TN_MERGE = 256
TN_MOD = 1024
ROUTER_W = 128
MOE_CHUNK = 128

_NT = (((1,), (1,)), ((), ()))
_TN = (((0,), (0,)), ((), ()))


def _cparams(sem):
    return pltpu.CompilerParams(dimension_semantics=sem, vmem_limit_bytes=VMEM_LIMIT)


def _sigmoid(x):
    return 1.0 / (1.0 + jnp.exp(-x))


def _norm_mod(x, g, shift, scale):
    ms = jnp.mean(x * x, axis=-1, keepdims=True)
    return (x * lax.rsqrt(ms + EPS) * g) * (1.0 + scale) + shift


def _mod_kernel(c_ref, w_ref, b_ref, o_ref):
    c = c_ref[...]
    s = (c * _sigmoid(c)).astype(BF16)
    o_ref[0] = jnp.dot(s, w_ref[0].astype(BF16), preferred_element_type=F32) + b_ref[0]


def _modulation(cond, w_mod, b_mod):
    n = cond.shape[0]
    out_w = w_mod.shape[-1]
    return pl.pallas_call(
        _mod_kernel,
        out_shape=jax.ShapeDtypeStruct((DEPTH, n, out_w), F32),
        grid=(DEPTH, out_w // TN_MOD),
        in_specs=[
            pl.BlockSpec((n, D_MODEL), lambda l, j: (0, 0)),
            pl.BlockSpec((1, D_MODEL, TN_MOD), lambda l, j: (l, 0, j)),
            pl.BlockSpec((1, 1, TN_MOD), lambda l, j: (l, 0, j)),
        ],
        out_specs=pl.BlockSpec((1, n, TN_MOD), lambda l, j: (l, 0, j)),
        compiler_params=_cparams(("parallel", "parallel")),
        name="modulation",
    )(cond, w_mod, b_mod.reshape(DEPTH, 1, out_w))


def _rope128(y, cos, sin):
    lane = lax.broadcasted_iota(jnp.int32, y.shape, 1)
    first = (lane % 32) < 16
    partner = jnp.where(first, pltpu.roll(y, LANES - 16, 1), pltpu.roll(y, 16, 1))
    return y * cos + partner * sin


def _inproj_kernel(*refs, latent, n_alias, seq):
    if latent:
        x_ref, mod_ref, g_ref, w_ref, cos_ref, sin_ref = refs[:6]
        out_refs = refs[6:]
    else:
        x_ref, mod_ref, g_ref, w_ref = refs[:4]
        out_refs = refs[4 + n_alias:]
    h = _norm_mod(x_ref[...], g_ref[...], mod_ref[0, 0:1, :], mod_ref[0, 1:2, :]).astype(BF16)
    for (name, start, width), o_ref in zip(_IN_GROUPS, out_refs):
        y = jnp.dot(h, w_ref[:, start:start + width], preferred_element_type=F32)
        if name in _Q_GROUPS:
            y = y * SCALE
        if latent and name in _ROPED:
            cos = cos_ref[...]
            sin = sin_ref[...]
            for c0 in range(0, width, LANES):
                o_ref[:, c0:c0 + LANES] = _rope128(y[:, c0:c0 + LANES], cos, sin).astype(o_ref.dtype)
        elif not latent and name in _CTX_F32:
            for bb in range(o_ref.shape[0]):
                yb = y[bb * seq:(bb + 1) * seq]
                if name in _CTX_SEQ_MINOR:
                    o_ref[bb] = yb.T
                else:
                    for hd in range(DIFF_HEADS):
                        o_ref[bb, :, hd, :] = _head(yb, hd, DIFF_V_DIM)
        else:
            o_ref[...] = y.astype(o_ref.dtype)


def _inproj(x2d, mod, g, w_in_bf, tokens_per_mod, *, rope=None, seq=None, layer=0, prev_kv=None):
    t = x2d.shape[0]
    tm = TM_PROJ
    latent = rope is not None
    in_specs = [
        pl.BlockSpec((tm, D_MODEL), lambda i: (i, 0)),
        pl.BlockSpec((1, 6, D_MODEL), lambda i: ((i * tm) // tokens_per_mod, 0, 0)),
        pl.BlockSpec((1, D_MODEL), lambda i: (0, 0)),
        pl.BlockSpec((None, D_MODEL, IN_W), lambda i: (layer, 0, 0), pipeline_mode=pl.Buffered(1)),
    ]
    args = [x2d, mod, g.reshape(1, D_MODEL), w_in_bf]
    aliases = {}
    n_alias = 0
    if latent:
        nblk = rope[0].shape[0] // tm
        in_specs += [pl.BlockSpec((tm, LANES), lambda i: (i % nblk, 0))] * 2
        args += list(rope)
    elif prev_kv is not None:
        n_alias = len(_CTX_F32)
        in_specs += [pl.BlockSpec(memory_space=pl.ANY)] * n_alias
        args += [prev_kv[name] for name in _CTX_F32]
    out_shape, out_specs = [], []
    tb = tm // seq if not latent else None
    for oi, (name, _, width) in enumerate(_IN_GROUPS):
        if not latent and name in _CTX_F32:
            blk = (width, seq) if name in _CTX_SEQ_MINOR else (seq, DIFF_HEADS, DIFF_V_DIM)
            out_shape.append(jax.ShapeDtypeStruct((t // seq, DEPTH) + blk, F32))
            out_specs.append(pl.BlockSpec((tb, None) + blk,
                                          lambda i, nd=len(blk): (i, layer) + (0,) * nd))
            if prev_kv is not None:
                aliases[4 + _CTX_F32.index(name)] = oi
        else:
            out_shape.append(jax.ShapeDtypeStruct((t, width), BF16))
            out_specs.append(pl.BlockSpec((tm, width), lambda i: (i, 0)))
    outs = pl.pallas_call(
        functools.partial(_inproj_kernel, latent=latent, n_alias=n_alias, seq=seq),
        out_shape=out_shape,
        grid=(t // tm,),
        in_specs=in_specs,
        out_specs=out_specs,
        input_output_aliases=aliases,
        compiler_params=_cparams(("parallel",)),
        name="inproj_latent" if latent else "inproj_context",
    )(*args)
    return dict(zip([n for n, _, _ in _IN_GROUPS], outs))


def _attend(q, segs, sink_col=None):
    scores = []
    for k, _, bias, k_seq_minor, _ in segs:
        if k_seq_minor:
            s = jnp.dot(q, k, preferred_element_type=F32)
        else:
            s = lax.dot_general(q, k, _NT, preferred_element_type=F32)
        if bias is not None:
            s = s + bias
        scores.append(s)
    m = scores[0].max(axis=-1, keepdims=True)
    for s in scores[1:]:
        m = jnp.maximum(m, s.max(axis=-1, keepdims=True))
    if sink_col is not None:
        m = jnp.maximum(m, sink_col)
    acc = None
    den = None
    for s, (_, v, _, _, v_seq_minor) in zip(scores, segs):
        p = jnp.exp(s - m)
        ps = p.sum(axis=-1, keepdims=True)
        if v_seq_minor:
            o = lax.dot_general(p.astype(BF16), v, _NT, preferred_element_type=F32)
        else:
            o = jnp.dot(p.astype(BF16), v, preferred_element_type=F32)
        acc = o if acc is None else acc + o
        den = ps if den is None else den + ps
    if sink_col is not None:
        den = den + jnp.exp(sink_col - m)
    return acc, den


def _head(x, h, width=HEAD_DIM):
    return x[:, h * width:(h + 1) * width]


def _head_rows(x, h, width=HEAD_DIM):
    return x[h * width:(h + 1) * width, :]


def _kv_spec(width, seq, layer):
    return pl.BlockSpec((None, None, width, seq), lambda b, i: (b, layer, 0, 0))


def _sink_attn_kernel(*refs, latent, tq, seq):
    if latent:
        sink_ref, q_ref, k_ref, v_ref, kc_ref, vc_ref, o_ref = refs
    else:
        sink_ref, q_ref, kc_ref, vc_ref, o_ref = refs
    q_all = q_ref[...]
    k_ctx = kc_ref[...].astype(BF16)
    v_ctx = vc_ref[...].astype(BF16)
    if latent:
        band = tq + 2 * WIN
        qb = pl.program_id(1)
        start = pl.multiple_of(jnp.clip(qb * tq - WIN, 0, seq - band), WIN)
        k_loc = k_ref[pl.ds(start, band), :]
        v_loc = v_ref[pl.ds(start, band), :]
        rows = A_GROUP * tq
        qpos = qb * tq + lax.broadcasted_iota(jnp.int32, (rows, band), 0) % tq
        kpos = start + lax.broadcasted_iota(jnp.int32, (rows, band), 1)
        bias = jnp.where(jnp.abs(qpos - kpos) <= WIN, 0.0, NEG).astype(F32)
    outs = [None] * A_HEADS
    for kv in range(A_KV_HEADS):
        heads = [kv * A_GROUP + g for g in range(A_GROUP)]
        q = jnp.concatenate([_head(q_all, h) for h in heads], axis=0)
        sink_col = jnp.concatenate(
            [jnp.full((tq, 1), sink_ref[0, h], F32) for h in heads], axis=0)
        segs = [(_head_rows(k_ctx, kv), _head_rows(v_ctx, kv), None, True, True)]
        if latent:
            segs.append((_head(k_loc, kv), _head(v_loc, kv), bias, False, False))
        acc, den = _attend(q, segs, sink_col)
        o = acc / den
        for g, h in enumerate(heads):
            outs[h] = o[g * tq:(g + 1) * tq]
    o_ref[...] = jnp.concatenate(outs, axis=1).astype(o_ref.dtype)


def _sink_attn(sink, q, kv_ctx, *, nb, seq, layer, kv_loc=None):
    latent = kv_loc is not None
    tq = TQ_SINK if latent else seq
    nq = seq // tq
    kw = A_KV_HEADS * HEAD_DIM
    s_ctx = kv_ctx[0].shape[-1]
    in_specs = [
        pl.BlockSpec(memory_space=pltpu.SMEM),
        pl.BlockSpec((tq, BR_W), lambda b, i: (b * nq + i, 0)),
    ]
    args = [sink.reshape(1, A_HEADS), q]
    if latent:
        in_specs += [pl.BlockSpec((seq, kw), lambda b, i: (b, 0))] * 2
        args += list(kv_loc)
    in_specs += [_kv_spec(kw, s_ctx, layer)] * 2
    args += list(kv_ctx)
    return pl.pallas_call(
        functools.partial(_sink_attn_kernel, latent=latent, tq=tq, seq=seq),
        out_shape=jax.ShapeDtypeStruct(q.shape, BF16),
        grid=(nb, nq),
        in_specs=in_specs,
        out_specs=pl.BlockSpec((tq, BR_W), lambda b, i: (b * nq + i, 0)),
        compiler_params=_cparams(("parallel", "arbitrary")),
        name="sink_attn_latent" if latent else "sink_attn_context",
    )(*args)


def _diff_attn_kernel(*refs, latent, lam_init):
    if latent:
        lam_ref, g_ref, q_ref, k_ref, v_ref, kc_ref, vc_ref, o_ref = refs
    else:
        lam_ref, g_ref, q_ref, kc_ref, vc_ref, o_ref = refs
    lam = lam_ref[...]
    lam_a = jnp.sum(lam[0:1] * lam[1:2], axis=-1, keepdims=True)
    lam_b = jnp.sum(lam[2:3] * lam[3:4], axis=-1, keepdims=True)
    lam_full = jnp.exp(lam_a) - jnp.exp(lam_b) + lam_init
    q_all = q_ref[...]
    k_ctx = kc_ref[...].astype(BF16)
    if latent:
        k_loc = k_ref[...]
        v_loc = v_ref[...]
    outs = []
    for h in range(DIFF_HEADS):
        v_ctx = vc_ref[:, h, :].astype(BF16)
        res = []
        for sub in range(2):
            hh = 2 * h + sub
            segs = [(_head_rows(k_ctx, hh), v_ctx, None, True, False)]
            if latent:
                segs.append((_head(k_loc, hh), _head(v_loc, h, DIFF_V_DIM), None, False, False))
            acc, den = _attend(_head(q_all, hh), segs)
            res.append(acc / den)
        o = res[0] - lam_full * res[1]
        ms = jnp.mean(o * o, axis=-1, keepdims=True)
        outs.append(o * lax.rsqrt(ms + EPS) * g_ref[...] * (1.0 - lam_init))
    o_ref[...] = jnp.concatenate(outs, axis=1).astype(o_ref.dtype)


def _diff_attn(lam, sub_g, q, kv_ctx, *, nb, seq, lam_init, layer, kv_loc=None):
    latent = kv_loc is not None
    tq = TQ_DIFF if latent else seq
    nq = seq // tq
    kw = DIFF_HEADS * 2 * HEAD_DIM
    s_ctx = kv_ctx[0].shape[-1]
    in_specs = [
        pl.BlockSpec((4, HEAD_DIM), lambda b, i: (0, 0)),
        pl.BlockSpec((1, DIFF_V_DIM), lambda b, i: (0, 0)),
        pl.BlockSpec((tq, BR_W), lambda b, i: (b * nq + i, 0)),
    ]
    args = [lam, sub_g.reshape(1, DIFF_V_DIM), q]
    if latent:
        in_specs += [pl.BlockSpec((seq, kw), lambda b, i: (b, 0))] * 2
        args += list(kv_loc)
    in_specs += [_kv_spec(kw, s_ctx, layer),
                 pl.BlockSpec((None, None, s_ctx, DIFF_HEADS, DIFF_V_DIM),
                              lambda b, i: (b, layer, 0, 0, 0))]
    args += list(kv_ctx)
    return pl.pallas_call(
        functools.partial(_diff_attn_kernel, latent=latent, lam_init=lam_init),
        out_shape=jax.ShapeDtypeStruct(q.shape, BF16),
        grid=(nb, nq),
        in_specs=in_specs,
        out_specs=pl.BlockSpec((tq, BR_W), lambda b, i: (b * nq + i, 0)),
        compiler_params=_cparams(("parallel", "arbitrary")),
        name="diff_attn_latent" if latent else "diff_attn_context",
    )(*args)


def _dense_attn_kernel(q_ref, k_ref, v_ref, o_ref):
    q_all = q_ref[...]
    k = k_ref[...].astype(BF16)
    v = v_ref[...].astype(BF16)
    outs = []
    for h in range(NA_HEADS):
        acc, den = _attend(_head(q_all, h), [(_head_rows(k, h), _head_rows(v, h), None, True, True)])
        outs.append(acc / den)
    o_ref[...] = jnp.concatenate(outs, axis=1).astype(o_ref.dtype)


def _dense_attn(q, kv_ctx, *, nb, seq, layer):
    return pl.pallas_call(
        _dense_attn_kernel,
        out_shape=jax.ShapeDtypeStruct(q.shape, BF16),
        grid=(nb, 1),
        in_specs=[pl.BlockSpec((seq, BR_W), lambda b, i: (b, 0)),
                  _kv_spec(BR_W, seq, layer), _kv_spec(BR_W, seq, layer)],
        out_specs=pl.BlockSpec((seq, BR_W), lambda b, i: (b, 0)),
        compiler_params=_cparams(("parallel", "arbitrary")),
        name="dense_attn_context",
    )(q, *kv_ctx)


def _na_window_start(r, rows):
    return jnp.clip(r - NA_ROWS // 2, 0, rows - NA_ROWS)


def _na_band_start(step, rows):
    return jnp.clip(step * NA_QROWS - NA_ROWS // 2, 0, rows - NA_BAND)


def _na_attn_kernel(q_ref, k_ref, v_ref, kc_ref, vc_ref, tab_ref, o_ref, bias_ref, *, rows):
    step = pl.program_id(0)
    band_row = _na_band_start(step, rows)

    @pl.when(pl.program_id(1) == 0)
    def _():
        for ri in range(NA_QROWS):
            r = step * NA_QROWS + ri
            rs = _na_window_start(r, rows)
            for h in range(NA_HEADS):
                blocks = []
                for j in range(NA_BAND):
                    kr = band_row + j
                    ok = (kr >= rs) & (kr < rs + NA_ROWS)
                    dr = jnp.clip(kr - r + NA_ROWS - 1, 0, 2 * NA_ROWS - 2)
                    blocks.append(jnp.where(ok, tab_ref[h, dr], NEG))
                bias_ref[h, ri * GRID_W:(ri + 1) * GRID_W, :] = jnp.concatenate(blocks, axis=1)

    start = pl.multiple_of(band_row * GRID_W, GRID_W)
    band = NA_BAND * GRID_W
    q_all = q_ref[...]
    k_loc = k_ref[pl.ds(start, band), :]
    v_loc = v_ref[pl.ds(start, band), :]
    k_ctx = kc_ref[...].astype(BF16)
    v_ctx = vc_ref[...].astype(BF16)
    outs = []
    for h in range(NA_HEADS):
        segs = [(_head_rows(k_ctx, h), _head_rows(v_ctx, h), None, True, True),
                (_head(k_loc, h), _head(v_loc, h), bias_ref[h], False, False)]
        acc, den = _attend(_head(q_all, h), segs)
        outs.append(acc / den)
    o_ref[...] = jnp.concatenate(outs, axis=1).astype(o_ref.dtype)


def _na_attn(q, kv_loc, kv_ctx, rpb_tab, *, nb, seq, layer):
    rows = seq // GRID_W
    nsteps = rows // NA_QROWS
    tq = NA_QROWS * GRID_W
    band = NA_BAND * GRID_W
    past = kv_ctx[0].shape[-1]
    for s in range(nsteps):
        us = min(max(s * NA_QROWS - NA_ROWS // 2, 0), rows - NA_BAND)
        for r in range(s * NA_QROWS, (s + 1) * NA_QROWS):
            rs = min(max(r - NA_ROWS // 2, 0), rows - NA_ROWS)
            assert us <= rs and rs + NA_ROWS <= us + NA_BAND
    ctx_spec = pl.BlockSpec((None, None, BR_W, past), lambda s, b: (b, layer, 0, 0))
    return pl.pallas_call(
        functools.partial(_na_attn_kernel, rows=rows),
        out_shape=jax.ShapeDtypeStruct(q.shape, BF16),
        grid=(nsteps, nb),
        in_specs=[
            pl.BlockSpec((tq, BR_W), lambda s, b: (b * nsteps + s, 0)),
            pl.BlockSpec((seq, BR_W), lambda s, b: (b, 0)),
            pl.BlockSpec((seq, BR_W), lambda s, b: (b, 0)),
            ctx_spec, ctx_spec,
            pl.BlockSpec(rpb_tab.shape, lambda s, b: (0, 0, 0, 0)),
        ],
        out_specs=pl.BlockSpec((tq, BR_W), lambda s, b: (b * nsteps + s, 0)),
        scratch_shapes=[pltpu.VMEM((NA_HEADS, tq, band), F32)],
        compiler_params=_cparams(("arbitrary", "arbitrary")),
        name="na_attn_latent",
    )(q, *kv_loc, *kv_ctx, rpb_tab)


def _rpb_expand_kernel(rpb_ref, onehot_ref, o_ref):
    o_ref[...] = jnp.dot(rpb_ref[...], onehot_ref[...], preferred_element_type=F32,
                         precision=lax.Precision.HIGHEST)


def _na_rpb_table(rpb):
    n_dr, n_dc = 2 * NA_ROWS - 1, 2 * NA_COLS - 1
    qcol = np.arange(GRID_W)
    kcol = np.arange(GRID_W)
    cs = np.clip(qcol - NA_COLS // 2, 0, GRID_W - NA_COLS)
    col_ok = (kcol[None, :] >= cs[:, None]) & (kcol[None, :] < cs[:, None] + NA_COLS)
    dc = np.clip(kcol[None, :] - qcol[:, None] + NA_COLS - 1, 0, n_dc - 1)
    onehot = np.zeros((32, GRID_W * GRID_W), np.float32)
    onehot[dc.reshape(-1), np.arange(GRID_W * GRID_W)] = 1.0
    rpb2 = jnp.pad(rpb.reshape(NA_HEADS * n_dr, n_dc), ((0, 0), (0, 32 - n_dc)))
    full = pl.pallas_call(
        _rpb_expand_kernel,
        out_shape=jax.ShapeDtypeStruct((NA_HEADS * n_dr, GRID_W * GRID_W), F32),
        name="rpb_expand",
    )(rpb2, jnp.asarray(onehot))
    full = full.reshape(NA_HEADS, n_dr, GRID_W, GRID_W)
    return full + jnp.asarray(np.where(col_ok, 0.0, NEG).astype(np.float32))


def _dft_tables(seq):
    pos = np.arange(seq)
    ang = 2.0 * np.pi * ((pos[:, None] * pos[None, :]) % seq) / seq
    left = np.concatenate([np.cos(ang), -np.sin(ang)], axis=1).astype(np.float32)
    ch = np.arange(F_GROUP_W)
    angw = 2.0 * np.pi * ((ch[:, None] * ch[None, :]) % F_GROUP_W) / F_GROUP_W
    right = np.concatenate([np.cos(angw), np.sin(angw)], axis=1).astype(np.float32)
    return jnp.asarray(left).astype(BF16), jnp.asarray(right).astype(BF16)


def _fourier_kernel(f_ref, left_ref, right_ref, o_ref, stack_ref, *, seq):
    f = f_ref[...]
    right = right_ref[...]
    for g in range(F_GROUPS):
        t = jnp.dot(_head(f, g, F_GROUP_W), right, preferred_element_type=F32).astype(BF16)
        stack_ref[0:seq, g * F_GROUP_W:(g + 1) * F_GROUP_W] = t[:, :F_GROUP_W]
        stack_ref[seq:2 * seq, g * F_GROUP_W:(g + 1) * F_GROUP_W] = t[:, F_GROUP_W:]
    y = jnp.dot(left_ref[...], stack_ref[...], preferred_element_type=F32)
    o_ref[...] = (y * (1.0 / math.sqrt(seq * F_GROUP_W))).astype(o_ref.dtype)


def _fourier(f, *, nb, seq):
    left, right = _dft_tables(seq)
    w = F_GROUPS * F_GROUP_W
    return pl.pallas_call(
        functools.partial(_fourier_kernel, seq=seq),
        out_shape=jax.ShapeDtypeStruct(f.shape, BF16),
        grid=(nb,),
        in_specs=[
            pl.BlockSpec((seq, w), lambda b: (b, 0)),
            pl.BlockSpec((seq, 2 * seq), lambda b: (0, 0)),
            pl.BlockSpec((F_GROUP_W, 2 * F_GROUP_W), lambda b: (0, 0)),
        ],
        out_specs=pl.BlockSpec((seq, w), lambda b: (b, 0)),
        scratch_shapes=[pltpu.VMEM((2 * seq, w), BF16)],
        compiler_params=_cparams(("parallel",)),
        name=f"fourier_{seq}",
    )(f, left, right)


def _merge_kernel(x_ref, mod_ref, g_ref, *refs):
    br_refs = refs[0:N_BRANCH]
    wbr_ref = refs[N_BRANCH]
    wg_refs = refs[N_BRANCH + 1:2 * N_BRANCH + 1]
    bg_refs = refs[2 * N_BRANCH + 1:3 * N_BRANCH + 1]
    wo_ref, o_ref, h_ref, acc_ref = refs[3 * N_BRANCH + 1:]
    j = pl.program_id(1)

    @pl.when(j == 0)
    def _():
        h_ref[...] = _norm_mod(x_ref[...], g_ref[...], mod_ref[0, 0:1, :],
                               mod_ref[0, 1:2, :]).astype(BF16)
        acc_ref[...] = jnp.zeros_like(acc_ref)

    h = h_ref[...]
    mix = None
    for n in range(N_BRANCH):
        gate = _sigmoid(jnp.dot(h, wg_refs[n][...], preferred_element_type=F32) + bg_refs[n][0])
        proj = jnp.dot(br_refs[n][...], wbr_ref[n], preferred_element_type=F32)
        term = gate * proj
        mix = term if mix is None else mix + term
    acc_ref[...] += jnp.dot(mix.astype(BF16), wo_ref[...], preferred_element_type=F32)

    @pl.when(j == pl.num_programs(1) - 1)
    def _():
        o_ref[...] = x_ref[...] + mod_ref[0, 2:3, :] * acc_ref[...]


def _merge(x2d, mod, g, branches, w_branch_bf, w_gate_bf, b_gate, w_out_bf, tokens_per_mod, layer):
    t = x2d.shape[0]
    tm, tn = TM_PROJ, TN_MERGE
    nch = D_MODEL // tn
    br_spec = pl.BlockSpec((tm, BR_W), lambda i, j: (i, 0))
    wg_specs = [pl.BlockSpec((None, D_MODEL, tn),
                             functools.partial(lambda i, j, n: (layer, 0, n * nch + j), n=n))
                for n in range(N_BRANCH)]
    bg_specs = [pl.BlockSpec((1, 1, tn), functools.partial(lambda i, j, n: (n * nch + j, 0, 0), n=n))
                for n in range(N_BRANCH)]
    bg = b_gate.reshape(N_BRANCH * nch, 1, tn)
    return pl.pallas_call(
        _merge_kernel,
        out_shape=jax.ShapeDtypeStruct((t, D_MODEL), F32),
        grid=(t // tm, nch),
        in_specs=[
            pl.BlockSpec((tm, D_MODEL), lambda i, j: (i, 0)),
            pl.BlockSpec((1, 6, D_MODEL), lambda i, j: ((i * tm) // tokens_per_mod, 0, 0)),
            pl.BlockSpec((1, D_MODEL), lambda i, j: (0, 0)),
            br_spec, br_spec, br_spec, br_spec,
            pl.BlockSpec((None, N_BRANCH, BR_W, tn), lambda i, j: (layer, 0, 0, j)),
            *wg_specs, *bg_specs,
            pl.BlockSpec((None, tn, D_MODEL), lambda i, j: (layer, j, 0)),
        ],
        out_specs=pl.BlockSpec((tm, D_MODEL), lambda i, j: (i, 0)),
        scratch_shapes=[pltpu.VMEM((tm, D_MODEL), BF16), pltpu.VMEM((tm, D_MODEL), F32)],
        compiler_params=_cparams(("parallel", "arbitrary")),
        name="merge",
    )(x2d, mod, g.reshape(1, D_MODEL), *branches, w_branch_bf,
      *([w_gate_bf] * N_BRANCH), *([bg] * N_BRANCH), w_out_bf)


def _route(logits):
    lane = lax.broadcasted_iota(jnp.int32, logits.shape, 1)
    ninf = -jnp.inf
    lg = jnp.where(lane < N_GROUPS, logits, ninf)
    mg = lg.max(axis=-1, keepdims=True)
    pg_top = 1.0 / jnp.exp(lg - mg).sum(axis=-1, keepdims=True)
    g_idx = jnp.where(lg == mg, lane, ROUTER_W).min(axis=-1, keepdims=True)
    lo = N_GROUPS + EXP_PER_GROUP * g_idx
    le = jnp.where((lane >= lo) & (lane < lo + EXP_PER_GROUP), logits, ninf)
    m1 = le.max(axis=-1, keepdims=True)
    e1 = jnp.where(le == m1, lane, ROUTER_W).min(axis=-1, keepdims=True)
    le2 = jnp.where(lane == e1, ninf, le)
    m2 = le2.max(axis=-1, keepdims=True)
    e2 = jnp.where(le2 == m2, lane, ROUTER_W).min(axis=-1, keepdims=True)
    se = jnp.exp(le - m1).sum(axis=-1, keepdims=True)
    pe1 = 1.0 / se
    pe2 = jnp.exp(m2 - m1) / se
    tot = pe1 + pe2
    w1 = pe1 / tot * pg_top
    w2 = pe2 / tot * pg_top
    return jnp.where(lane == e1, w1, 0.0) + jnp.where(lane == e2, w2, 0.0), g_idx


def _moe_kernel(x_ref, mod_ref, g_ref, wr_ref, br_ref, wg_ref, wu_ref, wd_ref, fg_ref,
                o_ref, h_ref, comb_ref, grp_ref, rank_ref, cnt_ref, *, final):
    grp = pl.program_id(1)
    tm = x_ref.shape[0]

    @pl.when(grp == 0)
    def _():
        h = _norm_mod(x_ref[...], g_ref[...], mod_ref[0, 3:4, :], mod_ref[0, 4:5, :])
        h_hi = h.astype(BF16)
        h_ref[...] = h_hi
        h_lo = (h - h_hi.astype(F32)).astype(BF16)
        w = wr_ref[...]
        w_hi = w.astype(BF16)
        w_lo = (w - w_hi.astype(F32)).astype(BF16)
        logits = (jnp.dot(h_hi, w_hi, preferred_element_type=F32)
                  + jnp.dot(h_lo, w_hi, preferred_element_type=F32)
                  + jnp.dot(h_hi, w_lo, preferred_element_type=F32)) + br_ref[...]
        comb, g_idx = _route(logits)
        comb_ref[...] = comb
        lane = lax.broadcasted_iota(jnp.int32, (tm, ROUTER_W), 1)
        member = lane == g_idx
        onehot = jnp.where(member, 1.0, 0.0).astype(BF16)
        earlier = jnp.where(lax.broadcasted_iota(jnp.int32, (tm, tm), 0)
                            > lax.broadcasted_iota(jnp.int32, (tm, tm), 1), 1.0, 0.0).astype(BF16)
        before = jnp.dot(earlier, onehot, preferred_element_type=F32)
        rank_ref[...] = jnp.where(member, before, 0.0).sum(axis=-1, keepdims=True).astype(jnp.int32)
        grp_ref[...] = g_idx
        cnt_ref[...] = jnp.where(member, 1.0, 0.0).sum(axis=0, keepdims=True)
        o_ref[...] = jnp.zeros_like(o_ref)

    lane_r = lax.broadcasted_iota(jnp.int32, (1, ROUTER_W), 1)
    count = jnp.where(lane_r == grp, cnt_ref[...], 0.0).sum().astype(jnp.int32)
    in_group = grp_ref[...] == grp
    rank = rank_ref[...]
    comb = comb_ref[...]
    comb_hi = comb.astype(BF16)
    comb_lo = (comb - comb_hi.astype(F32)).astype(BF16)
    slot_lane = lax.broadcasted_iota(jnp.int32, (tm, MOE_CHUNK), 1)
    lane_c = lax.broadcasted_iota(jnp.int32, (MOE_CHUNK, ROUTER_W), 1)

    def chunk(c, carry):
        sel = jnp.where(in_group & (rank - c * MOE_CHUNK == slot_lane), 1.0, 0.0).astype(BF16)
        xc = lax.dot_general(sel, h_ref[...], _TN, preferred_element_type=F32).astype(BF16)
        cwc = (lax.dot_general(sel, comb_hi, _TN, preferred_element_type=F32)
               + lax.dot_general(sel, comb_lo, _TN, preferred_element_type=F32))
        out = None
        for e in range(EXP_PER_GROUP):
            cw = jnp.where(lane_c == N_GROUPS + EXP_PER_GROUP * grp + e, cwc, 0.0).sum(
                axis=-1, keepdims=True)
            gate = jnp.dot(xc, wg_ref[e], preferred_element_type=F32)
            up = jnp.dot(xc, wu_ref[e], preferred_element_type=F32)
            hid = (gate * _sigmoid(gate)) * up * cw
            term = jnp.dot(hid.astype(BF16), wd_ref[e], preferred_element_type=F32)
            out = term if out is None else out + term
        o_ref[...] += jnp.dot(sel, out.astype(BF16), preferred_element_type=F32)
        return carry

    lax.fori_loop(0, (count + MOE_CHUNK - 1) // MOE_CHUNK, chunk, 0)

    @pl.when(grp == pl.num_programs(1) - 1)
    def _():
        y = x_ref[...] + mod_ref[0, 5:6, :] * o_ref[...]
        if final:
            ms = jnp.mean(y * y, axis=-1, keepdims=True)
            y = y * lax.rsqrt(ms + EPS) * fg_ref[...]
        o_ref[...] = y


def _moe(x2d, mod, g, w_router, b_router, w_g_bf, w_u_bf, w_d_bf, final_g, tokens_per_mod, final,
         layer):
    t = x2d.shape[0]
    tm = TM_PROJ
    w_spec_in = pl.BlockSpec((None, None, EXP_PER_GROUP, D_MODEL, D_EXPERT),
                             lambda i, grp: (layer, grp, 0, 0, 0))
    return pl.pallas_call(
        functools.partial(_moe_kernel, final=final),
        out_shape=jax.ShapeDtypeStruct((t, D_MODEL), F32),
        grid=(t // tm, N_GROUPS),
        in_specs=[
            pl.BlockSpec((tm, D_MODEL), lambda i, grp: (i, 0)),
            pl.BlockSpec((1, 6, D_MODEL), lambda i, grp: ((i * tm) // tokens_per_mod, 0, 0)),
            pl.BlockSpec((1, D_MODEL), lambda i, grp: (0, 0)),
            pl.BlockSpec((D_MODEL, ROUTER_W), lambda i, grp: (0, 0)),
            pl.BlockSpec((1, ROUTER_W), lambda i, grp: (0, 0)),
            w_spec_in, w_spec_in,
            pl.BlockSpec((None, None, EXP_PER_GROUP, D_EXPERT, D_MODEL),
                         lambda i, grp: (layer, grp, 0, 0, 0)),
            pl.BlockSpec((1, D_MODEL), lambda i, grp: (0, 0)),
        ],
        out_specs=pl.BlockSpec((tm, D_MODEL), lambda i, grp: (i, 0)),
        scratch_shapes=[pltpu.VMEM((tm, D_MODEL), BF16), pltpu.VMEM((tm, ROUTER_W), F32),
                        pltpu.VMEM((tm, 1), jnp.int32), pltpu.VMEM((tm, 1), jnp.int32),
                        pltpu.VMEM((1, ROUTER_W), F32)],
        compiler_params=_cparams(("parallel", "arbitrary")),
        name="moe_final" if final else "moe",
    )(x2d, mod, g.reshape(1, D_MODEL), w_router, b_router, w_g_bf, w_u_bf, w_d_bf,
      final_g.reshape(1, D_MODEL))


def _rope_tables(seq):
    nf = HEAD_DIM // 4
    t = jnp.arange(seq)
    inv = ROPE_BASE ** (-jnp.arange(nf, dtype=F32) / nf)
    pos = jnp.stack([t // GRID_W, t % GRID_W], -1).astype(F32)
    ang = pos[:, :, None] * inv
    cos, sin = jnp.cos(ang), jnp.sin(ang)
    cos64 = jnp.concatenate([cos[:, 0], cos[:, 0], cos[:, 1], cos[:, 1]], axis=-1)
    sin64 = jnp.concatenate([-sin[:, 0], sin[:, 0], -sin[:, 1], sin[:, 1]], axis=-1)
    return jnp.tile(cos64, (1, 2)), jnp.tile(sin64, (1, 2))


def _seq_minor_view(cache):
    nb, depth, seq = cache.shape[:3]
    nd = cache.ndim
    return jnp.transpose(cache, (0, 1, *range(3, nd), 2)).reshape(nb, depth, -1, seq)


def _seq_major_view(arr, head_dims):
    nb, depth, _, seq = arr.shape
    nd = 3 + len(head_dims)
    arr = arr.reshape(nb, depth, *head_dims, seq)
    return jnp.transpose(arr, (0, 1, nd - 1, *range(2, nd - 1)))


def kernel(x_prompt, x_sample, cache_attn_k, cache_attn_v, cache_diff_k, cache_diff_v, cache_na_k, cache_na_v, c, c_ctx, w_mod, b_mod, norm1_g, w_in, attn_sink, diff_lambda, diff_sub_g, na_rpb, w_branch, w_gate, b_gate, w_out, norm2_g, w_router_group, b_router_group, w_router_expert, b_router_expert, w_exp_gate, w_exp_up, w_exp_down, final_g):
    nbp, seq_p, _ = x_prompt.shape
    nbs, seq_s, _ = x_sample.shape
    xp = x_prompt.reshape(nbp * seq_p, D_MODEL)
    xs = x_sample.reshape(nbs * seq_s, D_MODEL)

    cond = jnp.concatenate([c_ctx[None, :], c], axis=0)
    n_cond = -(-cond.shape[0] // 8) * 8
    cond = jnp.pad(cond, ((0, n_cond - cond.shape[0]), (0, 0)))
    mods = _modulation(cond, w_mod, b_mod).reshape(DEPTH, n_cond, 6, D_MODEL)

    rope = _rope_tables(seq_s)
    cache_a = (_seq_minor_view(cache_attn_k), _seq_minor_view(cache_attn_v))
    cache_b = (_seq_minor_view(cache_diff_k), cache_diff_v)
    cache_c = (_seq_minor_view(cache_na_k), _seq_minor_view(cache_na_v))

    w_in_bf = w_in.astype(BF16)
    w_branch_bf = w_branch.astype(BF16)
    w_gate_bf = w_gate.astype(BF16)
    w_out_bf = w_out.astype(BF16)
    grouped = (DEPTH, N_GROUPS, EXP_PER_GROUP)
    w_eg_bf = w_exp_gate.astype(BF16).reshape(*grouped, D_MODEL, D_EXPERT)
    w_eu_bf = w_exp_up.astype(BF16).reshape(*grouped, D_MODEL, D_EXPERT)
    w_ed_bf = w_exp_down.astype(BF16).reshape(*grouped, D_EXPERT, D_MODEL)
    pad = ROUTER_W - N_GROUPS - N_EXPERTS
    w_router = jnp.concatenate(
        [w_router_group, w_router_expert, jnp.zeros((DEPTH, D_MODEL, pad), F32)], axis=-1)
    b_router = jnp.concatenate(
        [b_router_group, b_router_expert, jnp.zeros((DEPTH, pad), F32)], axis=-1)

    kv = None
    for l in range(DEPTH):
        lam_init = 0.8 - 0.6 * math.exp(-0.3 * l)
        mod_p = mods[l, 0:1]
        mod_s = mods[l, 1:1 + nbs]
        final = l == DEPTH - 1
        moe_w = (w_router[l], b_router[l].reshape(1, ROUTER_W), w_eg_bf, w_eu_bf, w_ed_bf)
        merge_w = (w_branch_bf, w_gate_bf, b_gate[l], w_out_bf)

        pr = _inproj(xp, mod_p, norm1_g[l], w_in_bf, xp.shape[0], seq=seq_p, layer=l, prev_kv=kv)
        kv = {name: pr[name] for name in _CTX_F32}
        o_a = _sink_attn(attn_sink[l], pr["a_q"], (kv["a_k"], kv["a_v"]), nb=nbp, seq=seq_p, layer=l)
        o_b = _diff_attn(diff_lambda[l], diff_sub_g[l], pr["b_q"], (kv["b_k"], kv["b_v"]),
                         nb=nbp, seq=seq_p, lam_init=lam_init, layer=l)
        o_c = _dense_attn(pr["c_q"], (kv["c_k"], kv["c_v"]), nb=nbp, seq=seq_p, layer=l)
        o_f = _fourier(pr["f"], nb=nbp, seq=seq_p)
        xp = _merge(xp, mod_p, norm1_g[l], (o_a, o_b, o_c, o_f), *merge_w, xp.shape[0], l)
        xp = _moe(xp, mod_p, norm2_g[l], *moe_w, final_g, xp.shape[0], final, l)

        pr = _inproj(xs, mod_s, norm1_g[l], w_in_bf, seq_s, rope=rope, layer=l)
        o_a = _sink_attn(attn_sink[l], pr["a_q"], cache_a, nb=nbs, seq=seq_s, layer=l,
                         kv_loc=(pr["a_k"], pr["a_v"]))
        o_b = _diff_attn(diff_lambda[l], diff_sub_g[l], pr["b_q"], cache_b, nb=nbs, seq=seq_s,
                         lam_init=lam_init, layer=l, kv_loc=(pr["b_k"], pr["b_v"]))
        o_c = _na_attn(pr["c_q"], (pr["c_k"], pr["c_v"]), cache_c,
                       _na_rpb_table(na_rpb[l]), nb=nbs, seq=seq_s, layer=l)
        o_f = _fourier(pr["f"], nb=nbs, seq=seq_s)
        xs = _merge(xs, mod_s, norm1_g[l], (o_a, o_b, o_c, o_f), *merge_w, seq_s, l)
        xs = _moe(xs, mod_s, norm2_g[l], *moe_w, final_g, seq_s, final, l)

    return (
        xp.reshape(x_prompt.shape),
        xs.reshape(x_sample.shape),
        _seq_major_view(kv["a_k"], (A_KV_HEADS, HEAD_DIM)),
        _seq_major_view(kv["a_v"], (A_KV_HEADS, HEAD_DIM)),
        _seq_major_view(kv["b_k"], (DIFF_HEADS, 2, HEAD_DIM)),
        kv["b_v"],
        _seq_major_view(kv["c_k"], (NA_HEADS, HEAD_DIM)),
        _seq_major_view(kv["c_v"], (NA_HEADS, HEAD_DIM)),
    )
```

```python
import functools
import math

import jax
import jax.numpy as jnp
import numpy as np
from jax import lax
from jax.experimental import pallas as pl
from jax.experimental.pallas import tpu as pltpu

F32 = jnp.float32
BF16 = jnp.bfloat16

D_MODEL = 2048
DEPTH = 2
GRID_W = 64
HEAD_DIM = 64
A_HEADS = 8
A_KV_HEADS = 2
A_GROUP = A_HEADS // A_KV_HEADS
WIN = 128
DIFF_HEADS = 4
DIFF_V_DIM = 2 * HEAD_DIM
NA_HEADS = 8
NA_ROWS = 8
NA_COLS = 16
F_GROUPS = 4
F_GROUP_W = 128
N_BRANCH = 4
BR_W = A_HEADS * HEAD_DIM
N_GROUPS = 4
EXP_PER_GROUP = 4
N_EXPERTS = N_GROUPS * EXP_PER_GROUP
D_EXPERT = 256
ROPE_BASE = 10000.0
EPS = 1e-6
NEG = -1e30
SCALE = HEAD_DIM ** -0.5

_IN_GROUPS = (
    ("a_q", 0, 512), ("a_k", 512, 128), ("a_v", 640, 128),
    ("b_q", 768, 512), ("b_k", 1280, 512), ("b_v", 1792, 512),
    ("c_q", 2304, 512), ("c_k", 2816, 512), ("c_v", 3328, 512),
    ("f", 3840, 512),
)
IN_W = 4352
_Q_GROUPS = ("a_q", "b_q", "c_q")
_ROPED = ("a_q", "a_k", "b_q", "b_k")
_CTX_F32 = ("a_k", "a_v", "b_k", "b_v", "c_k", "c_v")
_CTX_SEQ_MINOR = ("a_k", "a_v", "b_k", "c_k", "c_v")

LANES = 128
VMEM_LIMIT = 56 * 1024 * 1024

TM_PROJ = 512
TQ_SINK = 256
TQ_DIFF = 512
NA_QROWS = 4
NA_BAND = NA_ROWS + NA_QROWS - 1
TN_MERGE = 256
TN_MOD = 1024
ROUTER_W = 128
MOE_CHUNK = 128

_NT = (((1,), (1,)), ((), ()))
_TN = (((0,), (0,)), ((), ()))


def _cparams(sem):
    return pltpu.CompilerParams(dimension_semantics=sem, vmem_limit_bytes=VMEM_LIMIT)


def _sigmoid(x):
    return 1.0 / (1.0 + jnp.exp(-x))


def _norm_mod(x, g, shift, scale):
    ms = jnp.mean(x * x, axis=-1, keepdims=True)
    return (x * lax.rsqrt(ms + EPS) * g) * (1.0 + scale) + shift


def _mod_kernel(c_ref, w_ref, b_ref, o_ref):
    c = c_ref[...]
    s = (c * _sigmoid(c)).astype(BF16)
    o_ref[0] = jnp.dot(s, w_ref[0].astype(BF16), preferred_element_type=F32) + b_ref[0]


def _modulation(cond, w_mod, b_mod):
    n = cond.shape[0]
    out_w = w_mod.shape[-1]
    return pl.pallas_call(
        _mod_kernel,
        out_shape=jax.ShapeDtypeStruct((DEPTH, n, out_w), F32),
        grid=(DEPTH, out_w // TN_MOD),
        in_specs=[
            pl.BlockSpec((n, D_MODEL), lambda l, j: (0, 0)),
            pl.BlockSpec((1, D_MODEL, TN_MOD), lambda l, j: (l, 0, j)),
            pl.BlockSpec((1, 1, TN_MOD), lambda l, j: (l, 0, j)),
        ],
        out_specs=pl.BlockSpec((1, n, TN_MOD), lambda l, j: (l, 0, j)),
        compiler_params=_cparams(("parallel", "parallel")),
        name="modulation",
    )(cond, w_mod, b_mod.reshape(DEPTH, 1, out_w))


def _rope128(y, cos, sin):
    lane = lax.broadcasted_iota(jnp.int32, y.shape, 1)
    first = (lane % 32) < 16
    partner = jnp.where(first, pltpu.roll(y, LANES - 16, 1), pltpu.roll(y, 16, 1))
    return y * cos + partner * sin


def _inproj_kernel(*refs, latent, n_alias, seq, layer, fill_depth):
    if latent:
        x_ref, mod_ref, g_ref, w_ref, cos_ref, sin_ref = refs[:6]
        out_refs = refs[6:]
    else:
        x_ref, mod_ref, g_ref, w_ref = refs[:4]
        out_refs = refs[4 + n_alias:]
    h = _norm_mod(x_ref[...], g_ref[...], mod_ref[0, 0:1, :], mod_ref[0, 1:2, :]).astype(BF16)
    for (name, start, width), o_ref in zip(_IN_GROUPS, out_refs):
        y = jnp.dot(h, w_ref[:, start:start + width], preferred_element_type=F32)
        if name in _Q_GROUPS:
            y = y * SCALE
        if latent and name in _ROPED:
            cos = cos_ref[...]
            sin = sin_ref[...]
            for c0 in range(0, width, LANES):
                o_ref[:, c0:c0 + LANES] = _rope128(y[:, c0:c0 + LANES], cos, sin).astype(o_ref.dtype)
        elif not latent and name in _CTX_F32:
            for bb in range(o_ref.shape[0]):
                yb = y[bb * seq:(bb + 1) * seq]
                dst = o_ref.at[bb, layer] if fill_depth else o_ref.at[bb]
                if name in _CTX_SEQ_MINOR:
                    dst[...] = yb.T
                else:
                    for hd in range(DIFF_HEADS):
                        dst[:, hd, :] = _head(yb, hd, DIFF_V_DIM)
                if fill_depth:
                    for other in range(DEPTH):
                        if other != layer:
                            o_ref[bb, other] = jnp.zeros(o_ref.shape[2:], F32)
        else:
            o_ref[...] = y.astype(o_ref.dtype)


def _inproj(x2d, mod, g, w_in_bf, tokens_per_mod, *, rope=None, seq=None, layer=0, prev_kv=None):
    t = x2d.shape[0]
    tm = TM_PROJ
    latent = rope is not None
    fill_depth = not latent and prev_kv is None
    in_specs = [
        pl.BlockSpec((tm, D_MODEL), lambda i: (i, 0)),
        pl.BlockSpec((1, 6, D_MODEL), lambda i: ((i * tm) // tokens_per_mod, 0, 0)),
        pl.BlockSpec((1, D_MODEL), lambda i: (0, 0)),
        pl.BlockSpec((None, D_MODEL, IN_W), lambda i: (layer, 0, 0), pipeline_mode=pl.Buffered(1)),
    ]
    args = [x2d, mod, g.reshape(1, D_MODEL), w_in_bf]
    aliases = {}
    n_alias = 0
    if latent:
        nblk = rope[0].shape[0] // tm
        in_specs += [pl.BlockSpec((tm, LANES), lambda i: (i % nblk, 0))] * 2
        args += list(rope)
    elif prev_kv is not None:
        n_alias = len(_CTX_F32)
        in_specs += [pl.BlockSpec(memory_space=pl.ANY)] * n_alias
        args += [prev_kv[name] for name in _CTX_F32]
    out_shape, out_specs = [], []
    tb = tm // seq if not latent else None
    for oi, (name, _, width) in enumerate(_IN_GROUPS):
        if not latent and name in _CTX_F32:
            blk = (width, seq) if name in _CTX_SEQ_MINOR else (seq, DIFF_HEADS, DIFF_V_DIM)
            out_shape.append(jax.ShapeDtypeStruct((t // seq, DEPTH) + blk, F32))
            if fill_depth:
                out_specs.append(pl.BlockSpec((tb, DEPTH) + blk,
                                              lambda i, nd=len(blk): (i, 0) + (0,) * nd))
            else:
                out_specs.append(pl.BlockSpec((tb, None) + blk,
                                              lambda i, nd=len(blk): (i, layer) + (0,) * nd))
                aliases[4 + _CTX_F32.index(name)] = oi
        else:
            out_shape.append(jax.ShapeDtypeStruct((t, width), BF16))
            out_specs.append(pl.BlockSpec((tm, width), lambda i: (i, 0)))
    outs = pl.pallas_call(
        functools.partial(_inproj_kernel, latent=latent, n_alias=n_alias, seq=seq, layer=layer,
                          fill_depth=fill_depth),
        out_shape=out_shape,
        grid=(t // tm,),
        in_specs=in_specs,
        out_specs=out_specs,
        input_output_aliases=aliases,
        compiler_params=_cparams(("parallel",)),
        name="inproj_latent" if latent else "inproj_context",
    )(*args)
    return dict(zip([n for n, _, _ in _IN_GROUPS], outs))


def _attend(q, segs, sink_col=None):
    scores = []
    for k, _, bias, k_seq_minor, _ in segs:
        if k_seq_minor:
            s = jnp.dot(q, k, preferred_element_type=F32)
        else:
            s = lax.dot_general(q, k, _NT, preferred_element_type=F32)
        if bias is not None:
            s = s + bias
        scores.append(s)
    m = scores[0].max(axis=-1, keepdims=True)
    for s in scores[1:]:
        m = jnp.maximum(m, s.max(axis=-1, keepdims=True))
    if sink_col is not None:
        m = jnp.maximum(m, sink_col)
    acc = None
    den = None
    for s, (_, v, _, _, v_seq_minor) in zip(scores, segs):
        p = jnp.exp(s - m)
        ps = p.sum(axis=-1, keepdims=True)
        if v_seq_minor:
            o = lax.dot_general(p.astype(BF16), v, _NT, preferred_element_type=F32)
        else:
            o = jnp.dot(p.astype(BF16), v, preferred_element_type=F32)
        acc = o if acc is None else acc + o
        den = ps if den is None else den + ps
    if sink_col is not None:
        den = den + jnp.exp(sink_col - m)
    return acc, den


def _head(x, h, width=HEAD_DIM):
    return x[:, h * width:(h + 1) * width]


def _head_rows(x, h, width=HEAD_DIM):
    return x[h * width:(h + 1) * width, :]


def _pair(x, i):
    return x[:, i * LANES:(i + 1) * LANES]


def _pair_rows(x, i):
    return x[i * LANES:(i + 1) * LANES, :]


def _keep_half(x, half, axis):
    idx = lax.broadcasted_iota(jnp.int32, x.shape, axis)
    return jnp.where((idx // HEAD_DIM) == half, x, jnp.zeros_like(x))


def _place_rows(x, half):
    z = jnp.zeros_like(x)
    return jnp.concatenate([x, z] if half == 0 else [z, x], axis=0)


def _kv_spec(width, seq, layer):
    return pl.BlockSpec((None, None, width, seq), lambda b, i: (b, layer, 0, 0))


def _sink_attn_kernel(*refs, latent, tq, seq):
    if latent:
        sink_ref, q_ref, k_ref, v_ref, kc_ref, vc_ref, o_ref = refs
    else:
        sink_ref, q_ref, kc_ref, vc_ref, o_ref = refs
    q_all = q_ref[...]
    k_ctx = kc_ref[...].astype(BF16)
    v_ctx = vc_ref[...].astype(BF16)
    if latent:
        band = tq + 2 * WIN
        qb = pl.program_id(1)
        start = pl.multiple_of(jnp.clip(qb * tq - WIN, 0, seq - band), WIN)
        k_loc = k_ref[pl.ds(start, band), :]
        v_loc = v_ref[pl.ds(start, band), :]
        k_swp = jnp.concatenate([_head(k_loc, 1), _head(k_loc, 0)], axis=1)
        v_swp = jnp.concatenate([_head(v_loc, 1), _head(v_loc, 0)], axis=1)
        qpos = qb * tq + lax.broadcasted_iota(jnp.int32, (tq, band), 0)
        kpos = start + lax.broadcasted_iota(jnp.int32, (tq, band), 1)
        bias = jnp.where(jnp.abs(qpos - kpos) <= WIN, 0.0, NEG).astype(F32)
    outs = []
    for i in range(A_HEADS // 2):
        kv = (2 * i) // A_GROUP
        o_pair = None
        for half in range(2):
            h = 2 * i + half
            sink_col = jnp.full((tq, 1), sink_ref[0, h], F32)
            segs = [(_place_rows(_head_rows(k_ctx, kv), half),
                     _place_rows(_head_rows(v_ctx, kv), half), None, True, True)]
            if latent:
                k_src, v_src = (k_loc, v_loc) if kv == half else (k_swp, v_swp)
                segs.append((_keep_half(k_src, half, 1), _keep_half(v_src, half, 1),
                             bias, False, False))
            acc, den = _attend(_pair(q_all, i), segs, sink_col)
            o = acc / den
            o_pair = o if o_pair is None else o_pair + o
        outs.append(o_pair)
    o_ref[...] = jnp.concatenate(outs, axis=1).astype(o_ref.dtype)


def _sink_attn(sink, q, kv_ctx, *, nb, seq, layer, kv_loc=None):
    latent = kv_loc is not None
    tq = TQ_SINK if latent else seq
    nq = seq // tq
    kw = A_KV_HEADS * HEAD_DIM
    s_ctx = kv_ctx[0].shape[-1]
    in_specs = [
        pl.BlockSpec(memory_space=pltpu.SMEM),
        pl.BlockSpec((tq, BR_W), lambda b, i: (b * nq + i, 0)),
    ]
    args = [sink.reshape(1, A_HEADS), q]
    if latent:
        in_specs += [pl.BlockSpec((seq, kw), lambda b, i: (b, 0))] * 2
        args += list(kv_loc)
    in_specs += [_kv_spec(kw, s_ctx, layer)] * 2
    args += list(kv_ctx)
    return pl.pallas_call(
        functools.partial(_sink_attn_kernel, latent=latent, tq=tq, seq=seq),
        out_shape=jax.ShapeDtypeStruct(q.shape, BF16),
        grid=(nb, nq),
        in_specs=in_specs,
        out_specs=pl.BlockSpec((tq, BR_W), lambda b, i: (b * nq + i, 0)),
        compiler_params=_cparams(("parallel", "arbitrary")),
        name="sink_attn_latent" if latent else "sink_attn_context",
    )(*args)


def _diff_attn_kernel(*refs, latent, lam_init):
    if latent:
        lam_ref, g_ref, q_ref, k_ref, v_ref, kc_ref, vc_ref, o_ref = refs
    else:
        lam_ref, g_ref, q_ref, kc_ref, vc_ref, o_ref = refs
    lam = lam_ref[...]
    lam_a = jnp.sum(lam[0:1] * lam[1:2], axis=-1, keepdims=True)
    lam_b = jnp.sum(lam[2:3] * lam[3:4], axis=-1, keepdims=True)
    lam_full = jnp.exp(lam_a) - jnp.exp(lam_b) + lam_init
    q_all = q_ref[...]
    k_ctx = kc_ref[...].astype(BF16)
    if latent:
        k_loc = k_ref[...]
        v_loc = v_ref[...]
    outs = []
    for h in range(DIFF_HEADS):
        v_ctx = vc_ref[:, h, :].astype(BF16)
        res = []
        for sub in range(2):
            segs = [(_keep_half(_pair_rows(k_ctx, h), sub, 0), v_ctx, None, True, False)]
            if latent:
                segs.append((_keep_half(_pair(k_loc, h), sub, 1), _pair(v_loc, h),
                             None, False, False))
            acc, den = _attend(_pair(q_all, h), segs)
            res.append(acc / den)
        o = res[0] - lam_full * res[1]
        ms = jnp.mean(o * o, axis=-1, keepdims=True)
        outs.append(o * lax.rsqrt(ms + EPS) * g_ref[...] * (1.0 - lam_init))
    o_ref[...] = jnp.concatenate(outs, axis=1).astype(o_ref.dtype)


def _diff_attn(lam, sub_g, q, kv_ctx, *, nb, seq, lam_init, layer, kv_loc=None):
    latent = kv_loc is not None
    tq = TQ_DIFF if latent else seq
    nq = seq // tq
    kw = DIFF_HEADS * 2 * HEAD_DIM
    s_ctx = kv_ctx[0].shape[-1]
    in_specs = [
        pl.BlockSpec((4, HEAD_DIM), lambda b, i: (0, 0)),
        pl.BlockSpec((1, DIFF_V_DIM), lambda b, i: (0, 0)),
        pl.BlockSpec((tq, BR_W), lambda b, i: (b * nq + i, 0)),
    ]
    args = [lam, sub_g.reshape(1, DIFF_V_DIM), q]
    if latent:
        in_specs += [pl.BlockSpec((seq, kw), lambda b, i: (b, 0))] * 2
        args += list(kv_loc)
    in_specs += [_kv_spec(kw, s_ctx, layer),
                 pl.BlockSpec((None, None, s_ctx, DIFF_HEADS, DIFF_V_DIM),
                              lambda b, i: (b, layer, 0, 0, 0))]
    args += list(kv_ctx)
    return pl.pallas_call(
        functools.partial(_diff_attn_kernel, latent=latent, lam_init=lam_init),
        out_shape=jax.ShapeDtypeStruct(q.shape, BF16),
        grid=(nb, nq),
        in_specs=in_specs,
        out_specs=pl.BlockSpec((tq, BR_W), lambda b, i: (b * nq + i, 0)),
        compiler_params=_cparams(("parallel", "arbitrary")),
        name="diff_attn_latent" if latent else "diff_attn_context",
    )(*args)


def _dense_attn_kernel(q_ref, k_ref, v_ref, o_ref):
    q_all = q_ref[...]
    k = k_ref[...].astype(BF16)
    v = v_ref[...].astype(BF16)
    outs = []
    for i in range(NA_HEADS // 2):
        o_pair = None
        for half in range(2):
            segs = [(_keep_half(_pair_rows(k, i), half, 0), _keep_half(_pair_rows(v, i), half, 0),
                     None, True, True)]
            acc, den = _attend(_pair(q_all, i), segs)
            o = acc / den
            o_pair = o if o_pair is None else o_pair + o
        outs.append(o_pair)
    o_ref[...] = jnp.concatenate(outs, axis=1).astype(o_ref.dtype)


def _dense_attn(q, kv_ctx, *, nb, seq, layer):
    return pl.pallas_call(
        _dense_attn_kernel,
        out_shape=jax.ShapeDtypeStruct(q.shape, BF16),
        grid=(nb, 1),
        in_specs=[pl.BlockSpec((seq, BR_W), lambda b, i: (b, 0)),
                  _kv_spec(BR_W, seq, layer), _kv_spec(BR_W, seq, layer)],
        out_specs=pl.BlockSpec((seq, BR_W), lambda b, i: (b, 0)),
        compiler_params=_cparams(("parallel", "arbitrary")),
        name="dense_attn_context",
    )(q, *kv_ctx)


def _na_window_start(r, rows):
    return jnp.clip(r - NA_ROWS // 2, 0, rows - NA_ROWS)


def _na_band_start(step, rows):
    return jnp.clip(step * NA_QROWS - NA_ROWS // 2, 0, rows - NA_BAND)


def _na_attn_kernel(q_ref, k_ref, v_ref, kc_ref, vc_ref, tab_ref, o_ref, bias_ref, *, rows):
    step = pl.program_id(0)
    band_row = _na_band_start(step, rows)

    @pl.when(pl.program_id(1) == 0)
    def _():
        for ri in range(NA_QROWS):
            r = step * NA_QROWS + ri
            rs = _na_window_start(r, rows)
            for h in range(NA_HEADS):
                blocks = []
                for j in range(NA_BAND):
                    kr = band_row + j
                    ok = (kr >= rs) & (kr < rs + NA_ROWS)
                    dr = jnp.clip(kr - r + NA_ROWS - 1, 0, 2 * NA_ROWS - 2)
                    blocks.append(jnp.where(ok, tab_ref[h, dr], NEG))
                bias_ref[h, ri * GRID_W:(ri + 1) * GRID_W, :] = jnp.concatenate(blocks, axis=1)

    start = pl.multiple_of(band_row * GRID_W, GRID_W)
    band = NA_BAND * GRID_W
    q_all = q_ref[...]
    k_loc = k_ref[pl.ds(start, band), :]
    v_loc = v_ref[pl.ds(start, band), :]
    k_ctx = kc_ref[...].astype(BF16)
    v_ctx = vc_ref[...].astype(BF16)
    outs = []
    for i in range(NA_HEADS // 2):
        o_pair = None
        for half in range(2):
            segs = [(_keep_half(_pair_rows(k_ctx, i), half, 0),
                     _keep_half(_pair_rows(v_ctx, i), half, 0), None, True, True),
                    (_keep_half(_pair(k_loc, i), half, 1), _keep_half(_pair(v_loc, i), half, 1),
                     bias_ref[2 * i + half], False, False)]
            acc, den = _attend(_pair(q_all, i), segs)
            o = acc / den
            o_pair = o if o_pair is None else o_pair + o
        outs.append(o_pair)
    o_ref[...] = jnp.concatenate(outs, axis=1).astype(o_ref.dtype)


def _na_attn(q, kv_loc, kv_ctx, rpb_tab, *, nb, seq, layer):
    rows = seq // GRID_W
    nsteps = rows // NA_QROWS
    tq = NA_QROWS * GRID_W
    band = NA_BAND * GRID_W
    past = kv_ctx[0].shape[-1]
    for s in range(nsteps):
        us = min(max(s * NA_QROWS - NA_ROWS // 2, 0), rows - NA_BAND)
        for r in range(s * NA_QROWS, (s + 1) * NA_QROWS):
            rs = min(max(r - NA_ROWS // 2, 0), rows - NA_ROWS)
            assert us <= rs and rs + NA_ROWS <= us + NA_BAND
    ctx_spec = pl.BlockSpec((None, None, BR_W, past), lambda s, b: (b, layer, 0, 0))
    return pl.pallas_call(
        functools.partial(_na_attn_kernel, rows=rows),
        out_shape=jax.ShapeDtypeStruct(q.shape, BF16),
        grid=(nsteps, nb),
        in_specs=[
            pl.BlockSpec((tq, BR_W), lambda s, b: (b * nsteps + s, 0)),
            pl.BlockSpec((seq, BR_W), lambda s, b: (b, 0)),
            pl.BlockSpec((seq, BR_W), lambda s, b: (b, 0)),
            ctx_spec, ctx_spec,
            pl.BlockSpec(rpb_tab.shape, lambda s, b: (0, 0, 0, 0)),
        ],
        out_specs=pl.BlockSpec((tq, BR_W), lambda s, b: (b * nsteps + s, 0)),
        scratch_shapes=[pltpu.VMEM((NA_HEADS, tq, band), F32)],
        compiler_params=_cparams(("arbitrary", "arbitrary")),
        name="na_attn_latent",
    )(q, *kv_loc, *kv_ctx, rpb_tab)


def _rpb_expand_kernel(rpb_ref, onehot_ref, o_ref):
    o_ref[...] = jnp.dot(rpb_ref[...], onehot_ref[...], preferred_element_type=F32,
                         precision=lax.Precision.HIGHEST)


def _na_rpb_table(rpb):
    n_dr, n_dc = 2 * NA_ROWS - 1, 2 * NA_COLS - 1
    qcol = np.arange(GRID_W)
    kcol = np.arange(GRID_W)
    cs = np.clip(qcol - NA_COLS // 2, 0, GRID_W - NA_COLS)
    col_ok = (kcol[None, :] >= cs[:, None]) & (kcol[None, :] < cs[:, None] + NA_COLS)
    dc = np.clip(kcol[None, :] - qcol[:, None] + NA_COLS - 1, 0, n_dc - 1)
    onehot = np.zeros((32, GRID_W * GRID_W), np.float32)
    onehot[dc.reshape(-1), np.arange(GRID_W * GRID_W)] = 1.0
    rpb2 = jnp.pad(rpb.reshape(NA_HEADS * n_dr, n_dc), ((0, 0), (0, 32 - n_dc)))
    full = pl.pallas_call(
        _rpb_expand_kernel,
        out_shape=jax.ShapeDtypeStruct((NA_HEADS * n_dr, GRID_W * GRID_W), F32),
        name="rpb_expand",
    )(rpb2, jnp.asarray(onehot))
    full = full.reshape(NA_HEADS, n_dr, GRID_W, GRID_W)
    return full + jnp.asarray(np.where(col_ok, 0.0, NEG).astype(np.float32))


def _dft_tables(seq):
    pos = np.arange(seq)
    ang = 2.0 * np.pi * ((pos[:, None] * pos[None, :]) % seq) / seq
    left = np.concatenate([np.cos(ang), -np.sin(ang)], axis=1).astype(np.float32)
    ch = np.arange(F_GROUP_W)
    angw = 2.0 * np.pi * ((ch[:, None] * ch[None, :]) % F_GROUP_W) / F_GROUP_W
    right = np.concatenate([np.cos(angw), np.sin(angw)], axis=1).astype(np.float32)
    return jnp.asarray(left).astype(BF16), jnp.asarray(right).astype(BF16)


def _fourier_kernel(f_ref, left_ref, right_ref, o_ref, stack_ref, *, seq):
    f = f_ref[...]
    right = right_ref[...]
    for g in range(F_GROUPS):
        t = jnp.dot(_head(f, g, F_GROUP_W), right, preferred_element_type=F32).astype(BF16)
        stack_ref[0:seq, g * F_GROUP_W:(g + 1) * F_GROUP_W] = t[:, :F_GROUP_W]
        stack_ref[seq:2 * seq, g * F_GROUP_W:(g + 1) * F_GROUP_W] = t[:, F_GROUP_W:]
    y = jnp.dot(left_ref[...], stack_ref[...], preferred_element_type=F32)
    o_ref[...] = (y * (1.0 / math.sqrt(seq * F_GROUP_W))).astype(o_ref.dtype)


def _fourier(f, *, nb, seq):
    left, right = _dft_tables(seq)
    w = F_GROUPS * F_GROUP_W
    return pl.pallas_call(
        functools.partial(_fourier_kernel, seq=seq),
        out_shape=jax.ShapeDtypeStruct(f.shape, BF16),
        grid=(nb,),
        in_specs=[
            pl.BlockSpec((seq, w), lambda b: (b, 0)),
            pl.BlockSpec((seq, 2 * seq), lambda b: (0, 0)),
            pl.BlockSpec((F_GROUP_W, 2 * F_GROUP_W), lambda b: (0, 0)),
        ],
        out_specs=pl.BlockSpec((seq, w), lambda b: (b, 0)),
        scratch_shapes=[pltpu.VMEM((2 * seq, w), BF16)],
        compiler_params=_cparams(("parallel",)),
        name=f"fourier_{seq}",
    )(f, left, right)


def _merge_kernel(x_ref, mod_ref, g_ref, *refs):
    br_refs = refs[0:N_BRANCH]
    wbr_ref = refs[N_BRANCH]
    wg_refs = refs[N_BRANCH + 1:2 * N_BRANCH + 1]
    bg_refs = refs[2 * N_BRANCH + 1:3 * N_BRANCH + 1]
    wo_ref, o_ref, h_ref, acc_ref = refs[3 * N_BRANCH + 1:]
    j = pl.program_id(1)

    @pl.when(j == 0)
    def _():
        h_ref[...] = _norm_mod(x_ref[...], g_ref[...], mod_ref[0, 0:1, :],
                               mod_ref[0, 1:2, :]).astype(BF16)
        acc_ref[...] = jnp.zeros_like(acc_ref)

    h = h_ref[...]
    mix = None
    for n in range(N_BRANCH):
        gate = _sigmoid(jnp.dot(h, wg_refs[n][...], preferred_element_type=F32) + bg_refs[n][0])
        proj = jnp.dot(br_refs[n][...], wbr_ref[n], preferred_element_type=F32)
        term = gate * proj
        mix = term if mix is None else mix + term
    acc_ref[...] += jnp.dot(mix.astype(BF16), wo_ref[...], preferred_element_type=F32)

    @pl.when(j == pl.num_programs(1) - 1)
    def _():
        o_ref[...] = x_ref[...] + mod_ref[0, 2:3, :] * acc_ref[...]


def _merge(x2d, mod, g, branches, w_branch_bf, w_gate_bf, b_gate, w_out_bf, tokens_per_mod, layer):
    t = x2d.shape[0]
    tm, tn = TM_PROJ, TN_MERGE
    nch = D_MODEL // tn
    br_spec = pl.BlockSpec((tm, BR_W), lambda i, j: (i, 0))
    wg_specs = [pl.BlockSpec((None, D_MODEL, tn),
                             functools.partial(lambda i, j, n: (layer, 0, n * nch + j), n=n))
                for n in range(N_BRANCH)]
    bg_specs = [pl.BlockSpec((1, 1, tn), functools.partial(lambda i, j, n: (n * nch + j, 0, 0), n=n))
                for n in range(N_BRANCH)]
    bg = b_gate.reshape(N_BRANCH * nch, 1, tn)
    return pl.pallas_call(
        _merge_kernel,
        out_shape=jax.ShapeDtypeStruct((t, D_MODEL), F32),
        grid=(t // tm, nch),
        in_specs=[
            pl.BlockSpec((tm, D_MODEL), lambda i, j: (i, 0)),
            pl.BlockSpec((1, 6, D_MODEL), lambda i, j: ((i * tm) // tokens_per_mod, 0, 0)),
            pl.BlockSpec((1, D_MODEL), lambda i, j: (0, 0)),
            br_spec, br_spec, br_spec, br_spec,
            pl.BlockSpec((None, N_BRANCH, BR_W, tn), lambda i, j: (layer, 0, 0, j)),
            *wg_specs, *bg_specs,
            pl.BlockSpec((None, tn, D_MODEL), lambda i, j: (layer, j, 0)),
        ],
        out_specs=pl.BlockSpec((tm, D_MODEL), lambda i, j: (i, 0)),
        scratch_shapes=[pltpu.VMEM((tm, D_MODEL), BF16), pltpu.VMEM((tm, D_MODEL), F32)],
        compiler_params=_cparams(("parallel", "arbitrary")),
        name="merge",
    )(x2d, mod, g.reshape(1, D_MODEL), *branches, w_branch_bf,
      *([w_gate_bf] * N_BRANCH), *([bg] * N_BRANCH), w_out_bf)


def _route(logits):
    lane = lax.broadcasted_iota(jnp.int32, logits.shape, 1)
    ninf = -jnp.inf
    lg = jnp.where(lane < N_GROUPS, logits, ninf)
    mg = lg.max(axis=-1, keepdims=True)
    pg_top = 1.0 / jnp.exp(lg - mg).sum(axis=-1, keepdims=True)
    g_idx = jnp.where(lg == mg, lane, ROUTER_W).min(axis=-1, keepdims=True)
    lo = N_GROUPS + EXP_PER_GROUP * g_idx
    le = jnp.where((lane >= lo) & (lane < lo + EXP_PER_GROUP), logits, ninf)
    m1 = le.max(axis=-1, keepdims=True)
    e1 = jnp.where(le == m1, lane, ROUTER_W).min(axis=-1, keepdims=True)
    le2 = jnp.where(lane == e1, ninf, le)
    m2 = le2.max(axis=-1, keepdims=True)
    e2 = jnp.where(le2 == m2, lane, ROUTER_W).min(axis=-1, keepdims=True)
    se = jnp.exp(le - m1).sum(axis=-1, keepdims=True)
    pe1 = 1.0 / se
    pe2 = jnp.exp(m2 - m1) / se
    tot = pe1 + pe2
    w1 = pe1 / tot * pg_top
    w2 = pe2 / tot * pg_top
    return jnp.where(lane == e1, w1, 0.0) + jnp.where(lane == e2, w2, 0.0), g_idx


def _moe_kernel(x_ref, mod_ref, g_ref, wr_ref, br_ref, wg_ref, wu_ref, wd_ref, fg_ref,
                o_ref, h_ref, comb_ref, grp_ref, rank_ref, cnt_ref, *, final):
    grp = pl.program_id(1)
    tm = x_ref.shape[0]

    @pl.when(grp == 0)
    def _():
        h = _norm_mod(x_ref[...], g_ref[...], mod_ref[0, 3:4, :], mod_ref[0, 4:5, :])
        h_hi = h.astype(BF16)
        h_ref[...] = h_hi
        h_lo = (h - h_hi.astype(F32)).astype(BF16)
        w = wr_ref[...]
        w_hi = w.astype(BF16)
        w_lo = (w - w_hi.astype(F32)).astype(BF16)
        logits = (jnp.dot(h_hi, w_hi, preferred_element_type=F32)
                  + jnp.dot(h_lo, w_hi, preferred_element_type=F32)
                  + jnp.dot(h_hi, w_lo, preferred_element_type=F32)) + br_ref[...]
        comb, g_idx = _route(logits)
        comb_ref[...] = comb
        lane = lax.broadcasted_iota(jnp.int32, (tm, ROUTER_W), 1)
        member = lane == g_idx
        onehot = jnp.where(member, 1.0, 0.0).astype(BF16)
        earlier = jnp.where(lax.broadcasted_iota(jnp.int32, (tm, tm), 0)
                            > lax.broadcasted_iota(jnp.int32, (tm, tm), 1), 1.0, 0.0).astype(BF16)
        before = jnp.dot(earlier, onehot, preferred_element_type=F32)
        rank_ref[...] = jnp.where(member, before, 0.0).sum(axis=-1, keepdims=True).astype(jnp.int32)
        grp_ref[...] = g_idx
        cnt_ref[...] = jnp.where(member, 1.0, 0.0).sum(axis=0, keepdims=True)
        o_ref[...] = jnp.zeros_like(o_ref)

    lane_r = lax.broadcasted_iota(jnp.int32, (1, ROUTER_W), 1)
    count = jnp.where(lane_r == grp, cnt_ref[...], 0.0).sum().astype(jnp.int32)
    in_group = grp_ref[...] == grp
    rank = rank_ref[...]
    comb = comb_ref[...]
    comb_hi = comb.astype(BF16)
    comb_lo = (comb - comb_hi.astype(F32)).astype(BF16)
    slot_lane = lax.broadcasted_iota(jnp.int32, (tm, MOE_CHUNK), 1)
    lane_c = lax.broadcasted_iota(jnp.int32, (MOE_CHUNK, ROUTER_W), 1)

    def chunk(c, carry):
        sel = jnp.where(in_group & (rank - c * MOE_CHUNK == slot_lane), 1.0, 0.0).astype(BF16)
        xc = lax.dot_general(sel, h_ref[...], _TN, preferred_element_type=F32).astype(BF16)
        cwc = (lax.dot_general(sel, comb_hi, _TN, preferred_element_type=F32)
               + lax.dot_general(sel, comb_lo, _TN, preferred_element_type=F32))
        out = None
        for e in range(EXP_PER_GROUP):
            cw = jnp.where(lane_c == N_GROUPS + EXP_PER_GROUP * grp + e, cwc, 0.0).sum(
                axis=-1, keepdims=True)
            gate = jnp.dot(xc, wg_ref[e], preferred_element_type=F32)
            up = jnp.dot(xc, wu_ref[e], preferred_element_type=F32)
            hid = (gate * _sigmoid(gate)) * up * cw
            term = jnp.dot(hid.astype(BF16), wd_ref[e], preferred_element_type=F32)
            out = term if out is None else out + term
        o_ref[...] += jnp.dot(sel, out.astype(BF16), preferred_element_type=F32)
        return carry

    lax.fori_loop(0, (count + MOE_CHUNK - 1) // MOE_CHUNK, chunk, 0)

    @pl.when(grp == pl.num_programs(1) - 1)
    def _():
        y = x_ref[...] + mod_ref[0, 5:6, :] * o_ref[...]
        if final:
            ms = jnp.mean(y * y, axis=-1, keepdims=True)
            y = y * lax.rsqrt(ms + EPS) * fg_ref[...]
        o_ref[...] = y


def _moe(x2d, mod, g, w_router, b_router, w_g_bf, w_u_bf, w_d_bf, final_g, tokens_per_mod, final,
         layer):
    t = x2d.shape[0]
    tm = TM_PROJ
    w_spec_in = pl.BlockSpec((None, None, EXP_PER_GROUP, D_MODEL, D_EXPERT),
                             lambda i, grp: (layer, grp, 0, 0, 0))
    return pl.pallas_call(
        functools.partial(_moe_kernel, final=final),
        out_shape=jax.ShapeDtypeStruct((t, D_MODEL), F32),
        grid=(t // tm, N_GROUPS),
        in_specs=[
            pl.BlockSpec((tm, D_MODEL), lambda i, grp: (i, 0)),
            pl.BlockSpec((1, 6, D_MODEL), lambda i, grp: ((i * tm) // tokens_per_mod, 0, 0)),
            pl.BlockSpec((1, D_MODEL), lambda i, grp: (0, 0)),
            pl.BlockSpec((D_MODEL, ROUTER_W), lambda i, grp: (0, 0)),
            pl.BlockSpec((1, ROUTER_W), lambda i, grp: (0, 0)),
            w_spec_in, w_spec_in,
            pl.BlockSpec((None, None, EXP_PER_GROUP, D_EXPERT, D_MODEL),
                         lambda i, grp: (layer, grp, 0, 0, 0)),
            pl.BlockSpec((1, D_MODEL), lambda i, grp: (0, 0)),
        ],
        out_specs=pl.BlockSpec((tm, D_MODEL), lambda i, grp: (i, 0)),
        scratch_shapes=[pltpu.VMEM((tm, D_MODEL), BF16), pltpu.VMEM((tm, ROUTER_W), F32),
                        pltpu.VMEM((tm, 1), jnp.int32), pltpu.VMEM((tm, 1), jnp.int32),
                        pltpu.VMEM((1, ROUTER_W), F32)],
        compiler_params=_cparams(("parallel", "arbitrary")),
        name="moe_final" if final else "moe",
    )(x2d, mod, g.reshape(1, D_MODEL), w_router, b_router, w_g_bf, w_u_bf, w_d_bf,
      final_g.reshape(1, D_MODEL))


def _rope_tables(seq):
    nf = HEAD_DIM // 4
    t = jnp.arange(seq)
    inv = ROPE_BASE ** (-jnp.arange(nf, dtype=F32) / nf)
    pos = jnp.stack([t // GRID_W, t % GRID_W], -1).astype(F32)
    ang = pos[:, :, None] * inv
    cos, sin = jnp.cos(ang), jnp.sin(ang)
    cos64 = jnp.concatenate([cos[:, 0], cos[:, 0], cos[:, 1], cos[:, 1]], axis=-1)
    sin64 = jnp.concatenate([-sin[:, 0], sin[:, 0], -sin[:, 1], sin[:, 1]], axis=-1)
    return jnp.tile(cos64, (1, 2)), jnp.tile(sin64, (1, 2))


def _seq_minor_view(cache):
    nb, depth, seq = cache.shape[:3]
    nd = cache.ndim
    return jnp.transpose(cache, (0, 1, *range(3, nd), 2)).reshape(nb, depth, -1, seq)


def _seq_major_view(arr, head_dims):
    nb, depth, _, seq = arr.shape
    nd = 3 + len(head_dims)
    arr = arr.reshape(nb, depth, *head_dims, seq)
    return jnp.transpose(arr, (0, 1, nd - 1, *range(2, nd - 1)))


def kernel(x_prompt, x_sample, cache_attn_k, cache_attn_v, cache_diff_k, cache_diff_v, cache_na_k, cache_na_v, c, c_ctx, w_mod, b_mod, norm1_g, w_in, attn_sink, diff_lambda, diff_sub_g, na_rpb, w_branch, w_gate, b_gate, w_out, norm2_g, w_router_group, b_router_group, w_router_expert, b_router_expert, w_exp_gate, w_exp_up, w_exp_down, final_g):
    nbp, seq_p, _ = x_prompt.shape
    nbs, seq_s, _ = x_sample.shape
    xp = x_prompt.reshape(nbp * seq_p, D_MODEL)
    xs = x_sample.reshape(nbs * seq_s, D_MODEL)

    cond = jnp.concatenate([c_ctx[None, :], c], axis=0)
    n_cond = -(-cond.shape[0] // 8) * 8
    cond = jnp.pad(cond, ((0, n_cond - cond.shape[0]), (0, 0)))
    mods = _modulation(cond, w_mod, b_mod).reshape(DEPTH, n_cond, 6, D_MODEL)

    rope = _rope_tables(seq_s)
    cache_a = (_seq_minor_view(cache_attn_k), _seq_minor_view(cache_attn_v))
    cache_b = (_seq_minor_view(cache_diff_k), cache_diff_v)
    cache_c = (_seq_minor_view(cache_na_k), _seq_minor_view(cache_na_v))

    w_in_bf = w_in.astype(BF16)
    w_branch_bf = w_branch.astype(BF16)
    w_gate_bf = w_gate.astype(BF16)
    w_out_bf = w_out.astype(BF16)
    grouped = (DEPTH, N_GROUPS, EXP_PER_GROUP)
    w_eg_bf = w_exp_gate.astype(BF16).reshape(*grouped, D_MODEL, D_EXPERT)
    w_eu_bf = w_exp_up.astype(BF16).reshape(*grouped, D_MODEL, D_EXPERT)
    w_ed_bf = w_exp_down.astype(BF16).reshape(*grouped, D_EXPERT, D_MODEL)
    pad = ROUTER_W - N_GROUPS - N_EXPERTS
    w_router = jnp.concatenate(
        [w_router_group, w_router_expert, jnp.zeros((DEPTH, D_MODEL, pad), F32)], axis=-1)
    b_router = jnp.concatenate(
        [b_router_group, b_router_expert, jnp.zeros((DEPTH, pad), F32)], axis=-1)

    kv = None
    for l in range(DEPTH):
        lam_init = 0.8 - 0.6 * math.exp(-0.3 * l)
        mod_p = mods[l, 0:1]
        mod_s = mods[l, 1:1 + nbs]
        final = l == DEPTH - 1
        moe_w = (w_router[l], b_router[l].reshape(1, ROUTER_W), w_eg_bf, w_eu_bf, w_ed_bf)
        merge_w = (w_branch_bf, w_gate_bf, b_gate[l], w_out_bf)

        pr = _inproj(xp, mod_p, norm1_g[l], w_in_bf, xp.shape[0], seq=seq_p, layer=l, prev_kv=kv)
        kv = {name: pr[name] for name in _CTX_F32}
        o_a = _sink_attn(attn_sink[l], pr["a_q"], (kv["a_k"], kv["a_v"]), nb=nbp, seq=seq_p, layer=l)
        o_b = _diff_attn(diff_lambda[l], diff_sub_g[l], pr["b_q"], (kv["b_k"], kv["b_v"]),
                         nb=nbp, seq=seq_p, lam_init=lam_init, layer=l)
        o_c = _dense_attn(pr["c_q"], (kv["c_k"], kv["c_v"]), nb=nbp, seq=seq_p, layer=l)
        o_f = _fourier(pr["f"], nb=nbp, seq=seq_p)
        xp = _merge(xp, mod_p, norm1_g[l], (o_a, o_b, o_c, o_f), *merge_w, xp.shape[0], l)
        xp = _moe(xp, mod_p, norm2_g[l], *moe_w, final_g, xp.shape[0], final, l)

        pr = _inproj(xs, mod_s, norm1_g[l], w_in_bf, seq_s, rope=rope, layer=l)
        o_a = _sink_attn(attn_sink[l], pr["a_q"], cache_a, nb=nbs, seq=seq_s, layer=l,
                         kv_loc=(pr["a_k"], pr["a_v"]))
        o_b = _diff_attn(diff_lambda[l], diff_sub_g[l], pr["b_q"], cache_b, nb=nbs, seq=seq_s,
                         lam_init=lam_init, layer=l, kv_loc=(pr["b_k"], pr["b_v"]))
        o_c = _na_attn(pr["c_q"], (pr["c_k"], pr["c_v"]), cache_c,
                       _na_rpb_table(na_rpb[l]), nb=nbs, seq=seq_s, layer=l)
        o_f = _fourier(pr["f"], nb=nbs, seq=seq_s)
        xs = _merge(xs, mod_s, norm1_g[l], (o_a, o_b, o_c, o_f), *merge_w, seq_s, l)
        xs = _moe(xs, mod_s, norm2_g[l], *moe_w, final_g, seq_s, final, l)

    return (
        xp.reshape(x_prompt.shape),
        xs.reshape(x_sample.shape),
        _seq_major_view(kv["a_k"], (A_KV_HEADS, HEAD_DIM)),
        _seq_major_view(kv["a_v"], (A_KV_HEADS, HEAD_DIM)),
        _seq_major_view(kv["b_k"], (DIFF_HEADS, 2, HEAD_DIM)),
        kv["b_v"],
        _seq_major_view(kv["c_k"], (NA_HEADS, HEAD_DIM)),
        _seq_major_view(kv["c_v"], (NA_HEADS, HEAD_DIM)),
    )
```

```python
import functools
import math

import jax
import jax.numpy as jnp
import numpy as np
from jax import lax
from jax.experimental import pallas as pl
from jax.experimental.pallas import tpu as pltpu

F32 = jnp.float32
BF16 = jnp.bfloat16

D_MODEL = 2048
DEPTH = 2
GRID_W = 64
HEAD_DIM = 64
A_HEADS = 8
A_KV_HEADS = 2
A_GROUP = A_HEADS // A_KV_HEADS
WIN = 128
DIFF_HEADS = 4
DIFF_V_DIM = 2 * HEAD_DIM
NA_HEADS = 8
NA_ROWS = 8
NA_COLS = 16
F_GROUPS = 4
F_GROUP_W = 128
N_BRANCH = 4
BR_W = A_HEADS * HEAD_DIM
N_GROUPS = 4
EXP_PER_GROUP = 4
N_EXPERTS = N_GROUPS * EXP_PER_GROUP
D_EXPERT = 256
ROPE_BASE = 10000.0
EPS = 1e-6
NEG = -1e30
SCALE = HEAD_DIM ** -0.5

_IN_GROUPS = (
    ("a_q", 0, 512), ("a_k", 512, 128), ("a_v", 640, 128),
    ("b_q", 768, 512), ("b_k", 1280, 512), ("b_v", 1792, 512),
    ("c_q", 2304, 512), ("c_k", 2816, 512), ("c_v", 3328, 512),
    ("f", 3840, 512),
)
IN_W = 4352
_Q_GROUPS = ("a_q", "b_q", "c_q")
_ROPED = ("a_q", "a_k", "b_q", "b_k")
_CTX_F32 = ("a_k", "a_v", "b_k", "b_v", "c_k", "c_v")
_CTX_SEQ_MINOR = ("a_k", "a_v", "b_k", "c_k", "c_v")

LANES = 128
VMEM_LIMIT = 56 * 1024 * 1024
VMEM_LIMIT_MOE = 59 * 1024 * 1024

TM_PROJ = 512
TQ_SINK = 256
TQ_DIFF = 512
NA_QROWS = 4
NA_BAND = NA_ROWS + NA_QROWS - 1
TN_MERGE = 256
TN_MOD = 1024
ROUTER_W = 128
MOE_CHUNK = 128

_NT = (((1,), (1,)), ((), ()))
_TN = (((0,), (0,)), ((), ()))


def _cparams(sem, vmem_limit=VMEM_LIMIT):
    return pltpu.CompilerParams(dimension_semantics=sem, vmem_limit_bytes=vmem_limit)


def _sigmoid(x):
    return 1.0 / (1.0 + jnp.exp(-x))


def _norm_mod(x, g, shift, scale):
    ms = jnp.mean(x * x, axis=-1, keepdims=True)
    return (x * lax.rsqrt(ms + EPS) * g) * (1.0 + scale) + shift


def _mod_kernel(c_ref, w_ref, b_ref, o_ref):
    c = c_ref[...]
    s = (c * _sigmoid(c)).astype(BF16)
    o_ref[0] = jnp.dot(s, w_ref[0].astype(BF16), preferred_element_type=F32) + b_ref[0]


def _modulation(cond, w_mod, b_mod):
    n = cond.shape[0]
    out_w = w_mod.shape[-1]
    return pl.pallas_call(
        _mod_kernel,
        out_shape=jax.ShapeDtypeStruct((DEPTH, n, out_w), F32),
        grid=(DEPTH, out_w // TN_MOD),
        in_specs=[
            pl.BlockSpec((n, D_MODEL), lambda l, j: (0, 0)),
            pl.BlockSpec((1, D_MODEL, TN_MOD), lambda l, j: (l, 0, j)),
            pl.BlockSpec((1, 1, TN_MOD), lambda l, j: (l, 0, j)),
        ],
        out_specs=pl.BlockSpec((1, n, TN_MOD), lambda l, j: (l, 0, j)),
        compiler_params=_cparams(("parallel", "parallel")),
        name="modulation",
    )(cond, w_mod, b_mod.reshape(DEPTH, 1, out_w))


def _rope128(y, cos, sin):
    lane = lax.broadcasted_iota(jnp.int32, y.shape, 1)
    first = (lane % 32) < 16
    partner = jnp.where(first, pltpu.roll(y, LANES - 16, 1), pltpu.roll(y, 16, 1))
    return y * cos + partner * sin


def _inproj_kernel(*refs, latent, n_alias, seq, layer, fill_depth):
    if latent:
        x_ref, mod_ref, g_ref, w_ref, cos_ref, sin_ref = refs[:6]
        out_refs = refs[6:]
    else:
        x_ref, mod_ref, g_ref, w_ref = refs[:4]
        out_refs = refs[4 + n_alias:]
    h = _norm_mod(x_ref[...], g_ref[...], mod_ref[0, 0:1, :], mod_ref[0, 1:2, :]).astype(BF16)
    for (name, start, width), o_ref in zip(_IN_GROUPS, out_refs):
        y = jnp.dot(h, w_ref[:, start:start + width], preferred_element_type=F32)
        if name in _Q_GROUPS:
            y = y * SCALE
        if latent and name in _ROPED:
            cos = cos_ref[...]
            sin = sin_ref[...]
            for c0 in range(0, width, LANES):
                o_ref[:, c0:c0 + LANES] = _rope128(y[:, c0:c0 + LANES], cos, sin).astype(o_ref.dtype)
        elif not latent and name in _CTX_F32:
            for bb in range(o_ref.shape[0]):
                yb = y[bb * seq:(bb + 1) * seq]
                dst = o_ref.at[bb, layer] if fill_depth else o_ref.at[bb]
                if name in _CTX_SEQ_MINOR:
                    dst[...] = yb.T
                else:
                    for hd in range(DIFF_HEADS):
                        dst[:, hd, :] = _head(yb, hd, DIFF_V_DIM)
                if fill_depth:
                    for other in range(DEPTH):
                        if other != layer:
                            o_ref[bb, other] = jnp.zeros(o_ref.shape[2:], F32)
        else:
            o_ref[...] = y.astype(o_ref.dtype)


def _inproj(x2d, mod, g, w_in_bf, tokens_per_mod, *, rope=None, seq=None, layer=0, prev_kv=None):
    t = x2d.shape[0]
    tm = TM_PROJ
    latent = rope is not None
    fill_depth = not latent and prev_kv is None
    in_specs = [
        pl.BlockSpec((tm, D_MODEL), lambda i: (i, 0)),
        pl.BlockSpec((1, 6, D_MODEL), lambda i: ((i * tm) // tokens_per_mod, 0, 0)),
        pl.BlockSpec((1, D_MODEL), lambda i: (0, 0)),
        pl.BlockSpec((None, D_MODEL, IN_W), lambda i: (layer, 0, 0), pipeline_mode=pl.Buffered(1)),
    ]
    args = [x2d, mod, g.reshape(1, D_MODEL), w_in_bf]
    aliases = {}
    n_alias = 0
    if latent:
        nblk = rope[0].shape[0] // tm
        in_specs += [pl.BlockSpec((tm, LANES), lambda i: (i % nblk, 0))] * 2
        args += list(rope)
    elif prev_kv is not None:
        n_alias = len(_CTX_F32)
        in_specs += [pl.BlockSpec(memory_space=pl.ANY)] * n_alias
        args += [prev_kv[name] for name in _CTX_F32]
    out_shape, out_specs = [], []
    tb = tm // seq if not latent else None
    for oi, (name, _, width) in enumerate(_IN_GROUPS):
        if not latent and name in _CTX_F32:
            blk = (width, seq) if name in _CTX_SEQ_MINOR else (seq, DIFF_HEADS, DIFF_V_DIM)
            out_shape.append(jax.ShapeDtypeStruct((t // seq, DEPTH) + blk, F32))
            if fill_depth:
                out_specs.append(pl.BlockSpec((tb, DEPTH) + blk,
                                              lambda i, nd=len(blk): (i, 0) + (0,) * nd))
            else:
                out_specs.append(pl.BlockSpec((tb, None) + blk,
                                              lambda i, nd=len(blk): (i, layer) + (0,) * nd))
                aliases[4 + _CTX_F32.index(name)] = oi
        else:
            out_shape.append(jax.ShapeDtypeStruct((t, width), BF16))
            out_specs.append(pl.BlockSpec((tm, width), lambda i: (i, 0)))
    outs = pl.pallas_call(
        functools.partial(_inproj_kernel, latent=latent, n_alias=n_alias, seq=seq, layer=layer,
                          fill_depth=fill_depth),
        out_shape=out_shape,
        grid=(t // tm,),
        in_specs=in_specs,
        out_specs=out_specs,
        input_output_aliases=aliases,
        compiler_params=_cparams(("parallel",)),
        name="inproj_latent" if latent else "inproj_context",
    )(*args)
    return dict(zip([n for n, _, _ in _IN_GROUPS], outs))


def _attend(q, segs, sink_col=None):
    scores = []
    for k, _, bias, k_seq_minor, _ in segs:
        if k_seq_minor:
            s = jnp.dot(q, k, preferred_element_type=F32)
        else:
            s = lax.dot_general(q, k, _NT, preferred_element_type=F32)
        if bias is not None:
            s = s + bias
        scores.append(s)
    m = scores[0].max(axis=-1, keepdims=True)
    for s in scores[1:]:
        m = jnp.maximum(m, s.max(axis=-1, keepdims=True))
    if sink_col is not None:
        m = jnp.maximum(m, sink_col)
    acc = None
    den = None
    for s, (_, v, _, _, v_seq_minor) in zip(scores, segs):
        p = jnp.exp(s - m)
        ps = p.sum(axis=-1, keepdims=True)
        if v_seq_minor:
            o = lax.dot_general(p.astype(BF16), v, _NT, preferred_element_type=F32)
        else:
            o = jnp.dot(p.astype(BF16), v, preferred_element_type=F32)
        acc = o if acc is None else acc + o
        den = ps if den is None else den + ps
    if sink_col is not None:
        den = den + jnp.exp(sink_col - m)
    return acc, den


def _head(x, h, width=HEAD_DIM):
    return x[:, h * width:(h + 1) * width]


def _head_rows(x, h, width=HEAD_DIM):
    return x[h * width:(h + 1) * width, :]


def _pair(x, i):
    return x[:, i * LANES:(i + 1) * LANES]


def _pair_rows(x, i):
    return x[i * LANES:(i + 1) * LANES, :]


def _keep_half(x, half, axis):
    idx = lax.broadcasted_iota(jnp.int32, x.shape, axis)
    return jnp.where((idx // HEAD_DIM) == half, x, jnp.zeros_like(x))


def _place_rows(x, half):
    z = jnp.zeros_like(x)
    return jnp.concatenate([x, z] if half == 0 else [z, x], axis=0)


def _kv_spec(width, seq, layer):
    return pl.BlockSpec((None, None, width, seq), lambda b, i: (b, layer, 0, 0))


def _sink_attn_kernel(*refs, latent, tq, seq):
    if latent:
        sink_ref, q_ref, k_ref, v_ref, kc_ref, vc_ref, o_ref = refs
    else:
        sink_ref, q_ref, kc_ref, vc_ref, o_ref = refs
    q_all = q_ref[...]
    k_ctx = kc_ref[...].astype(BF16)
    v_ctx = vc_ref[...].astype(BF16)
    if latent:
        band = tq + 2 * WIN
        qb = pl.program_id(1)
        start = pl.multiple_of(jnp.clip(qb * tq - WIN, 0, seq - band), WIN)
        k_loc = k_ref[pl.ds(start, band), :]
        v_loc = v_ref[pl.ds(start, band), :]
        k_swp = jnp.concatenate([_head(k_loc, 1), _head(k_loc, 0)], axis=1)
        v_swp = jnp.concatenate([_head(v_loc, 1), _head(v_loc, 0)], axis=1)
        qpos = qb * tq + lax.broadcasted_iota(jnp.int32, (tq, band), 0)
        kpos = start + lax.broadcasted_iota(jnp.int32, (tq, band), 1)
        bias = jnp.where(jnp.abs(qpos - kpos) <= WIN, 0.0, NEG).astype(F32)
    outs = []
    for i in range(A_HEADS // 2):
        kv = (2 * i) // A_GROUP
        o_pair = None
        for half in range(2):
            h = 2 * i + half
            sink_col = jnp.full((tq, 1), sink_ref[0, h], F32)
            segs = [(_place_rows(_head_rows(k_ctx, kv), half),
                     _place_rows(_head_rows(v_ctx, kv), half), None, True, True)]
            if latent:
                k_src, v_src = (k_loc, v_loc) if kv == half else (k_swp, v_swp)
                segs.append((_keep_half(k_src, half, 1), _keep_half(v_src, half, 1),
                             bias, False, False))
            acc, den = _attend(_pair(q_all, i), segs, sink_col)
            o = acc / den
            o_pair = o if o_pair is None else o_pair + o
        outs.append(o_pair)
    o_ref[...] = jnp.concatenate(outs, axis=1).astype(o_ref.dtype)


def _sink_attn(sink, q, kv_ctx, *, nb, seq, layer, kv_loc=None):
    latent = kv_loc is not None
    tq = TQ_SINK if latent else seq
    nq = seq // tq
    kw = A_KV_HEADS * HEAD_DIM
    s_ctx = kv_ctx[0].shape[-1]
    in_specs = [
        pl.BlockSpec(memory_space=pltpu.SMEM),
        pl.BlockSpec((tq, BR_W), lambda b, i: (b * nq + i, 0)),
    ]
    args = [sink.reshape(1, A_HEADS), q]
    if latent:
        in_specs += [pl.BlockSpec((seq, kw), lambda b, i: (b, 0))] * 2
        args += list(kv_loc)
    in_specs += [_kv_spec(kw, s_ctx, layer)] * 2
    args += list(kv_ctx)
    return pl.pallas_call(
        functools.partial(_sink_attn_kernel, latent=latent, tq=tq, seq=seq),
        out_shape=jax.ShapeDtypeStruct(q.shape, BF16),
        grid=(nb, nq),
        in_specs=in_specs,
        out_specs=pl.BlockSpec((tq, BR_W), lambda b, i: (b * nq + i, 0)),
        compiler_params=_cparams(("parallel", "arbitrary")),
        name="sink_attn_latent" if latent else "sink_attn_context",
    )(*args)


def _diff_attn_kernel(*refs, latent, lam_init):
    if latent:
        lam_ref, g_ref, q_ref, k_ref, v_ref, kc_ref, vc_ref, o_ref = refs
    else:
        lam_ref, g_ref, q_ref, kc_ref, vc_ref, o_ref = refs
    lam = lam_ref[...]
    lam_a = jnp.sum(lam[0:1] * lam[1:2], axis=-1, keepdims=True)
    lam_b = jnp.sum(lam[2:3] * lam[3:4], axis=-1, keepdims=True)
    lam_full = jnp.exp(lam_a) - jnp.exp(lam_b) + lam_init
    q_all = q_ref[...]
    k_ctx = kc_ref[...].astype(BF16)
    if latent:
        k_loc = k_ref[...]
        v_loc = v_ref[...]
    outs = []
    for h in range(DIFF_HEADS):
        v_ctx = vc_ref[:, h, :].astype(BF16)
        res = []
        for sub in range(2):
            segs = [(_keep_half(_pair_rows(k_ctx, h), sub, 0), v_ctx, None, True, False)]
            if latent:
                segs.append((_keep_half(_pair(k_loc, h), sub, 1), _pair(v_loc, h),
                             None, False, False))
            acc, den = _attend(_pair(q_all, h), segs)
            res.append(acc / den)
        o = res[0] - lam_full * res[1]
        ms = jnp.mean(o * o, axis=-1, keepdims=True)
        outs.append(o * lax.rsqrt(ms + EPS) * g_ref[...] * (1.0 - lam_init))
    o_ref[...] = jnp.concatenate(outs, axis=1).astype(o_ref.dtype)


def _diff_attn(lam, sub_g, q, kv_ctx, *, nb, seq, lam_init, layer, kv_loc=None):
    latent = kv_loc is not None
    tq = TQ_DIFF if latent else seq
    nq = seq // tq
    kw = DIFF_HEADS * 2 * HEAD_DIM
    s_ctx = kv_ctx[0].shape[-1]
    in_specs = [
        pl.BlockSpec((4, HEAD_DIM), lambda b, i: (0, 0)),
        pl.BlockSpec((1, DIFF_V_DIM), lambda b, i: (0, 0)),
        pl.BlockSpec((tq, BR_W), lambda b, i: (b * nq + i, 0)),
    ]
    args = [lam, sub_g.reshape(1, DIFF_V_DIM), q]
    if latent:
        in_specs += [pl.BlockSpec((seq, kw), lambda b, i: (b, 0))] * 2
        args += list(kv_loc)
    in_specs += [_kv_spec(kw, s_ctx, layer),
                 pl.BlockSpec((None, None, s_ctx, DIFF_HEADS, DIFF_V_DIM),
                              lambda b, i: (b, layer, 0, 0, 0))]
    args += list(kv_ctx)
    return pl.pallas_call(
        functools.partial(_diff_attn_kernel, latent=latent, lam_init=lam_init),
        out_shape=jax.ShapeDtypeStruct(q.shape, BF16),
        grid=(nb, nq),
        in_specs=in_specs,
        out_specs=pl.BlockSpec((tq, BR_W), lambda b, i: (b * nq + i, 0)),
        compiler_params=_cparams(("parallel", "arbitrary")),
        name="diff_attn_latent" if latent else "diff_attn_context",
    )(*args)


def _dense_attn_kernel(q_ref, k_ref, v_ref, o_ref):
    q_all = q_ref[...]
    k = k_ref[...].astype(BF16)
    v = v_ref[...].astype(BF16)
    outs = []
    for i in range(NA_HEADS // 2):
        o_pair = None
        for half in range(2):
            segs = [(_keep_half(_pair_rows(k, i), half, 0), _keep_half(_pair_rows(v, i), half, 0),
                     None, True, True)]
            acc, den = _attend(_pair(q_all, i), segs)
            o = acc / den
            o_pair = o if o_pair is None else o_pair + o
        outs.append(o_pair)
    o_ref[...] = jnp.concatenate(outs, axis=1).astype(o_ref.dtype)


def _dense_attn(q, kv_ctx, *, nb, seq, layer):
    return pl.pallas_call(
        _dense_attn_kernel,
        out_shape=jax.ShapeDtypeStruct(q.shape, BF16),
        grid=(nb, 1),
        in_specs=[pl.BlockSpec((seq, BR_W), lambda b, i: (b, 0)),
                  _kv_spec(BR_W, seq, layer), _kv_spec(BR_W, seq, layer)],
        out_specs=pl.BlockSpec((seq, BR_W), lambda b, i: (b, 0)),
        compiler_params=_cparams(("parallel", "arbitrary")),
        name="dense_attn_context",
    )(q, *kv_ctx)


def _na_window_start(r, rows):
    return jnp.clip(r - NA_ROWS // 2, 0, rows - NA_ROWS)


def _na_band_start(step, rows):
    return jnp.clip(step * NA_QROWS - NA_ROWS // 2, 0, rows - NA_BAND)


def _na_attn_kernel(q_ref, k_ref, v_ref, kc_ref, vc_ref, tab_ref, o_ref, bias_ref, *, rows):
    step = pl.program_id(0)
    band_row = _na_band_start(step, rows)

    @pl.when(pl.program_id(1) == 0)
    def _():
        for ri in range(NA_QROWS):
            r = step * NA_QROWS + ri
            rs = _na_window_start(r, rows)
            for h in range(NA_HEADS):
                blocks = []
                for j in range(NA_BAND):
                    kr = band_row + j
                    ok = (kr >= rs) & (kr < rs + NA_ROWS)
                    dr = jnp.clip(kr - r + NA_ROWS - 1, 0, 2 * NA_ROWS - 2)
                    blocks.append(jnp.where(ok, tab_ref[h, dr], NEG))
                bias_ref[h, ri * GRID_W:(ri + 1) * GRID_W, :] = jnp.concatenate(blocks, axis=1)

    start = pl.multiple_of(band_row * GRID_W, GRID_W)
    band = NA_BAND * GRID_W
    q_all = q_ref[...]
    k_loc = k_ref[pl.ds(start, band), :]
    v_loc = v_ref[pl.ds(start, band), :]
    k_ctx = kc_ref[...].astype(BF16)
    v_ctx = vc_ref[...].astype(BF16)
    outs = []
    for i in range(NA_HEADS // 2):
        o_pair = None
        for half in range(2):
            segs = [(_keep_half(_pair_rows(k_ctx, i), half, 0),
                     _keep_half(_pair_rows(v_ctx, i), half, 0), None, True, True),
                    (_keep_half(_pair(k_loc, i), half, 1), _keep_half(_pair(v_loc, i), half, 1),
                     bias_ref[2 * i + half], False, False)]
            acc, den = _attend(_pair(q_all, i), segs)
            o = acc / den
            o_pair = o if o_pair is None else o_pair + o
        outs.append(o_pair)
    o_ref[...] = jnp.concatenate(outs, axis=1).astype(o_ref.dtype)


def _na_attn(q, kv_loc, kv_ctx, rpb_tab, *, nb, seq, layer):
    rows = seq // GRID_W
    nsteps = rows // NA_QROWS
    tq = NA_QROWS * GRID_W
    band = NA_BAND * GRID_W
    past = kv_ctx[0].shape[-1]
    for s in range(nsteps):
        us = min(max(s * NA_QROWS - NA_ROWS // 2, 0), rows - NA_BAND)
        for r in range(s * NA_QROWS, (s + 1) * NA_QROWS):
            rs = min(max(r - NA_ROWS // 2, 0), rows - NA_ROWS)
            assert us <= rs and rs + NA_ROWS <= us + NA_BAND
    ctx_spec = pl.BlockSpec((None, None, BR_W, past), lambda s, b: (b, layer, 0, 0))
    return pl.pallas_call(
        functools.partial(_na_attn_kernel, rows=rows),
        out_shape=jax.ShapeDtypeStruct(q.shape, BF16),
        grid=(nsteps, nb),
        in_specs=[
            pl.BlockSpec((tq, BR_W), lambda s, b: (b * nsteps + s, 0)),
            pl.BlockSpec((seq, BR_W), lambda s, b: (b, 0)),
            pl.BlockSpec((seq, BR_W), lambda s, b: (b, 0)),
            ctx_spec, ctx_spec,
            pl.BlockSpec(rpb_tab.shape, lambda s, b: (0, 0, 0, 0)),
        ],
        out_specs=pl.BlockSpec((tq, BR_W), lambda s, b: (b * nsteps + s, 0)),
        scratch_shapes=[pltpu.VMEM((NA_HEADS, tq, band), F32)],
        compiler_params=_cparams(("arbitrary", "arbitrary")),
        name="na_attn_latent",
    )(q, *kv_loc, *kv_ctx, rpb_tab)


def _rpb_expand_kernel(rpb_ref, onehot_ref, o_ref):
    o_ref[...] = jnp.dot(rpb_ref[...], onehot_ref[...], preferred_element_type=F32,
                         precision=lax.Precision.HIGHEST)


def _na_rpb_table(rpb):
    n_dr, n_dc = 2 * NA_ROWS - 1, 2 * NA_COLS - 1
    qcol = np.arange(GRID_W)
    kcol = np.arange(GRID_W)
    cs = np.clip(qcol - NA_COLS // 2, 0, GRID_W - NA_COLS)
    col_ok = (kcol[None, :] >= cs[:, None]) & (kcol[None, :] < cs[:, None] + NA_COLS)
    dc = np.clip(kcol[None, :] - qcol[:, None] + NA_COLS - 1, 0, n_dc - 1)
    onehot = np.zeros((32, GRID_W * GRID_W), np.float32)
    onehot[dc.reshape(-1), np.arange(GRID_W * GRID_W)] = 1.0
    rpb2 = jnp.pad(rpb.reshape(NA_HEADS * n_dr, n_dc), ((0, 0), (0, 32 - n_dc)))
    full = pl.pallas_call(
        _rpb_expand_kernel,
        out_shape=jax.ShapeDtypeStruct((NA_HEADS * n_dr, GRID_W * GRID_W), F32),
        name="rpb_expand",
    )(rpb2, jnp.asarray(onehot))
    full = full.reshape(NA_HEADS, n_dr, GRID_W, GRID_W)
    return full + jnp.asarray(np.where(col_ok, 0.0, NEG).astype(np.float32))


def _dft_tables(seq):
    pos = np.arange(seq)
    ang = 2.0 * np.pi * ((pos[:, None] * pos[None, :]) % seq) / seq
    left = np.concatenate([np.cos(ang), -np.sin(ang)], axis=1).astype(np.float32)
    ch = np.arange(F_GROUP_W)
    angw = 2.0 * np.pi * ((ch[:, None] * ch[None, :]) % F_GROUP_W) / F_GROUP_W
    right = np.concatenate([np.cos(angw), np.sin(angw)], axis=1).astype(np.float32)
    return jnp.asarray(left).astype(BF16), jnp.asarray(right).astype(BF16)


def _fourier_kernel(f_ref, left_ref, right_ref, o_ref, stack_ref, *, seq):
    f = f_ref[...]
    right = right_ref[...]
    for g in range(F_GROUPS):
        t = jnp.dot(_head(f, g, F_GROUP_W), right, preferred_element_type=F32).astype(BF16)
        stack_ref[0:seq, g * F_GROUP_W:(g + 1) * F_GROUP_W] = t[:, :F_GROUP_W]
        stack_ref[seq:2 * seq, g * F_GROUP_W:(g + 1) * F_GROUP_W] = t[:, F_GROUP_W:]
    y = jnp.dot(left_ref[...], stack_ref[...], preferred_element_type=F32)
    o_ref[...] = (y * (1.0 / math.sqrt(seq * F_GROUP_W))).astype(o_ref.dtype)


def _fourier(f, *, nb, seq):
    left, right = _dft_tables(seq)
    w = F_GROUPS * F_GROUP_W
    return pl.pallas_call(
        functools.partial(_fourier_kernel, seq=seq),
        out_shape=jax.ShapeDtypeStruct(f.shape, BF16),
        grid=(nb,),
        in_specs=[
            pl.BlockSpec((seq, w), lambda b: (b, 0)),
            pl.BlockSpec((seq, 2 * seq), lambda b: (0, 0)),
            pl.BlockSpec((F_GROUP_W, 2 * F_GROUP_W), lambda b: (0, 0)),
        ],
        out_specs=pl.BlockSpec((seq, w), lambda b: (b, 0)),
        scratch_shapes=[pltpu.VMEM((2 * seq, w), BF16)],
        compiler_params=_cparams(("parallel",)),
        name=f"fourier_{seq}",
    )(f, left, right)


def _merge_kernel(x_ref, xn_ref, mod_ref, modn_ref, g_ref, *refs):
    br_refs = refs[0:N_BRANCH]
    wbr_ref = refs[N_BRANCH]
    wg_refs = refs[N_BRANCH + 1:2 * N_BRANCH + 1]
    bg_refs = refs[2 * N_BRANCH + 1:3 * N_BRANCH + 1]
    wo_ref, o_ref, h_ref, hn_ref, acc_ref = refs[3 * N_BRANCH + 1:]
    i = pl.program_id(0)
    j = pl.program_id(1)

    @pl.when(j == 0)
    def _():
        @pl.when(i == 0)
        def _():
            h_ref[...] = _norm_mod(x_ref[...], g_ref[...], mod_ref[0, 0:1, :],
                                   mod_ref[0, 1:2, :]).astype(BF16)

        @pl.when(i > 0)
        def _():
            h_ref[...] = hn_ref[...]

        acc_ref[...] = jnp.zeros_like(acc_ref)

    rows = hn_ref.shape[0] // (D_MODEL // TN_MERGE)
    r0 = pl.multiple_of(j * rows, rows)
    hn_ref[pl.ds(r0, rows), :] = _norm_mod(xn_ref[pl.ds(r0, rows), :], g_ref[...],
                                           modn_ref[0, 0:1, :], modn_ref[0, 1:2, :]).astype(BF16)

    h = h_ref[...]
    mix = None
    for n in range(N_BRANCH):
        gate = _sigmoid(jnp.dot(h, wg_refs[n][...], preferred_element_type=F32) + bg_refs[n][0])
        proj = jnp.dot(br_refs[n][...], wbr_ref[n], preferred_element_type=F32)
        term = gate * proj
        mix = term if mix is None else mix + term
    acc_ref[...] += jnp.dot(mix.astype(BF16), wo_ref[...], preferred_element_type=F32)

    @pl.when(j == pl.num_programs(1) - 1)
    def _():
        o_ref[...] = x_ref[...] + mod_ref[0, 2:3, :] * acc_ref[...]


def _merge(x2d, mod, g, branches, w_branch_bf, w_gate_bf, b_gate, w_out_bf, tokens_per_mod, layer):
    t = x2d.shape[0]
    tm, tn = TM_PROJ, TN_MERGE
    nch = D_MODEL // tn
    br_spec = pl.BlockSpec((tm, BR_W), lambda i, j: (i, 0))
    wg_specs = [pl.BlockSpec((None, D_MODEL, tn),
                             functools.partial(lambda i, j, n: (layer, 0, n * nch + j), n=n))
                for n in range(N_BRANCH)]
    bg_specs = [pl.BlockSpec((1, 1, tn), functools.partial(lambda i, j, n: (n * nch + j, 0, 0), n=n))
                for n in range(N_BRANCH)]
    bg = b_gate.reshape(N_BRANCH * nch, 1, tn)
    n_tiles = t // tm

    def nxt(i):
        return jnp.minimum(i + 1, n_tiles - 1)

    return pl.pallas_call(
        _merge_kernel,
        out_shape=jax.ShapeDtypeStruct((t, D_MODEL), F32),
        grid=(n_tiles, nch),
        in_specs=[
            pl.BlockSpec((tm, D_MODEL), lambda i, j: (i, 0)),
            pl.BlockSpec((tm, D_MODEL), lambda i, j: (nxt(i), 0)),
            pl.BlockSpec((1, 6, D_MODEL), lambda i, j: ((i * tm) // tokens_per_mod, 0, 0)),
            pl.BlockSpec((1, 6, D_MODEL), lambda i, j: ((nxt(i) * tm) // tokens_per_mod, 0, 0)),
            pl.BlockSpec((1, D_MODEL), lambda i, j: (0, 0)),
            br_spec, br_spec, br_spec, br_spec,
            pl.BlockSpec((None, N_BRANCH, BR_W, tn), lambda i, j: (layer, 0, 0, j)),
            *wg_specs, *bg_specs,
            pl.BlockSpec((None, tn, D_MODEL), lambda i, j: (layer, j, 0)),
        ],
        out_specs=pl.BlockSpec((tm, D_MODEL), lambda i, j: (i, 0)),
        scratch_shapes=[pltpu.VMEM((tm, D_MODEL), BF16), pltpu.VMEM((tm, D_MODEL), BF16),
                        pltpu.VMEM((tm, D_MODEL), F32)],
        compiler_params=_cparams(("arbitrary", "arbitrary")),
        name="merge",
    )(x2d, x2d, mod, mod, g.reshape(1, D_MODEL), *branches, w_branch_bf,
      *([w_gate_bf] * N_BRANCH), *([bg] * N_BRANCH), w_out_bf)


def _route(logits):
    lane = lax.broadcasted_iota(jnp.int32, logits.shape, 1)
    ninf = -jnp.inf
    lg = jnp.where(lane < N_GROUPS, logits, ninf)
    mg = lg.max(axis=-1, keepdims=True)
    pg_top = 1.0 / jnp.exp(lg - mg).sum(axis=-1, keepdims=True)
    g_idx = jnp.where(lg == mg, lane, ROUTER_W).min(axis=-1, keepdims=True)
    lo = N_GROUPS + EXP_PER_GROUP * g_idx
    le = jnp.where((lane >= lo) & (lane < lo + EXP_PER_GROUP), logits, ninf)
    m1 = le.max(axis=-1, keepdims=True)
    e1 = jnp.where(le == m1, lane, ROUTER_W).min(axis=-1, keepdims=True)
    le2 = jnp.where(lane == e1, ninf, le)
    m2 = le2.max(axis=-1, keepdims=True)
    e2 = jnp.where(le2 == m2, lane, ROUTER_W).min(axis=-1, keepdims=True)
    se = jnp.exp(le - m1).sum(axis=-1, keepdims=True)
    pe1 = 1.0 / se
    pe2 = jnp.exp(m2 - m1) / se
    tot = pe1 + pe2
    w1 = pe1 / tot * pg_top
    w2 = pe2 / tot * pg_top
    return jnp.where(lane == e1, w1, 0.0) + jnp.where(lane == e2, w2, 0.0), g_idx


def _moe_kernel(x_ref, mod_ref, g_ref, wr_ref, br_ref, wg_ref, wu_ref, wd_ref, fg_ref,
                o_ref, h_ref, comb_ref, grp_ref, rank_ref, cnt_ref, *, final):
    grp = pl.program_id(1)
    tm = x_ref.shape[0]

    @pl.when(grp == 0)
    def _():
        h = _norm_mod(x_ref[...], g_ref[...], mod_ref[0, 3:4, :], mod_ref[0, 4:5, :])
        h_hi = h.astype(BF16)
        h_ref[...] = h_hi
        h_lo = (h - h_hi.astype(F32)).astype(BF16)
        w = wr_ref[...]
        w_hi = w.astype(BF16)
        w_lo = (w - w_hi.astype(F32)).astype(BF16)
        logits = (jnp.dot(h_hi, w_hi, preferred_element_type=F32)
                  + jnp.dot(h_lo, w_hi, preferred_element_type=F32)
                  + jnp.dot(h_hi, w_lo, preferred_element_type=F32)) + br_ref[...]
        comb, g_idx = _route(logits)
        comb_ref[...] = comb
        lane = lax.broadcasted_iota(jnp.int32, (tm, ROUTER_W), 1)
        member = lane == g_idx
        onehot = jnp.where(member, 1.0, 0.0).astype(BF16)
        earlier = jnp.where(lax.broadcasted_iota(jnp.int32, (tm, tm), 0)
                            > lax.broadcasted_iota(jnp.int32, (tm, tm), 1), 1.0, 0.0).astype(BF16)
        before = jnp.dot(earlier, onehot, preferred_element_type=F32)
        rank_ref[...] = jnp.where(member, before, 0.0).sum(axis=-1, keepdims=True).astype(jnp.int32)
        grp_ref[...] = g_idx
        cnt_ref[...] = jnp.where(member, 1.0, 0.0).sum(axis=0, keepdims=True)
        o_ref[...] = jnp.zeros_like(o_ref)

    lane_r = lax.broadcasted_iota(jnp.int32, (1, ROUTER_W), 1)
    count = jnp.where(lane_r == grp, cnt_ref[...], 0.0).sum().astype(jnp.int32)
    in_group = grp_ref[...] == grp
    rank = rank_ref[...]
    comb = comb_ref[...]
    comb_hi = comb.astype(BF16)
    comb_lo = (comb - comb_hi.astype(F32)).astype(BF16)
    slot_lane = lax.broadcasted_iota(jnp.int32, (tm, MOE_CHUNK), 1)
    lane_c = lax.broadcasted_iota(jnp.int32, (MOE_CHUNK, ROUTER_W), 1)

    def chunk(c, carry):
        sel = jnp.where(in_group & (rank - c * MOE_CHUNK == slot_lane), 1.0, 0.0).astype(BF16)
        xc = lax.dot_general(sel, h_ref[...], _TN, preferred_element_type=F32).astype(BF16)
        cwc = (lax.dot_general(sel, comb_hi, _TN, preferred_element_type=F32)
               + lax.dot_general(sel, comb_lo, _TN, preferred_element_type=F32))
        out = None
        for e in range(EXP_PER_GROUP):
            cw = jnp.where(lane_c == N_GROUPS + EXP_PER_GROUP * grp + e, cwc, 0.0).sum(
                axis=-1, keepdims=True)
            gate = jnp.dot(xc, wg_ref[e], preferred_element_type=F32)
            up = jnp.dot(xc, wu_ref[e], preferred_element_type=F32)
            hid = (gate * _sigmoid(gate)) * up * cw
            term = jnp.dot(hid.astype(BF16), wd_ref[grp, e], preferred_element_type=F32)
            out = term if out is None else out + term
        o_ref[...] += jnp.dot(sel, out.astype(BF16), preferred_element_type=F32)
        return carry

    lax.fori_loop(0, (count + MOE_CHUNK - 1) // MOE_CHUNK, chunk, 0)

    @pl.when(grp == pl.num_programs(1) - 1)
    def _():
        y = x_ref[...] + mod_ref[0, 5:6, :] * o_ref[...]
        if final:
            ms = jnp.mean(y * y, axis=-1, keepdims=True)
            y = y * lax.rsqrt(ms + EPS) * fg_ref[...]
        o_ref[...] = y


def _moe(x2d, mod, g, w_router, b_router, w_g_bf, w_u_bf, w_d_bf, final_g, tokens_per_mod, final,
         layer):
    t = x2d.shape[0]
    tm = TM_PROJ
    w_spec_in = pl.BlockSpec((None, None, EXP_PER_GROUP, D_MODEL, D_EXPERT),
                             lambda i, grp: (layer, grp, 0, 0, 0))
    return pl.pallas_call(
        functools.partial(_moe_kernel, final=final),
        out_shape=jax.ShapeDtypeStruct((t, D_MODEL), F32),
        grid=(t // tm, N_GROUPS),
        in_specs=[
            pl.BlockSpec((tm, D_MODEL), lambda i, grp: (i, 0)),
            pl.BlockSpec((1, 6, D_MODEL), lambda i, grp: ((i * tm) // tokens_per_mod, 0, 0)),
            pl.BlockSpec((1, D_MODEL), lambda i, grp: (0, 0)),
            pl.BlockSpec((D_MODEL, ROUTER_W), lambda i, grp: (0, 0)),
            pl.BlockSpec((1, ROUTER_W), lambda i, grp: (0, 0)),
            w_spec_in, w_spec_in,
            pl.BlockSpec((None, N_GROUPS, EXP_PER_GROUP, D_EXPERT, D_MODEL),
                         lambda i, grp: (layer, 0, 0, 0, 0), pipeline_mode=pl.Buffered(1)),
            pl.BlockSpec((1, D_MODEL), lambda i, grp: (0, 0)),
        ],
        out_specs=pl.BlockSpec((tm, D_MODEL), lambda i, grp: (i, 0)),
        scratch_shapes=[pltpu.VMEM((tm, D_MODEL), BF16), pltpu.VMEM((tm, ROUTER_W), F32),
                        pltpu.VMEM((tm, 1), jnp.int32), pltpu.VMEM((tm, 1), jnp.int32),
                        pltpu.VMEM((1, ROUTER_W), F32)],
        compiler_params=_cparams(("parallel", "arbitrary"), VMEM_LIMIT_MOE),
        name="moe_final" if final else "moe",
    )(x2d, mod, g.reshape(1, D_MODEL), w_router, b_router, w_g_bf, w_u_bf, w_d_bf,
      final_g.reshape(1, D_MODEL))


def _rope_tables(seq):
    nf = HEAD_DIM // 4
    t = jnp.arange(seq)
    inv = ROPE_BASE ** (-jnp.arange(nf, dtype=F32) / nf)
    pos = jnp.stack([t // GRID_W, t % GRID_W], -1).astype(F32)
    ang = pos[:, :, None] * inv
    cos, sin = jnp.cos(ang), jnp.sin(ang)
    cos64 = jnp.concatenate([cos[:, 0], cos[:, 0], cos[:, 1], cos[:, 1]], axis=-1)
    sin64 = jnp.concatenate([-sin[:, 0], sin[:, 0], -sin[:, 1], sin[:, 1]], axis=-1)
    return jnp.tile(cos64, (1, 2)), jnp.tile(sin64, (1, 2))


def _seq_minor_view(cache):
    nb, depth, seq = cache.shape[:3]
    nd = cache.ndim
    return jnp.transpose(cache, (0, 1, *range(3, nd), 2)).reshape(nb, depth, -1, seq)


def _seq_major_view(arr, head_dims):
    nb, depth, _, seq = arr.shape
    nd = 3 + len(head_dims)
    arr = arr.reshape(nb, depth, *head_dims, seq)
    return jnp.transpose(arr, (0, 1, nd - 1, *range(2, nd - 1)))


def kernel(x_prompt, x_sample, cache_attn_k, cache_attn_v, cache_diff_k, cache_diff_v, cache_na_k, cache_na_v, c, c_ctx, w_mod, b_mod, norm1_g, w_in, attn_sink, diff_lambda, diff_sub_g, na_rpb, w_branch, w_gate, b_gate, w_out, norm2_g, w_router_group, b_router_group, w_router_expert, b_router_expert, w_exp_gate, w_exp_up, w_exp_down, final_g):
    nbp, seq_p, _ = x_prompt.shape
    nbs, seq_s, _ = x_sample.shape
    xp = x_prompt.reshape(nbp * seq_p, D_MODEL)
    xs = x_sample.reshape(nbs * seq_s, D_MODEL)

    cond = jnp.concatenate([c_ctx[None, :], c], axis=0)
    n_cond = -(-cond.shape[0] // 8) * 8
    cond = jnp.pad(cond, ((0, n_cond - cond.shape[0]), (0, 0)))
    mods = _modulation(cond, w_mod, b_mod).reshape(DEPTH, n_cond, 6, D_MODEL)

    rope = _rope_tables(seq_s)
    cache_a = (_seq_minor_view(cache_attn_k), _seq_minor_view(cache_attn_v))
    cache_b = (_seq_minor_view(cache_diff_k), cache_diff_v)
    cache_c = (_seq_minor_view(cache_na_k), _seq_minor_view(cache_na_v))

    w_in_bf = w_in.astype(BF16)
    w_branch_bf = w_branch.astype(BF16)
    w_gate_bf = w_gate.astype(BF16)
    w_out_bf = w_out.astype(BF16)
    grouped = (DEPTH, N_GROUPS, EXP_PER_GROUP)
    w_eg_bf = w_exp_gate.astype(BF16).reshape(*grouped, D_MODEL, D_EXPERT)
    w_eu_bf = w_exp_up.astype(BF16).reshape(*grouped, D_MODEL, D_EXPERT)
    w_ed_bf = w_exp_down.astype(BF16).reshape(*grouped, D_EXPERT, D_MODEL)
    pad = ROUTER_W - N_GROUPS - N_EXPERTS
    w_router = jnp.concatenate(
        [w_router_group, w_router_expert, jnp.zeros((DEPTH, D_MODEL, pad), F32)], axis=-1)
    b_router = jnp.concatenate(
        [b_router_group, b_router_expert, jnp.zeros((DEPTH, pad), F32)], axis=-1)

    kv = None
    for l in range(DEPTH):
        lam_init = 0.8 - 0.6 * math.exp(-0.3 * l)
        mod_p = mods[l, 0:1]
        mod_s = mods[l, 1:1 + nbs]
        final = l == DEPTH - 1
        moe_w = (w_router[l], b_router[l].reshape(1, ROUTER_W), w_eg_bf, w_eu_bf, w_ed_bf)
        merge_w = (w_branch_bf, w_gate_bf, b_gate[l], w_out_bf)

        pr = _inproj(xp, mod_p, norm1_g[l], w_in_bf, xp.shape[0], seq=seq_p, layer=l, prev_kv=kv)
        kv = {name: pr[name] for name in _CTX_F32}
        o_a = _sink_attn(attn_sink[l], pr["a_q"], (kv["a_k"], kv["a_v"]), nb=nbp, seq=seq_p, layer=l)
        o_b = _diff_attn(diff_lambda[l], diff_sub_g[l], pr["b_q"], (kv["b_k"], kv["b_v"]),
                         nb=nbp, seq=seq_p, lam_init=lam_init, layer=l)
        o_c = _dense_attn(pr["c_q"], (kv["c_k"], kv["c_v"]), nb=nbp, seq=seq_p, layer=l)
        o_f = _fourier(pr["f"], nb=nbp, seq=seq_p)
        xp = _merge(xp, mod_p, norm1_g[l], (o_a, o_b, o_c, o_f), *merge_w, xp.shape[0], l)
        xp = _moe(xp, mod_p, norm2_g[l], *moe_w, final_g, xp.shape[0], final, l)

        pr = _inproj(xs, mod_s, norm1_g[l], w_in_bf, seq_s, rope=rope, layer=l)
        o_a = _sink_attn(attn_sink[l], pr["a_q"], cache_a, nb=nbs, seq=seq_s, layer=l,
                         kv_loc=(pr["a_k"], pr["a_v"]))
        o_b = _diff_attn(diff_lambda[l], diff_sub_g[l], pr["b_q"], cache_b, nb=nbs, seq=seq_s,
                         lam_init=lam_init, layer=l, kv_loc=(pr["b_k"], pr["b_v"]))
        o_c = _na_attn(pr["c_q"], (pr["c_k"], pr["c_v"]), cache_c,
                       _na_rpb_table(na_rpb[l]), nb=nbs, seq=seq_s, layer=l)
        o_f = _fourier(pr["f"], nb=nbs, seq=seq_s)
        xs = _merge(xs, mod_s, norm1_g[l], (o_a, o_b, o_c, o_f), *merge_w, seq_s, l)
        xs = _moe(xs, mod_s, norm2_g[l], *moe_w, final_g, seq_s, final, l)

    return (
        xp.reshape(x_prompt.shape),
        xs.reshape(x_sample.shape),
        _seq_major_view(kv["a_k"], (A_KV_HEADS, HEAD_DIM)),
        _seq_major_view(kv["a_v"], (A_KV_HEADS, HEAD_DIM)),
        _seq_major_view(kv["b_k"], (DIFF_HEADS, 2, HEAD_DIM)),
        kv["b_v"],
        _seq_major_view(kv["c_k"], (NA_HEADS, HEAD_DIM)),
        _seq_major_view(kv["c_v"], (NA_HEADS, HEAD_DIM)),
    )
```

```python
import functools
import math

import jax
import jax.numpy as jnp
import numpy as np
from jax import lax
from jax.experimental import pallas as pl
from jax.experimental.pallas import tpu as pltpu

F32 = jnp.float32
BF16 = jnp.bfloat16

D_MODEL = 2048
DEPTH = 2
GRID_W = 64
HEAD_DIM = 64
A_HEADS = 8
A_KV_HEADS = 2
A_GROUP = A_HEADS // A_KV_HEADS
WIN = 128
DIFF_HEADS = 4
DIFF_V_DIM = 2 * HEAD_DIM
NA_HEADS = 8
NA_ROWS = 8
NA_COLS = 16
F_GROUPS = 4
F_GROUP_W = 128
N_BRANCH = 4
BR_W = A_HEADS * HEAD_DIM
N_GROUPS = 4
EXP_PER_GROUP = 4
N_EXPERTS = N_GROUPS * EXP_PER_GROUP
D_EXPERT = 256
ROPE_BASE = 10000.0
EPS = 1e-6
NEG = -1e30
SCALE = HEAD_DIM ** -0.5

_IN_GROUPS = (
    ("a_q", 0, 512), ("a_k", 512, 128), ("a_v", 640, 128),
    ("b_q", 768, 512), ("b_k", 1280, 512), ("b_v", 1792, 512),
    ("c_q", 2304, 512), ("c_k", 2816, 512), ("c_v", 3328, 512),
    ("f", 3840, 512),
)
IN_W = 4352
_Q_GROUPS = ("a_q", "b_q", "c_q")
_ROPED = ("a_q", "a_k", "b_q", "b_k")
_CTX_F32 = ("a_k", "a_v", "b_k", "b_v", "c_k", "c_v")
_CTX_SEQ_MINOR = ("a_k", "a_v", "b_k", "c_k", "c_v")

LANES = 128
VMEM_LIMIT = 56 * 1024 * 1024
VMEM_LIMIT_MOE = 59 * 1024 * 1024

TM_PROJ = 512
TQ_SINK = 256
TQ_DIFF = 512
NA_QROWS = 4
NA_BAND = NA_ROWS + NA_QROWS - 1
TN_MERGE = 256
TN_MOD = 1024
ROUTER_W = 128
MOE_CHUNK = 128

_NT = (((1,), (1,)), ((), ()))
_TN = (((0,), (0,)), ((), ()))


def _cparams(sem, vmem_limit=VMEM_LIMIT):
    return pltpu.CompilerParams(dimension_semantics=sem, vmem_limit_bytes=vmem_limit)


def _sigmoid(x):
    return 1.0 / (1.0 + jnp.exp(-x))


def _norm_mod(x, g, shift, scale):
    ms = jnp.mean(x * x, axis=-1, keepdims=True)
    return (x * lax.rsqrt(ms + EPS) * g) * (1.0 + scale) + shift


def _mod_kernel(c_ref, w_ref, b_ref, o_ref):
    c = c_ref[...]
    s = (c * _sigmoid(c)).astype(BF16)
    o_ref[0] = jnp.dot(s, w_ref[0].astype(BF16), preferred_element_type=F32) + b_ref[0]


def _modulation(cond, w_mod, b_mod):
    n = cond.shape[0]
    out_w = w_mod.shape[-1]
    return pl.pallas_call(
        _mod_kernel,
        out_shape=jax.ShapeDtypeStruct((DEPTH, n, out_w), F32),
        grid=(DEPTH, out_w // TN_MOD),
        in_specs=[
            pl.BlockSpec((n, D_MODEL), lambda l, j: (0, 0)),
            pl.BlockSpec((1, D_MODEL, TN_MOD), lambda l, j: (l, 0, j)),
            pl.BlockSpec((1, 1, TN_MOD), lambda l, j: (l, 0, j)),
        ],
        out_specs=pl.BlockSpec((1, n, TN_MOD), lambda l, j: (l, 0, j)),
        compiler_params=_cparams(("parallel", "parallel")),
        name="modulation",
    )(cond, w_mod, b_mod.reshape(DEPTH, 1, out_w))


def _rope128(y, cos, sin):
    lane = lax.broadcasted_iota(jnp.int32, y.shape, 1)
    first = (lane % 32) < 16
    partner = jnp.where(first, pltpu.roll(y, LANES - 16, 1), pltpu.roll(y, 16, 1))
    return y * cos + partner * sin


def _inproj_kernel(*refs, latent, n_alias, seq, layer, fill_depth):
    if latent:
        x_ref, mod_ref, g_ref, w_ref, cos_ref, sin_ref = refs[:6]
        out_refs = refs[6:]
    else:
        x_ref, mod_ref, g_ref, w_ref = refs[:4]
        out_refs = refs[4 + n_alias:]
    h = _norm_mod(x_ref[...], g_ref[...], mod_ref[0, 0:1, :], mod_ref[0, 1:2, :]).astype(BF16)
    for (name, start, width), o_ref in zip(_IN_GROUPS, out_refs):
        y = jnp.dot(h, w_ref[:, start:start + width], preferred_element_type=F32)
        if name in _Q_GROUPS:
            y = y * SCALE
        if latent and name in _ROPED:
            cos = cos_ref[...]
            sin = sin_ref[...]
            for c0 in range(0, width, LANES):
                o_ref[:, c0:c0 + LANES] = _rope128(y[:, c0:c0 + LANES], cos, sin).astype(o_ref.dtype)
        elif not latent and name in _CTX_F32:
            for bb in range(o_ref.shape[0]):
                yb = y[bb * seq:(bb + 1) * seq]
                dst = o_ref.at[bb, layer] if fill_depth else o_ref.at[bb]
                if name in _CTX_SEQ_MINOR:
                    dst[...] = yb.T
                else:
                    for hd in range(DIFF_HEADS):
                        dst[:, hd, :] = _head(yb, hd, DIFF_V_DIM)
                if fill_depth:
                    for other in range(DEPTH):
                        if other != layer:
                            o_ref[bb, other] = jnp.zeros(o_ref.shape[2:], F32)
        else:
            o_ref[...] = y.astype(o_ref.dtype)


def _inproj(x2d, mod, g, w_in_bf, tokens_per_mod, *, rope=None, seq=None, layer=0, prev_kv=None):
    t = x2d.shape[0]
    tm = TM_PROJ
    latent = rope is not None
    fill_depth = not latent and prev_kv is None
    in_specs = [
        pl.BlockSpec((tm, D_MODEL), lambda i: (i, 0)),
        pl.BlockSpec((1, 6, D_MODEL), lambda i: ((i * tm) // tokens_per_mod, 0, 0)),
        pl.BlockSpec((1, D_MODEL), lambda i: (0, 0)),
        pl.BlockSpec((None, D_MODEL, IN_W), lambda i: (layer, 0, 0), pipeline_mode=pl.Buffered(1)),
    ]
    args = [x2d, mod, g.reshape(1, D_MODEL), w_in_bf]
    aliases = {}
    n_alias = 0
    if latent:
        nblk = rope[0].shape[0] // tm
        in_specs += [pl.BlockSpec((tm, LANES), lambda i: (i % nblk, 0))] * 2
        args += list(rope)
    elif prev_kv is not None:
        n_alias = len(_CTX_F32)
        in_specs += [pl.BlockSpec(memory_space=pl.ANY)] * n_alias
        args += [prev_kv[name] for name in _CTX_F32]
    out_shape, out_specs = [], []
    tb = tm // seq if not latent else None
    for oi, (name, _, width) in enumerate(_IN_GROUPS):
        if not latent and name in _CTX_F32:
            blk = (width, seq) if name in _CTX_SEQ_MINOR else (seq, DIFF_HEADS, DIFF_V_DIM)
            out_shape.append(jax.ShapeDtypeStruct((t // seq, DEPTH) + blk, F32))
            if fill_depth:
                out_specs.append(pl.BlockSpec((tb, DEPTH) + blk,
                                              lambda i, nd=len(blk): (i, 0) + (0,) * nd))
            else:
                out_specs.append(pl.BlockSpec((tb, None) + blk,
                                              lambda i, nd=len(blk): (i, layer) + (0,) * nd))
                aliases[4 + _CTX_F32.index(name)] = oi
        else:
            out_shape.append(jax.ShapeDtypeStruct((t, width), BF16))
            out_specs.append(pl.BlockSpec((tm, width), lambda i: (i, 0)))
    outs = pl.pallas_call(
        functools.partial(_inproj_kernel, latent=latent, n_alias=n_alias, seq=seq, layer=layer,
                          fill_depth=fill_depth),
        out_shape=out_shape,
        grid=(t // tm,),
        in_specs=in_specs,
        out_specs=out_specs,
        input_output_aliases=aliases,
        compiler_params=_cparams(("parallel",)),
        name="inproj_latent" if latent else "inproj_context",
    )(*args)
    return dict(zip([n for n, _, _ in _IN_GROUPS], outs))


def _attend(q, segs, sink_col=None):
    scores = []
    for k, _, bias, k_seq_minor, _ in segs:
        if k_seq_minor:
            s = jnp.dot(q, k, preferred_element_type=F32)
        else:
            s = lax.dot_general(q, k, _NT, preferred_element_type=F32)
        if bias is not None:
            s = s + bias
        scores.append(s)
    m = scores[0].max(axis=-1, keepdims=True)
    for s in scores[1:]:
        m = jnp.maximum(m, s.max(axis=-1, keepdims=True))
    if sink_col is not None:
        m = jnp.maximum(m, sink_col)
    acc = None
    den = None
    for s, (_, v, _, _, v_seq_minor) in zip(scores, segs):
        p = jnp.exp(s - m)
        ps = p.sum(axis=-1, keepdims=True)
        if v_seq_minor:
            o = lax.dot_general(p.astype(BF16), v, _NT, preferred_element_type=F32)
        else:
            o = jnp.dot(p.astype(BF16), v, preferred_element_type=F32)
        acc = o if acc is None else acc + o
        den = ps if den is None else den + ps
    if sink_col is not None:
        den = den + jnp.exp(sink_col - m)
    return acc, den


def _head(x, h, width=HEAD_DIM):
    return x[:, h * width:(h + 1) * width]


def _head_rows(x, h, width=HEAD_DIM):
    return x[h * width:(h + 1) * width, :]


def _pair(x, i):
    return x[:, i * LANES:(i + 1) * LANES]


def _pair_rows(x, i):
    return x[i * LANES:(i + 1) * LANES, :]


def _keep_half(x, half, axis):
    idx = lax.broadcasted_iota(jnp.int32, x.shape, axis)
    return jnp.where((idx // HEAD_DIM) == half, x, jnp.zeros_like(x))


def _place_rows(x, half):
    z = jnp.zeros_like(x)
    return jnp.concatenate([x, z] if half == 0 else [z, x], axis=0)


def _kv_spec(width, seq, layer):
    return pl.BlockSpec((None, None, width, seq), lambda b, i: (b, layer, 0, 0))


def _sink_attn_kernel(*refs, latent, tq, seq):
    if latent:
        sink_ref, q_ref, k_ref, v_ref, kc_ref, vc_ref, o_ref = refs
    else:
        sink_ref, q_ref, kc_ref, vc_ref, o_ref = refs
    q_all = q_ref[...]
    k_ctx = kc_ref[...].astype(BF16)
    v_ctx = vc_ref[...].astype(BF16)
    if latent:
        band = tq + 2 * WIN
        qb = pl.program_id(1)
        start = pl.multiple_of(jnp.clip(qb * tq - WIN, 0, seq - band), WIN)
        k_loc = k_ref[pl.ds(start, band), :]
        v_loc = v_ref[pl.ds(start, band), :]
        k_swp = jnp.concatenate([_head(k_loc, 1), _head(k_loc, 0)], axis=1)
        v_swp = jnp.concatenate([_head(v_loc, 1), _head(v_loc, 0)], axis=1)
        qpos = qb * tq + lax.broadcasted_iota(jnp.int32, (tq, band), 0)
        kpos = start + lax.broadcasted_iota(jnp.int32, (tq, band), 1)
        bias = jnp.where(jnp.abs(qpos - kpos) <= WIN, 0.0, NEG).astype(F32)
    outs = []
    for i in range(A_HEADS // 2):
        kv = (2 * i) // A_GROUP
        o_pair = None
        for half in range(2):
            h = 2 * i + half
            sink_col = jnp.full((tq, 1), sink_ref[0, h], F32)
            segs = [(_place_rows(_head_rows(k_ctx, kv), half),
                     _place_rows(_head_rows(v_ctx, kv), half), None, True, True)]
            if latent:
                k_src, v_src = (k_loc, v_loc) if kv == half else (k_swp, v_swp)
                segs.append((_keep_half(k_src, half, 1), _keep_half(v_src, half, 1),
                             bias, False, False))
            acc, den = _attend(_pair(q_all, i), segs, sink_col)
            o = acc / den
            o_pair = o if o_pair is None else o_pair + o
        outs.append(o_pair)
    o_ref[...] = jnp.concatenate(outs, axis=1).astype(o_ref.dtype)


def _sink_attn(sink, q, kv_ctx, *, nb, seq, layer, kv_loc=None):
    latent = kv_loc is not None
    tq = TQ_SINK if latent else seq
    nq = seq // tq
    kw = A_KV_HEADS * HEAD_DIM
    s_ctx = kv_ctx[0].shape[-1]
    in_specs = [
        pl.BlockSpec(memory_space=pltpu.SMEM),
        pl.BlockSpec((tq, BR_W), lambda b, i: (b * nq + i, 0)),
    ]
    args = [sink.reshape(1, A_HEADS), q]
    if latent:
        in_specs += [pl.BlockSpec((seq, kw), lambda b, i: (b, 0))] * 2
        args += list(kv_loc)
    in_specs += [_kv_spec(kw, s_ctx, layer)] * 2
    args += list(kv_ctx)
    return pl.pallas_call(
        functools.partial(_sink_attn_kernel, latent=latent, tq=tq, seq=seq),
        out_shape=jax.ShapeDtypeStruct(q.shape, BF16),
        grid=(nb, nq),
        in_specs=in_specs,
        out_specs=pl.BlockSpec((tq, BR_W), lambda b, i: (b * nq + i, 0)),
        compiler_params=_cparams(("parallel", "arbitrary")),
        name="sink_attn_latent" if latent else "sink_attn_context",
    )(*args)


def _diff_attn_kernel(*refs, latent, lam_init):
    if latent:
        lam_ref, g_ref, q_ref, k_ref, v_ref, kc_ref, vc_ref, o_ref = refs
    else:
        lam_ref, g_ref, q_ref, kc_ref, vc_ref, o_ref = refs
    lam = lam_ref[...]
    lam_a = jnp.sum(lam[0:1] * lam[1:2], axis=-1, keepdims=True)
    lam_b = jnp.sum(lam[2:3] * lam[3:4], axis=-1, keepdims=True)
    lam_full = jnp.exp(lam_a) - jnp.exp(lam_b) + lam_init
    q_all = q_ref[...]
    k_ctx = kc_ref[...].astype(BF16)
    if latent:
        k_loc = k_ref[...]
        v_loc = v_ref[...]
    outs = []
    for h in range(DIFF_HEADS):
        v_ctx = vc_ref[:, h, :].astype(BF16)
        res = []
        for sub in range(2):
            segs = [(_keep_half(_pair_rows(k_ctx, h), sub, 0), v_ctx, None, True, False)]
            if latent:
                segs.append((_keep_half(_pair(k_loc, h), sub, 1), _pair(v_loc, h),
                             None, False, False))
            acc, den = _attend(_pair(q_all, h), segs)
            res.append(acc / den)
        o = res[0] - lam_full * res[1]
        ms = jnp.mean(o * o, axis=-1, keepdims=True)
        outs.append(o * lax.rsqrt(ms + EPS) * g_ref[...] * (1.0 - lam_init))
    o_ref[...] = jnp.concatenate(outs, axis=1).astype(o_ref.dtype)


def _diff_attn(lam, sub_g, q, kv_ctx, *, nb, seq, lam_init, layer, kv_loc=None):
    latent = kv_loc is not None
    tq = TQ_DIFF if latent else seq
    nq = seq // tq
    kw = DIFF_HEADS * 2 * HEAD_DIM
    s_ctx = kv_ctx[0].shape[-1]
    in_specs = [
        pl.BlockSpec((4, HEAD_DIM), lambda b, i: (0, 0)),
        pl.BlockSpec((1, DIFF_V_DIM), lambda b, i: (0, 0)),
        pl.BlockSpec((tq, BR_W), lambda b, i: (b * nq + i, 0)),
    ]
    args = [lam, sub_g.reshape(1, DIFF_V_DIM), q]
    if latent:
        in_specs += [pl.BlockSpec((seq, kw), lambda b, i: (b, 0))] * 2
        args += list(kv_loc)
    in_specs += [_kv_spec(kw, s_ctx, layer),
                 pl.BlockSpec((None, None, s_ctx, DIFF_HEADS, DIFF_V_DIM),
                              lambda b, i: (b, layer, 0, 0, 0))]
    args += list(kv_ctx)
    return pl.pallas_call(
        functools.partial(_diff_attn_kernel, latent=latent, lam_init=lam_init),
        out_shape=jax.ShapeDtypeStruct(q.shape, BF16),
        grid=(nb, nq),
        in_specs=in_specs,
        out_specs=pl.BlockSpec((tq, BR_W), lambda b, i: (b * nq + i, 0)),
        compiler_params=_cparams(("parallel", "arbitrary")),
        name="diff_attn_latent" if latent else "diff_attn_context",
    )(*args)


def _dense_attn_kernel(q_ref, k_ref, v_ref, o_ref):
    q_all = q_ref[...]
    k = k_ref[...].astype(BF16)
    v = v_ref[...].astype(BF16)
    outs = []
    for i in range(NA_HEADS // 2):
        o_pair = None
        for half in range(2):
            segs = [(_keep_half(_pair_rows(k, i), half, 0), _keep_half(_pair_rows(v, i), half, 0),
                     None, True, True)]
            acc, den = _attend(_pair(q_all, i), segs)
            o = acc / den
            o_pair = o if o_pair is None else o_pair + o
        outs.append(o_pair)
    o_ref[...] = jnp.concatenate(outs, axis=1).astype(o_ref.dtype)


def _dense_attn(q, kv_ctx, *, nb, seq, layer):
    return pl.pallas_call(
        _dense_attn_kernel,
        out_shape=jax.ShapeDtypeStruct(q.shape, BF16),
        grid=(nb, 1),
        in_specs=[pl.BlockSpec((seq, BR_W), lambda b, i: (b, 0)),
                  _kv_spec(BR_W, seq, layer), _kv_spec(BR_W, seq, layer)],
        out_specs=pl.BlockSpec((seq, BR_W), lambda b, i: (b, 0)),
        compiler_params=_cparams(("parallel", "arbitrary")),
        name="dense_attn_context",
    )(q, *kv_ctx)


def _na_window_start(r, rows):
    return jnp.clip(r - NA_ROWS // 2, 0, rows - NA_ROWS)


def _na_band_start(step, rows):
    return jnp.clip(step * NA_QROWS - NA_ROWS // 2, 0, rows - NA_BAND)


def _na_attn_kernel(q_ref, k_ref, v_ref, kc_ref, vc_ref, tab_ref, o_ref, bias_ref, *, rows):
    step = pl.program_id(0)
    band_row = _na_band_start(step, rows)

    @pl.when(pl.program_id(1) == 0)
    def _():
        for ri in range(NA_QROWS):
            r = step * NA_QROWS + ri
            rs = _na_window_start(r, rows)
            for h in range(NA_HEADS):
                blocks = []
                for j in range(NA_BAND):
                    kr = band_row + j
                    ok = (kr >= rs) & (kr < rs + NA_ROWS)
                    dr = jnp.clip(kr - r + NA_ROWS - 1, 0, 2 * NA_ROWS - 2)
                    blocks.append(jnp.where(ok, tab_ref[h, dr], NEG))
                bias_ref[h, ri * GRID_W:(ri + 1) * GRID_W, :] = jnp.concatenate(blocks, axis=1)

    start = pl.multiple_of(band_row * GRID_W, GRID_W)
    band = NA_BAND * GRID_W
    q_all = q_ref[...]
    k_loc = k_ref[pl.ds(start, band), :]
    v_loc = v_ref[pl.ds(start, band), :]
    k_ctx = kc_ref[...].astype(BF16)
    v_ctx = vc_ref[...].astype(BF16)
    outs = []
    for i in range(NA_HEADS // 2):
        o_pair = None
        for half in range(2):
            segs = [(_keep_half(_pair_rows(k_ctx, i), half, 0),
                     _keep_half(_pair_rows(v_ctx, i), half, 0), None, True, True),
                    (_keep_half(_pair(k_loc, i), half, 1), _keep_half(_pair(v_loc, i), half, 1),
                     bias_ref[2 * i + half], False, False)]
            acc, den = _attend(_pair(q_all, i), segs)
            o = acc / den
            o_pair = o if o_pair is None else o_pair + o
        outs.append(o_pair)
    o_ref[...] = jnp.concatenate(outs, axis=1).astype(o_ref.dtype)


def _na_attn(q, kv_loc, kv_ctx, rpb_tab, *, nb, seq, layer):
    rows = seq // GRID_W
    nsteps = rows // NA_QROWS
    tq = NA_QROWS * GRID_W
    band = NA_BAND * GRID_W
    past = kv_ctx[0].shape[-1]
    for s in range(nsteps):
        us = min(max(s * NA_QROWS - NA_ROWS // 2, 0), rows - NA_BAND)
        for r in range(s * NA_QROWS, (s + 1) * NA_QROWS):
            rs = min(max(r - NA_ROWS // 2, 0), rows - NA_ROWS)
            assert us <= rs and rs + NA_ROWS <= us + NA_BAND
    ctx_spec = pl.BlockSpec((None, None, BR_W, past), lambda s, b: (b, layer, 0, 0))
    return pl.pallas_call(
        functools.partial(_na_attn_kernel, rows=rows),
        out_shape=jax.ShapeDtypeStruct(q.shape, BF16),
        grid=(nsteps, nb),
        in_specs=[
            pl.BlockSpec((tq, BR_W), lambda s, b: (b * nsteps + s, 0)),
            pl.BlockSpec((seq, BR_W), lambda s, b: (b, 0)),
            pl.BlockSpec((seq, BR_W), lambda s, b: (b, 0)),
            ctx_spec, ctx_spec,
            pl.BlockSpec(rpb_tab.shape, lambda s, b: (0, 0, 0, 0)),
        ],
        out_specs=pl.BlockSpec((tq, BR_W), lambda s, b: (b * nsteps + s, 0)),
        scratch_shapes=[pltpu.VMEM((NA_HEADS, tq, band), F32)],
        compiler_params=_cparams(("arbitrary", "arbitrary")),
        name="na_attn_latent",
    )(q, *kv_loc, *kv_ctx, rpb_tab)


def _rpb_expand_kernel(rpb_ref, onehot_ref, o_ref):
    o_ref[...] = jnp.dot(rpb_ref[...], onehot_ref[...], preferred_element_type=F32,
                         precision=lax.Precision.HIGHEST)


def _na_rpb_table(rpb):
    n_dr, n_dc = 2 * NA_ROWS - 1, 2 * NA_COLS - 1
    qcol = np.arange(GRID_W)
    kcol = np.arange(GRID_W)
    cs = np.clip(qcol - NA_COLS // 2, 0, GRID_W - NA_COLS)
    col_ok = (kcol[None, :] >= cs[:, None]) & (kcol[None, :] < cs[:, None] + NA_COLS)
    dc = np.clip(kcol[None, :] - qcol[:, None] + NA_COLS - 1, 0, n_dc - 1)
    onehot = np.zeros((32, GRID_W * GRID_W), np.float32)
    onehot[dc.reshape(-1), np.arange(GRID_W * GRID_W)] = 1.0
    rpb2 = jnp.pad(rpb.reshape(NA_HEADS * n_dr, n_dc), ((0, 0), (0, 32 - n_dc)))
    full = pl.pallas_call(
        _rpb_expand_kernel,
        out_shape=jax.ShapeDtypeStruct((NA_HEADS * n_dr, GRID_W * GRID_W), F32),
        name="rpb_expand",
    )(rpb2, jnp.asarray(onehot))
    full = full.reshape(NA_HEADS, n_dr, GRID_W, GRID_W)
    return full + jnp.asarray(np.where(col_ok, 0.0, NEG).astype(np.float32))


def _dft_tables(seq):
    pos = np.arange(seq)
    ang = 2.0 * np.pi * ((pos[:, None] * pos[None, :]) % seq) / seq
    left = np.concatenate([np.cos(ang), -np.sin(ang)], axis=1).astype(np.float32)
    ch = np.arange(F_GROUP_W)
    angw = 2.0 * np.pi * ((ch[:, None] * ch[None, :]) % F_GROUP_W) / F_GROUP_W
    right = np.concatenate([np.cos(angw), np.sin(angw)], axis=1).astype(np.float32)
    return jnp.asarray(left).astype(BF16), jnp.asarray(right).astype(BF16)


def _fourier_kernel(f_ref, left_ref, right_ref, o_ref, stack_ref, *, seq):
    f = f_ref[...]
    right = right_ref[...]
    for g in range(F_GROUPS):
        t = jnp.dot(_head(f, g, F_GROUP_W), right, preferred_element_type=F32).astype(BF16)
        stack_ref[0:seq, g * F_GROUP_W:(g + 1) * F_GROUP_W] = t[:, :F_GROUP_W]
        stack_ref[seq:2 * seq, g * F_GROUP_W:(g + 1) * F_GROUP_W] = t[:, F_GROUP_W:]
    y = jnp.dot(left_ref[...], stack_ref[...], preferred_element_type=F32)
    o_ref[...] = (y * (1.0 / math.sqrt(seq * F_GROUP_W))).astype(o_ref.dtype)


def _fourier(f, *, nb, seq):
    left, right = _dft_tables(seq)
    w = F_GROUPS * F_GROUP_W
    return pl.pallas_call(
        functools.partial(_fourier_kernel, seq=seq),
        out_shape=jax.ShapeDtypeStruct(f.shape, BF16),
        grid=(nb,),
        in_specs=[
            pl.BlockSpec((seq, w), lambda b: (b, 0)),
            pl.BlockSpec((seq, 2 * seq), lambda b: (0, 0)),
            pl.BlockSpec((F_GROUP_W, 2 * F_GROUP_W), lambda b: (0, 0)),
        ],
        out_specs=pl.BlockSpec((seq, w), lambda b: (b, 0)),
        scratch_shapes=[pltpu.VMEM((2 * seq, w), BF16)],
        compiler_params=_cparams(("parallel",)),
        name=f"fourier_{seq}",
    )(f, left, right)


def _merge_kernel(x_ref, mod_ref, g_ref, *refs):
    br_refs = refs[0:N_BRANCH]
    wbr_ref, wg_ref, bg_ref, wo_ref, o_ref, h_ref, acc_ref = refs[N_BRANCH:]
    j = pl.program_id(1)

    @pl.when(j == 0)
    def _():
        h_ref[...] = _norm_mod(x_ref[...], g_ref[...], mod_ref[0, 0:1, :],
                               mod_ref[0, 1:2, :]).astype(BF16)
        acc_ref[...] = jnp.zeros_like(acc_ref)

    h = h_ref[...]
    mix = None
    for n in range(N_BRANCH):
        gate = _sigmoid(jnp.dot(h, wg_ref[n], preferred_element_type=F32) + bg_ref[n])
        proj = jnp.dot(br_refs[n][...], wbr_ref[n], preferred_element_type=F32)
        term = gate * proj
        mix = term if mix is None else mix + term
    acc_ref[...] += jnp.dot(mix.astype(BF16), wo_ref[...], preferred_element_type=F32)

    @pl.when(j == pl.num_programs(1) - 1)
    def _():
        o_ref[...] = x_ref[...] + mod_ref[0, 2:3, :] * acc_ref[...]


def _merge(x2d, mod, g, branches, w_branch_blk, w_gate_blk, b_gate, w_out_bf, tokens_per_mod, layer):
    t = x2d.shape[0]
    tm, tn = TM_PROJ, TN_MERGE
    nch = D_MODEL // tn
    br_spec = pl.BlockSpec((tm, BR_W), lambda i, j: (i, 0))
    bg = jnp.transpose(b_gate.reshape(N_BRANCH, nch, 1, tn), (1, 0, 2, 3))
    return pl.pallas_call(
        _merge_kernel,
        out_shape=jax.ShapeDtypeStruct((t, D_MODEL), F32),
        grid=(t // tm, nch),
        in_specs=[
            pl.BlockSpec((tm, D_MODEL), lambda i, j: (i, 0)),
            pl.BlockSpec((1, 6, D_MODEL), lambda i, j: ((i * tm) // tokens_per_mod, 0, 0)),
            pl.BlockSpec((1, D_MODEL), lambda i, j: (0, 0)),
            br_spec, br_spec, br_spec, br_spec,
            pl.BlockSpec((None, None, N_BRANCH, BR_W, tn), lambda i, j: (layer, j, 0, 0, 0)),
            pl.BlockSpec((None, None, N_BRANCH, D_MODEL, tn), lambda i, j: (layer, j, 0, 0, 0)),
            pl.BlockSpec((None, N_BRANCH, 1, tn), lambda i, j: (j, 0, 0, 0)),
            pl.BlockSpec((None, tn, D_MODEL), lambda i, j: (layer, j, 0)),
        ],
        out_specs=pl.BlockSpec((tm, D_MODEL), lambda i, j: (i, 0)),
        scratch_shapes=[pltpu.VMEM((tm, D_MODEL), BF16), pltpu.VMEM((tm, D_MODEL), F32)],
        compiler_params=_cparams(("parallel", "arbitrary")),
        name="merge",
    )(x2d, mod, g.reshape(1, D_MODEL), *branches, w_branch_blk, w_gate_blk, bg, w_out_bf)


def _route(logits):
    lane = lax.broadcasted_iota(jnp.int32, logits.shape, 1)
    ninf = -jnp.inf
    lg = jnp.where(lane < N_GROUPS, logits, ninf)
    mg = lg.max(axis=-1, keepdims=True)
    pg_top = 1.0 / jnp.exp(lg - mg).sum(axis=-1, keepdims=True)
    g_idx = jnp.where(lg == mg, lane, ROUTER_W).min(axis=-1, keepdims=True)
    lo = N_GROUPS + EXP_PER_GROUP * g_idx
    le = jnp.where((lane >= lo) & (lane < lo + EXP_PER_GROUP), logits, ninf)
    m1 = le.max(axis=-1, keepdims=True)
    e1 = jnp.where(le == m1, lane, ROUTER_W).min(axis=-1, keepdims=True)
    le2 = jnp.where(lane == e1, ninf, le)
    m2 = le2.max(axis=-1, keepdims=True)
    e2 = jnp.where(le2 == m2, lane, ROUTER_W).min(axis=-1, keepdims=True)
    se = jnp.exp(le - m1).sum(axis=-1, keepdims=True)
    pe1 = 1.0 / se
    pe2 = jnp.exp(m2 - m1) / se
    tot = pe1 + pe2
    w1 = pe1 / tot * pg_top
    w2 = pe2 / tot * pg_top
    return jnp.where(lane == e1, w1, 0.0) + jnp.where(lane == e2, w2, 0.0), g_idx


def _moe_kernel(x_ref, mod_ref, g_ref, wr_ref, br_ref, wg_ref, wu_ref, wd_ref, fg_ref,
                o_ref, h_ref, comb_ref, grp_ref, rank_ref, cnt_ref, *, final):
    grp = pl.program_id(1)
    tm = x_ref.shape[0]

    @pl.when(grp == 0)
    def _():
        h = _norm_mod(x_ref[...], g_ref[...], mod_ref[0, 3:4, :], mod_ref[0, 4:5, :])
        h_hi = h.astype(BF16)
        h_ref[...] = h_hi
        h_lo = (h - h_hi.astype(F32)).astype(BF16)
        w = wr_ref[...]
        w_hi = w.astype(BF16)
        w_lo = (w - w_hi.astype(F32)).astype(BF16)
        logits = (jnp.dot(h_hi, w_hi, preferred_element_type=F32)
                  + jnp.dot(h_lo, w_hi, preferred_element_type=F32)
                  + jnp.dot(h_hi, w_lo, preferred_element_type=F32)) + br_ref[...]
        comb, g_idx = _route(logits)
        comb_ref[...] = comb
        lane = lax.broadcasted_iota(jnp.int32, (tm, ROUTER_W), 1)
        member = lane == g_idx
        onehot = jnp.where(member, 1.0, 0.0).astype(BF16)
        earlier = jnp.where(lax.broadcasted_iota(jnp.int32, (tm, tm), 0)
                            > lax.broadcasted_iota(jnp.int32, (tm, tm), 1), 1.0, 0.0).astype(BF16)
        before = jnp.dot(earlier, onehot, preferred_element_type=F32)
        rank_ref[...] = jnp.where(member, before, 0.0).sum(axis=-1, keepdims=True).astype(jnp.int32)
        grp_ref[...] = g_idx
        cnt_ref[...] = jnp.where(member, 1.0, 0.0).sum(axis=0, keepdims=True)
        o_ref[...] = jnp.zeros_like(o_ref)

    lane_r = lax.broadcasted_iota(jnp.int32, (1, ROUTER_W), 1)
    count = jnp.where(lane_r == grp, cnt_ref[...], 0.0).sum().astype(jnp.int32)
    in_group = grp_ref[...] == grp
    rank = rank_ref[...]
    comb = comb_ref[...]
    comb_hi = comb.astype(BF16)
    comb_lo = (comb - comb_hi.astype(F32)).astype(BF16)
    slot_lane = lax.broadcasted_iota(jnp.int32, (tm, MOE_CHUNK), 1)
    lane_c = lax.broadcasted_iota(jnp.int32, (MOE_CHUNK, ROUTER_W), 1)

    def chunk(c, carry):
        sel = jnp.where(in_group & (rank - c * MOE_CHUNK == slot_lane), 1.0, 0.0).astype(BF16)
        xc = lax.dot_general(sel, h_ref[...], _TN, preferred_element_type=F32).astype(BF16)
        cwc = (lax.dot_general(sel, comb_hi, _TN, preferred_element_type=F32)
               + lax.dot_general(sel, comb_lo, _TN, preferred_element_type=F32))
        out = None
        for e in range(EXP_PER_GROUP):
            cw = jnp.where(lane_c == N_GROUPS + EXP_PER_GROUP * grp + e, cwc, 0.0).sum(
                axis=-1, keepdims=True)
            gate = jnp.dot(xc, wg_ref[e], preferred_element_type=F32)
            up = jnp.dot(xc, wu_ref[e], preferred_element_type=F32)
            hid = (gate * _sigmoid(gate)) * up * cw
            term = jnp.dot(hid.astype(BF16), wd_ref[grp, e], preferred_element_type=F32)
            out = term if out is None else out + term
        o_ref[...] += jnp.dot(sel, out.astype(BF16), preferred_element_type=F32)
        return carry

    lax.fori_loop(0, (count + MOE_CHUNK - 1) // MOE_CHUNK, chunk, 0)

    @pl.when(grp == pl.num_programs(1) - 1)
    def _():
        y = x_ref[...] + mod_ref[0, 5:6, :] * o_ref[...]
        if final:
            ms = jnp.mean(y * y, axis=-1, keepdims=True)
            y = y * lax.rsqrt(ms + EPS) * fg_ref[...]
        o_ref[...] = y


def _moe(x2d, mod, g, w_router, b_router, w_g_bf, w_u_bf, w_d_bf, final_g, tokens_per_mod, final,
         layer):
    t = x2d.shape[0]
    tm = TM_PROJ
    w_spec_in = pl.BlockSpec((None, None, EXP_PER_GROUP, D_MODEL, D_EXPERT),
                             lambda i, grp: (layer, grp, 0, 0, 0))
    return pl.pallas_call(
        functools.partial(_moe_kernel, final=final),
        out_shape=jax.ShapeDtypeStruct((t, D_MODEL), F32),
        grid=(t // tm, N_GROUPS),
        in_specs=[
            pl.BlockSpec((tm, D_MODEL), lambda i, grp: (i, 0)),
            pl.BlockSpec((1, 6, D_MODEL), lambda i, grp: ((i * tm) // tokens_per_mod, 0, 0)),
            pl.BlockSpec((1, D_MODEL), lambda i, grp: (0, 0)),
            pl.BlockSpec((D_MODEL, ROUTER_W), lambda i, grp: (0, 0)),
            pl.BlockSpec((1, ROUTER_W), lambda i, grp: (0, 0)),
            w_spec_in, w_spec_in,
            pl.BlockSpec((None, N_GROUPS, EXP_PER_GROUP, D_EXPERT, D_MODEL),
                         lambda i, grp: (layer, 0, 0, 0, 0), pipeline_mode=pl.Buffered(1)),
            pl.BlockSpec((1, D_MODEL), lambda i, grp: (0, 0)),
        ],
        out_specs=pl.BlockSpec((tm, D_MODEL), lambda i, grp: (i, 0)),
        scratch_shapes=[pltpu.VMEM((tm, D_MODEL), BF16), pltpu.VMEM((tm, ROUTER_W), F32),
                        pltpu.VMEM((tm, 1), jnp.int32), pltpu.VMEM((tm, 1), jnp.int32),
                        pltpu.VMEM((1, ROUTER_W), F32)],
        compiler_params=_cparams(("parallel", "arbitrary"), VMEM_LIMIT_MOE),
        name="moe_final" if final else "moe",
    )(x2d, mod, g.reshape(1, D_MODEL), w_router, b_router, w_g_bf, w_u_bf, w_d_bf,
      final_g.reshape(1, D_MODEL))


def _rope_tables(seq):
    nf = HEAD_DIM // 4
    t = jnp.arange(seq)
    inv = ROPE_BASE ** (-jnp.arange(nf, dtype=F32) / nf)
    pos = jnp.stack([t // GRID_W, t % GRID_W], -1).astype(F32)
    ang = pos[:, :, None] * inv
    cos, sin = jnp.cos(ang), jnp.sin(ang)
    cos64 = jnp.concatenate([cos[:, 0], cos[:, 0], cos[:, 1], cos[:, 1]], axis=-1)
    sin64 = jnp.concatenate([-sin[:, 0], sin[:, 0], -sin[:, 1], sin[:, 1]], axis=-1)
    return jnp.tile(cos64, (1, 2)), jnp.tile(sin64, (1, 2))


def _seq_minor_view(cache):
    nb, depth, seq = cache.shape[:3]
    nd = cache.ndim
    return jnp.transpose(cache, (0, 1, *range(3, nd), 2)).reshape(nb, depth, -1, seq)


def _seq_major_view(arr, head_dims):
    nb, depth, _, seq = arr.shape
    nd = 3 + len(head_dims)
    arr = arr.reshape(nb, depth, *head_dims, seq)
    return jnp.transpose(arr, (0, 1, nd - 1, *range(2, nd - 1)))


def kernel(x_prompt, x_sample, cache_attn_k, cache_attn_v, cache_diff_k, cache_diff_v, cache_na_k, cache_na_v, c, c_ctx, w_mod, b_mod, norm1_g, w_in, attn_sink, diff_lambda, diff_sub_g, na_rpb, w_branch, w_gate, b_gate, w_out, norm2_g, w_router_group, b_router_group, w_router_expert, b_router_expert, w_exp_gate, w_exp_up, w_exp_down, final_g):
    nbp, seq_p, _ = x_prompt.shape
    nbs, seq_s, _ = x_sample.shape
    xp = x_prompt.reshape(nbp * seq_p, D_MODEL)
    xs = x_sample.reshape(nbs * seq_s, D_MODEL)

    cond = jnp.concatenate([c_ctx[None, :], c], axis=0)
    n_cond = -(-cond.shape[0] // 8) * 8
    cond = jnp.pad(cond, ((0, n_cond - cond.shape[0]), (0, 0)))
    mods = _modulation(cond, w_mod, b_mod).reshape(DEPTH, n_cond, 6, D_MODEL)

    rope = _rope_tables(seq_s)
    cache_a = (_seq_minor_view(cache_attn_k), _seq_minor_view(cache_attn_v))
    cache_b = (_seq_minor_view(cache_diff_k), cache_diff_v)
    cache_c = (_seq_minor_view(cache_na_k), _seq_minor_view(cache_na_v))

    w_in_bf = w_in.astype(BF16)
    nch = D_MODEL // TN_MERGE
    w_branch_bf = jnp.transpose(
        w_branch.astype(BF16).reshape(DEPTH, N_BRANCH, BR_W, nch, TN_MERGE), (0, 3, 1, 2, 4))
    w_gate_bf = jnp.transpose(
        w_gate.astype(BF16).reshape(DEPTH, D_MODEL, N_BRANCH, nch, TN_MERGE), (0, 3, 2, 1, 4))
    w_out_bf = w_out.astype(BF16)
    grouped = (DEPTH, N_GROUPS, EXP_PER_GROUP)
    w_eg_bf = w_exp_gate.astype(BF16).reshape(*grouped, D_MODEL, D_EXPERT)
    w_eu_bf = w_exp_up.astype(BF16).reshape(*grouped, D_MODEL, D_EXPERT)
    w_ed_bf = w_exp_down.astype(BF16).reshape(*grouped, D_EXPERT, D_MODEL)
    pad = ROUTER_W - N_GROUPS - N_EXPERTS
    w_router = jnp.concatenate(
        [w_router_group, w_router_expert, jnp.zeros((DEPTH, D_MODEL, pad), F32)], axis=-1)
    b_router = jnp.concatenate(
        [b_router_group, b_router_expert, jnp.zeros((DEPTH, pad), F32)], axis=-1)

    kv = None
    for l in range(DEPTH):
        lam_init = 0.8 - 0.6 * math.exp(-0.3 * l)
        mod_p = mods[l, 0:1]
        mod_s = mods[l, 1:1 + nbs]
        final = l == DEPTH - 1
        moe_w = (w_router[l], b_router[l].reshape(1, ROUTER_W), w_eg_bf, w_eu_bf, w_ed_bf)
        merge_w = (w_branch_bf, w_gate_bf, b_gate[l], w_out_bf)

        pr = _inproj(xp, mod_p, norm1_g[l], w_in_bf, xp.shape[0], seq=seq_p, layer=l, prev_kv=kv)
        kv = {name: pr[name] for name in _CTX_F32}
        o_a = _sink_attn(attn_sink[l], pr["a_q"], (kv["a_k"], kv["a_v"]), nb=nbp, seq=seq_p, layer=l)
        o_b = _diff_attn(diff_lambda[l], diff_sub_g[l], pr["b_q"], (kv["b_k"], kv["b_v"]),
                         nb=nbp, seq=seq_p, lam_init=lam_init, layer=l)
        o_c = _dense_attn(pr["c_q"], (kv["c_k"], kv["c_v"]), nb=nbp, seq=seq_p, layer=l)
        o_f = _fourier(pr["f"], nb=nbp, seq=seq_p)
        xp = _merge(xp, mod_p, norm1_g[l], (o_a, o_b, o_c, o_f), *merge_w, xp.shape[0], l)
        xp = _moe(xp, mod_p, norm2_g[l], *moe_w, final_g, xp.shape[0], final, l)

        pr = _inproj(xs, mod_s, norm1_g[l], w_in_bf, seq_s, rope=rope, layer=l)
        o_a = _sink_attn(attn_sink[l], pr["a_q"], cache_a, nb=nbs, seq=seq_s, layer=l,
                         kv_loc=(pr["a_k"], pr["a_v"]))
        o_b = _diff_attn(diff_lambda[l], diff_sub_g[l], pr["b_q"], cache_b, nb=nbs, seq=seq_s,
                         lam_init=lam_init, layer=l, kv_loc=(pr["b_k"], pr["b_v"]))
        o_c = _na_attn(pr["c_q"], (pr["c_k"], pr["c_v"]), cache_c,
                       _na_rpb_table(na_rpb[l]), nb=nbs, seq=seq_s, layer=l)
        o_f = _fourier(pr["f"], nb=nbs, seq=seq_s)
        xs = _merge(xs, mod_s, norm1_g[l], (o_a, o_b, o_c, o_f), *merge_w, seq_s, l)
        xs = _moe(xs, mod_s, norm2_g[l], *moe_w, final_g, seq_s, final, l)

    return (
        xp.reshape(x_prompt.shape),
        xs.reshape(x_sample.shape),
        _seq_major_view(kv["a_k"], (A_KV_HEADS, HEAD_DIM)),
        _seq_major_view(kv["a_v"], (A_KV_HEADS, HEAD_DIM)),
        _seq_major_view(kv["b_k"], (DIFF_HEADS, 2, HEAD_DIM)),
        kv["b_v"],
        _seq_major_view(kv["c_k"], (NA_HEADS, HEAD_DIM)),
        _seq_major_view(kv["c_v"], (NA_HEADS, HEAD_DIM)),
    )
```

```python
import functools
import math

import jax
import jax.numpy as jnp
import numpy as np
from jax import lax
from jax.experimental import pallas as pl
from jax.experimental.pallas import tpu as pltpu

F32 = jnp.float32
BF16 = jnp.bfloat16

D_MODEL = 2048
DEPTH = 2
GRID_W = 64
HEAD_DIM = 64
A_HEADS = 8
A_KV_HEADS = 2
A_GROUP = A_HEADS // A_KV_HEADS
WIN = 128
DIFF_HEADS = 4
DIFF_V_DIM = 2 * HEAD_DIM
NA_HEADS = 8
NA_ROWS = 8
NA_COLS = 16
F_GROUPS = 4
F_GROUP_W = 128
N_BRANCH = 4
BR_W = A_HEADS * HEAD_DIM
N_GROUPS = 4
EXP_PER_GROUP = 4
N_EXPERTS = N_GROUPS * EXP_PER_GROUP
D_EXPERT = 256
ROPE_BASE = 10000.0
EPS = 1e-6
NEG = -1e30
SCALE = HEAD_DIM ** -0.5

_IN_GROUPS = (
    ("a_q", 0, 512), ("a_k", 512, 128), ("a_v", 640, 128),
    ("b_q", 768, 512), ("b_k", 1280, 512), ("b_v", 1792, 512),
    ("c_q", 2304, 512), ("c_k", 2816, 512), ("c_v", 3328, 512),
    ("f", 3840, 512),
)
IN_W = 4352
_Q_GROUPS = ("a_q", "b_q", "c_q")
_ROPED = ("a_q", "a_k", "b_q", "b_k")
_CTX_F32 = ("a_k", "a_v", "b_k", "b_v", "c_k", "c_v")
_CTX_SEQ_MINOR = ("a_k", "a_v", "b_k", "c_k", "c_v")

LANES = 128
VMEM_LIMIT = 56 * 1024 * 1024
VMEM_LIMIT_MOE = 59 * 1024 * 1024

TM_PROJ = 512
TQ_SINK = 256
TQ_DIFF = 512
NA_QROWS = 4
NA_BAND = NA_ROWS + NA_QROWS - 1
TN_MERGE = 256
TN_MOD = 1024
ROUTER_W = 128
MOE_CHUNK = 128

_NT = (((1,), (1,)), ((), ()))
_TN = (((0,), (0,)), ((), ()))


def _cparams(sem, vmem_limit=VMEM_LIMIT):
    return pltpu.CompilerParams(dimension_semantics=sem, vmem_limit_bytes=vmem_limit)


def _sigmoid(x):
    return 1.0 / (1.0 + jnp.exp(-x))


def _norm_mod(x, g, shift, scale):
    ms = jnp.mean(x * x, axis=-1, keepdims=True)
    return (x * lax.rsqrt(ms + EPS) * g) * (1.0 + scale) + shift


def _mod_kernel(c_ref, w_ref, b_ref, o_ref):
    c = c_ref[...]
    s = (c * _sigmoid(c)).astype(BF16)
    o_ref[0] = jnp.dot(s, w_ref[0].astype(BF16), preferred_element_type=F32) + b_ref[0]


def _modulation(cond, w_mod, b_mod):
    n = cond.shape[0]
    out_w = w_mod.shape[-1]
    return pl.pallas_call(
        _mod_kernel,
        out_shape=jax.ShapeDtypeStruct((DEPTH, n, out_w), F32),
        grid=(DEPTH, out_w // TN_MOD),
        in_specs=[
            pl.BlockSpec((n, D_MODEL), lambda l, j: (0, 0)),
            pl.BlockSpec((1, D_MODEL, TN_MOD), lambda l, j: (l, 0, j)),
            pl.BlockSpec((1, 1, TN_MOD), lambda l, j: (l, 0, j)),
        ],
        out_specs=pl.BlockSpec((1, n, TN_MOD), lambda l, j: (l, 0, j)),
        compiler_params=_cparams(("parallel", "parallel")),
        name="modulation",
    )(cond, w_mod, b_mod.reshape(DEPTH, 1, out_w))


def _rope128(y, cos, sin):
    lane = lax.broadcasted_iota(jnp.int32, y.shape, 1)
    first = (lane % 32) < 16
    partner = jnp.where(first, pltpu.roll(y, LANES - 16, 1), pltpu.roll(y, 16, 1))
    return y * cos + partner * sin


def _inproj_kernel(*refs, latent, n_alias, seq, layer, fill_depth):
    if latent:
        x_ref, mod_ref, g_ref, w_ref, cos_ref, sin_ref = refs[:6]
        out_refs = refs[6:]
    else:
        x_ref, mod_ref, g_ref, w_ref = refs[:4]
        out_refs = refs[4 + n_alias:]
    h = _norm_mod(x_ref[...], g_ref[...], mod_ref[0, 0:1, :], mod_ref[0, 1:2, :]).astype(BF16)
    for (name, start, width), o_ref in zip(_IN_GROUPS, out_refs):
        y = jnp.dot(h, w_ref[:, start:start + width], preferred_element_type=F32)
        if name in _Q_GROUPS:
            y = y * SCALE
        if latent and name in _ROPED:
            cos = cos_ref[...]
            sin = sin_ref[...]
            for c0 in range(0, width, LANES):
                o_ref[:, c0:c0 + LANES] = _rope128(y[:, c0:c0 + LANES], cos, sin).astype(o_ref.dtype)
        elif not latent and name in _CTX_F32:
            for bb in range(o_ref.shape[0]):
                yb = y[bb * seq:(bb + 1) * seq]
                dst = o_ref.at[bb, layer] if fill_depth else o_ref.at[bb]
                if name in _CTX_SEQ_MINOR:
                    dst[...] = yb.T
                else:
                    for hd in range(DIFF_HEADS):
                        dst[:, hd, :] = _head(yb, hd, DIFF_V_DIM)
                if fill_depth:
                    for other in range(DEPTH):
                        if other != layer:
                            o_ref[bb, other] = jnp.zeros(o_ref.shape[2:], F32)
        else:
            o_ref[...] = y.astype(o_ref.dtype)


def _inproj(x2d, mod, g, w_in_bf, tokens_per_mod, *, rope=None, seq=None, layer=0, prev_kv=None):
    t = x2d.shape[0]
    tm = TM_PROJ
    latent = rope is not None
    fill_depth = not latent and prev_kv is None
    in_specs = [
        pl.BlockSpec((tm, D_MODEL), lambda i: (i, 0)),
        pl.BlockSpec((1, 6, D_MODEL), lambda i: ((i * tm) // tokens_per_mod, 0, 0)),
        pl.BlockSpec((1, D_MODEL), lambda i: (0, 0)),
        pl.BlockSpec((None, D_MODEL, IN_W), lambda i: (layer, 0, 0), pipeline_mode=pl.Buffered(1)),
    ]
    args = [x2d, mod, g.reshape(1, D_MODEL), w_in_bf]
    aliases = {}
    n_alias = 0
    if latent:
        nblk = rope[0].shape[0] // tm
        in_specs += [pl.BlockSpec((tm, LANES), lambda i: (i % nblk, 0))] * 2
        args += list(rope)
    elif prev_kv is not None:
        n_alias = len(_CTX_F32)
        in_specs += [pl.BlockSpec(memory_space=pl.ANY)] * n_alias
        args += [prev_kv[name] for name in _CTX_F32]
    out_shape, out_specs = [], []
    tb = tm // seq if not latent else None
    for oi, (name, _, width) in enumerate(_IN_GROUPS):
        if not latent and name in _CTX_F32:
            blk = (width, seq) if name in _CTX_SEQ_MINOR else (seq, DIFF_HEADS, DIFF_V_DIM)
            out_shape.append(jax.ShapeDtypeStruct((t // seq, DEPTH) + blk, F32))
            if fill_depth:
                out_specs.append(pl.BlockSpec((tb, DEPTH) + blk,
                                              lambda i, nd=len(blk): (i, 0) + (0,) * nd))
            else:
                out_specs.append(pl.BlockSpec((tb, None) + blk,
                                              lambda i, nd=len(blk): (i, layer) + (0,) * nd))
                aliases[4 + _CTX_F32.index(name)] = oi
        else:
            out_shape.append(jax.ShapeDtypeStruct((t, width), BF16))
            out_specs.append(pl.BlockSpec((tm, width), lambda i: (i, 0)))
    outs = pl.pallas_call(
        functools.partial(_inproj_kernel, latent=latent, n_alias=n_alias, seq=seq, layer=layer,
                          fill_depth=fill_depth),
        out_shape=out_shape,
        grid=(t // tm,),
        in_specs=in_specs,
        out_specs=out_specs,
        input_output_aliases=aliases,
        compiler_params=_cparams(("parallel",)),
        name="inproj_latent" if latent else "inproj_context",
    )(*args)
    return dict(zip([n for n, _, _ in _IN_GROUPS], outs))


def _attend(q, segs, sink_col=None):
    scores = []
    for k, _, bias, k_seq_minor, _ in segs:
        if k_seq_minor:
            s = jnp.dot(q, k, preferred_element_type=F32)
        else:
            s = lax.dot_general(q, k, _NT, preferred_element_type=F32)
        if bias is not None:
            s = s + bias
        scores.append(s)
    m = scores[0].max(axis=-1, keepdims=True)
    for s in scores[1:]:
        m = jnp.maximum(m, s.max(axis=-1, keepdims=True))
    if sink_col is not None:
        m = jnp.maximum(m, sink_col)
    acc = None
    den = None
    for s, (_, v, _, _, v_seq_minor) in zip(scores, segs):
        p = jnp.exp(s - m)
        ps = p.sum(axis=-1, keepdims=True)
        if v_seq_minor:
            o = lax.dot_general(p.astype(BF16), v, _NT, preferred_element_type=F32)
        else:
            o = jnp.dot(p.astype(BF16), v, preferred_element_type=F32)
        acc = o if acc is None else acc + o
        den = ps if den is None else den + ps
    if sink_col is not None:
        den = den + jnp.exp(sink_col - m)
    return acc, den


def _head(x, h, width=HEAD_DIM):
    return x[:, h * width:(h + 1) * width]


def _head_rows(x, h, width=HEAD_DIM):
    return x[h * width:(h + 1) * width, :]


def _pair(x, i):
    return x[:, i * LANES:(i + 1) * LANES]


def _pair_rows(x, i):
    return x[i * LANES:(i + 1) * LANES, :]


def _keep_half(x, half, axis):
    idx = lax.broadcasted_iota(jnp.int32, x.shape, axis)
    return jnp.where((idx // HEAD_DIM) == half, x, jnp.zeros_like(x))


def _place_rows(x, half):
    z = jnp.zeros_like(x)
    return jnp.concatenate([x, z] if half == 0 else [z, x], axis=0)


def _kv_spec(width, seq, layer):
    return pl.BlockSpec((None, None, width, seq), lambda b, i: (b, layer, 0, 0))


def _sink_attn_kernel(*refs, latent, tq, seq):
    if latent:
        sink_ref, q_ref, k_ref, v_ref, kc_ref, vc_ref, o_ref = refs
    else:
        sink_ref, q_ref, kc_ref, vc_ref, o_ref = refs
    q_all = q_ref[...]
    k_ctx = kc_ref[...].astype(BF16)
    v_ctx = vc_ref[...].astype(BF16)
    if latent:
        band = tq + 2 * WIN
        qb = pl.program_id(1)
        start = pl.multiple_of(jnp.clip(qb * tq - WIN, 0, seq - band), WIN)
        k_loc = k_ref[pl.ds(start, band), :]
        v_loc = v_ref[pl.ds(start, band), :]
        k_swp = jnp.concatenate([_head(k_loc, 1), _head(k_loc, 0)], axis=1)
        v_swp = jnp.concatenate([_head(v_loc, 1), _head(v_loc, 0)], axis=1)
        qpos = qb * tq + lax.broadcasted_iota(jnp.int32, (tq, band), 0)
        kpos = start + lax.broadcasted_iota(jnp.int32, (tq, band), 1)
        bias = jnp.where(jnp.abs(qpos - kpos) <= WIN, 0.0, NEG).astype(F32)
    outs = []
    for i in range(A_HEADS // 2):
        kv = (2 * i) // A_GROUP
        o_pair = None
        for half in range(2):
            h = 2 * i + half
            sink_col = jnp.full((tq, 1), sink_ref[0, h], F32)
            segs = [(_place_rows(_head_rows(k_ctx, kv), half),
                     _place_rows(_head_rows(v_ctx, kv), half), None, True, True)]
            if latent:
                k_src, v_src = (k_loc, v_loc) if kv == half else (k_swp, v_swp)
                segs.append((_keep_half(k_src, half, 1), _keep_half(v_src, half, 1),
                             bias, False, False))
            acc, den = _attend(_pair(q_all, i), segs, sink_col)
            o = acc / den
            o_pair = o if o_pair is None else o_pair + o
        outs.append(o_pair)
    o_ref[...] = jnp.concatenate(outs, axis=1).astype(o_ref.dtype)


def _sink_attn(sink, q, kv_ctx, *, nb, seq, layer, kv_loc=None):
    latent = kv_loc is not None
    tq = TQ_SINK if latent else seq
    nq = seq // tq
    kw = A_KV_HEADS * HEAD_DIM
    s_ctx = kv_ctx[0].shape[-1]
    in_specs = [
        pl.BlockSpec(memory_space=pltpu.SMEM),
        pl.BlockSpec((tq, BR_W), lambda b, i: (b * nq + i, 0)),
    ]
    args = [sink.reshape(1, A_HEADS), q]
    if latent:
        in_specs += [pl.BlockSpec((seq, kw), lambda b, i: (b, 0))] * 2
        args += list(kv_loc)
    in_specs += [_kv_spec(kw, s_ctx, layer)] * 2
    args += list(kv_ctx)
    return pl.pallas_call(
        functools.partial(_sink_attn_kernel, latent=latent, tq=tq, seq=seq),
        out_shape=jax.ShapeDtypeStruct(q.shape, BF16),
        grid=(nb, nq),
        in_specs=in_specs,
        out_specs=pl.BlockSpec((tq, BR_W), lambda b, i: (b * nq + i, 0)),
        compiler_params=_cparams(("parallel", "arbitrary")),
        name="sink_attn_latent" if latent else "sink_attn_context",
    )(*args)


def _diff_attn_kernel(*refs, latent, lam_init):
    if latent:
        lam_ref, g_ref, q_ref, k_ref, v_ref, kc_ref, vc_ref, o_ref = refs
    else:
        lam_ref, g_ref, q_ref, kc_ref, vc_ref, o_ref = refs
    lam = lam_ref[...]
    lam_a = jnp.sum(lam[0:1] * lam[1:2], axis=-1, keepdims=True)
    lam_b = jnp.sum(lam[2:3] * lam[3:4], axis=-1, keepdims=True)
    lam_full = jnp.exp(lam_a) - jnp.exp(lam_b) + lam_init
    q_all = q_ref[...]
    k_ctx = kc_ref[...].astype(BF16)
    if latent:
        k_loc = k_ref[...]
        v_loc = v_ref[...]
    outs = []
    for h in range(DIFF_HEADS):
        v_ctx = vc_ref[:, h, :].astype(BF16)
        res = []
        for sub in range(2):
            segs = [(_keep_half(_pair_rows(k_ctx, h), sub, 0), v_ctx, None, True, False)]
            if latent:
                segs.append((_keep_half(_pair(k_loc, h), sub, 1), _pair(v_loc, h),
                             None, False, False))
            acc, den = _attend(_pair(q_all, h), segs)
            res.append(acc / den)
        o = res[0] - lam_full * res[1]
        ms = jnp.mean(o * o, axis=-1, keepdims=True)
        outs.append(o * lax.rsqrt(ms + EPS) * g_ref[...] * (1.0 - lam_init))
    o_ref[...] = jnp.concatenate(outs, axis=1).astype(o_ref.dtype)


def _diff_attn(lam, sub_g, q, kv_ctx, *, nb, seq, lam_init, layer, kv_loc=None):
    latent = kv_loc is not None
    tq = TQ_DIFF if latent else seq
    nq = seq // tq
    kw = DIFF_HEADS * 2 * HEAD_DIM
    s_ctx = kv_ctx[0].shape[-1]
    in_specs = [
        pl.BlockSpec((4, HEAD_DIM), lambda b, i: (0, 0)),
        pl.BlockSpec((1, DIFF_V_DIM), lambda b, i: (0, 0)),
        pl.BlockSpec((tq, BR_W), lambda b, i: (b * nq + i, 0)),
    ]
    args = [lam, sub_g.reshape(1, DIFF_V_DIM), q]
    if latent:
        in_specs += [pl.BlockSpec((seq, kw), lambda b, i: (b, 0))] * 2
        args += list(kv_loc)
    in_specs += [_kv_spec(kw, s_ctx, layer),
                 pl.BlockSpec((None, None, s_ctx, DIFF_HEADS, DIFF_V_DIM),
                              lambda b, i: (b, layer, 0, 0, 0))]
    args += list(kv_ctx)
    return pl.pallas_call(
        functools.partial(_diff_attn_kernel, latent=latent, lam_init=lam_init),
        out_shape=jax.ShapeDtypeStruct(q.shape, BF16),
        grid=(nb, nq),
        in_specs=in_specs,
        out_specs=pl.BlockSpec((tq, BR_W), lambda b, i: (b * nq + i, 0)),
        compiler_params=_cparams(("parallel", "arbitrary")),
        name="diff_attn_latent" if latent else "diff_attn_context",
    )(*args)


def _dense_attn_kernel(q_ref, k_ref, v_ref, o_ref):
    q_all = q_ref[...]
    k = k_ref[...].astype(BF16)
    v = v_ref[...].astype(BF16)
    outs = []
    for i in range(NA_HEADS // 2):
        o_pair = None
        for half in range(2):
            segs = [(_keep_half(_pair_rows(k, i), half, 0), _keep_half(_pair_rows(v, i), half, 0),
                     None, True, True)]
            acc, den = _attend(_pair(q_all, i), segs)
            o = acc / den
            o_pair = o if o_pair is None else o_pair + o
        outs.append(o_pair)
    o_ref[...] = jnp.concatenate(outs, axis=1).astype(o_ref.dtype)


def _dense_attn(q, kv_ctx, *, nb, seq, layer):
    return pl.pallas_call(
        _dense_attn_kernel,
        out_shape=jax.ShapeDtypeStruct(q.shape, BF16),
        grid=(nb, 1),
        in_specs=[pl.BlockSpec((seq, BR_W), lambda b, i: (b, 0)),
                  _kv_spec(BR_W, seq, layer), _kv_spec(BR_W, seq, layer)],
        out_specs=pl.BlockSpec((seq, BR_W), lambda b, i: (b, 0)),
        compiler_params=_cparams(("parallel", "arbitrary")),
        name="dense_attn_context",
    )(q, *kv_ctx)


def _na_window_start(r, rows):
    return jnp.clip(r - NA_ROWS // 2, 0, rows - NA_ROWS)


def _na_band_start(step, rows):
    return jnp.clip(step * NA_QROWS - NA_ROWS // 2, 0, rows - NA_BAND)


def _na_attn_kernel(q_ref, k_ref, v_ref, kc_ref, vc_ref, tab_ref, o_ref, bias_ref, *, rows):
    step = pl.program_id(0)
    band_row = _na_band_start(step, rows)

    @pl.when(pl.program_id(1) == 0)
    def _():
        for ri in range(NA_QROWS):
            r = step * NA_QROWS + ri
            rs = _na_window_start(r, rows)
            for h in range(NA_HEADS):
                blocks = []
                for j in range(NA_BAND):
                    kr = band_row + j
                    ok = (kr >= rs) & (kr < rs + NA_ROWS)
                    dr = jnp.clip(kr - r + NA_ROWS - 1, 0, 2 * NA_ROWS - 2)
                    blocks.append(jnp.where(ok, tab_ref[h, dr], NEG))
                bias_ref[h, ri * GRID_W:(ri + 1) * GRID_W, :] = jnp.concatenate(blocks, axis=1)

    start = pl.multiple_of(band_row * GRID_W, GRID_W)
    band = NA_BAND * GRID_W
    q_all = q_ref[...]
    k_loc = k_ref[pl.ds(start, band), :]
    v_loc = v_ref[pl.ds(start, band), :]
    k_ctx = kc_ref[...].astype(BF16)
    v_ctx = vc_ref[...].astype(BF16)
    outs = []
    for i in range(NA_HEADS // 2):
        o_pair = None
        for half in range(2):
            segs = [(_keep_half(_pair_rows(k_ctx, i), half, 0),
                     _keep_half(_pair_rows(v_ctx, i), half, 0), None, True, True),
                    (_keep_half(_pair(k_loc, i), half, 1), _keep_half(_pair(v_loc, i), half, 1),
                     bias_ref[2 * i + half], False, False)]
            acc, den = _attend(_pair(q_all, i), segs)
            o = acc / den
            o_pair = o if o_pair is None else o_pair + o
        outs.append(o_pair)
    o_ref[...] = jnp.concatenate(outs, axis=1).astype(o_ref.dtype)


def _na_attn(q, kv_loc, kv_ctx, rpb_tab, *, nb, seq, layer):
    rows = seq // GRID_W
    nsteps = rows // NA_QROWS
    tq = NA_QROWS * GRID_W
    band = NA_BAND * GRID_W
    past = kv_ctx[0].shape[-1]
    for s in range(nsteps):
        us = min(max(s * NA_QROWS - NA_ROWS // 2, 0), rows - NA_BAND)
        for r in range(s * NA_QROWS, (s + 1) * NA_QROWS):
            rs = min(max(r - NA_ROWS // 2, 0), rows - NA_ROWS)
            assert us <= rs and rs + NA_ROWS <= us + NA_BAND
    ctx_spec = pl.BlockSpec((None, None, BR_W, past), lambda s, b: (b, layer, 0, 0))
    return pl.pallas_call(
        functools.partial(_na_attn_kernel, rows=rows),
        out_shape=jax.ShapeDtypeStruct(q.shape, BF16),
        grid=(nsteps, nb),
        in_specs=[
            pl.BlockSpec((tq, BR_W), lambda s, b: (b * nsteps + s, 0)),
            pl.BlockSpec((seq, BR_W), lambda s, b: (b, 0)),
            pl.BlockSpec((seq, BR_W), lambda s, b: (b, 0)),
            ctx_spec, ctx_spec,
            pl.BlockSpec(rpb_tab.shape, lambda s, b: (0, 0, 0, 0)),
        ],
        out_specs=pl.BlockSpec((tq, BR_W), lambda s, b: (b * nsteps + s, 0)),
        scratch_shapes=[pltpu.VMEM((NA_HEADS, tq, band), F32)],
        compiler_params=_cparams(("arbitrary", "arbitrary")),
        name="na_attn_latent",
    )(q, *kv_loc, *kv_ctx, rpb_tab)


def _rpb_expand_kernel(rpb_ref, onehot_ref, o_ref):
    o_ref[...] = jnp.dot(rpb_ref[...], onehot_ref[...], preferred_element_type=F32,
                         precision=lax.Precision.HIGHEST)


def _na_rpb_table(rpb):
    n_dr, n_dc = 2 * NA_ROWS - 1, 2 * NA_COLS - 1
    qcol = np.arange(GRID_W)
    kcol = np.arange(GRID_W)
    cs = np.clip(qcol - NA_COLS // 2, 0, GRID_W - NA_COLS)
    col_ok = (kcol[None, :] >= cs[:, None]) & (kcol[None, :] < cs[:, None] + NA_COLS)
    dc = np.clip(kcol[None, :] - qcol[:, None] + NA_COLS - 1, 0, n_dc - 1)
    onehot = np.zeros((32, GRID_W * GRID_W), np.float32)
    onehot[dc.reshape(-1), np.arange(GRID_W * GRID_W)] = 1.0
    rpb2 = jnp.pad(rpb.reshape(NA_HEADS * n_dr, n_dc), ((0, 0), (0, 32 - n_dc)))
    full = pl.pallas_call(
        _rpb_expand_kernel,
        out_shape=jax.ShapeDtypeStruct((NA_HEADS * n_dr, GRID_W * GRID_W), F32),
        name="rpb_expand",
    )(rpb2, jnp.asarray(onehot))
    full = full.reshape(NA_HEADS, n_dr, GRID_W, GRID_W)
    return full + jnp.asarray(np.where(col_ok, 0.0, NEG).astype(np.float32))


def _dft_tables(seq):
    pos = np.arange(seq)
    ang = 2.0 * np.pi * ((pos[:, None] * pos[None, :]) % seq) / seq
    left = np.concatenate([np.cos(ang), -np.sin(ang)], axis=1).astype(np.float32)
    ch = np.arange(F_GROUP_W)
    angw = 2.0 * np.pi * ((ch[:, None] * ch[None, :]) % F_GROUP_W) / F_GROUP_W
    right = np.concatenate([np.cos(angw), np.sin(angw)], axis=1).astype(np.float32)
    return jnp.asarray(left).astype(BF16), jnp.asarray(right).astype(BF16)


def _fourier_kernel(f_ref, left_ref, right_ref, o_ref, stack_ref, *, seq):
    f = f_ref[...]
    right = right_ref[...]
    for g in range(F_GROUPS):
        t = jnp.dot(_head(f, g, F_GROUP_W), right, preferred_element_type=F32).astype(BF16)
        stack_ref[0:seq, g * F_GROUP_W:(g + 1) * F_GROUP_W] = t[:, :F_GROUP_W]
        stack_ref[seq:2 * seq, g * F_GROUP_W:(g + 1) * F_GROUP_W] = t[:, F_GROUP_W:]
    y = jnp.dot(left_ref[...], stack_ref[...], preferred_element_type=F32)
    o_ref[...] = (y * (1.0 / math.sqrt(seq * F_GROUP_W))).astype(o_ref.dtype)


def _fourier(f, *, nb, seq):
    left, right = _dft_tables(seq)
    w = F_GROUPS * F_GROUP_W
    return pl.pallas_call(
        functools.partial(_fourier_kernel, seq=seq),
        out_shape=jax.ShapeDtypeStruct(f.shape, BF16),
        grid=(nb,),
        in_specs=[
            pl.BlockSpec((seq, w), lambda b: (b, 0)),
            pl.BlockSpec((seq, 2 * seq), lambda b: (0, 0)),
            pl.BlockSpec((F_GROUP_W, 2 * F_GROUP_W), lambda b: (0, 0)),
        ],
        out_specs=pl.BlockSpec((seq, w), lambda b: (b, 0)),
        scratch_shapes=[pltpu.VMEM((2 * seq, w), BF16)],
        compiler_params=_cparams(("parallel",)),
        name=f"fourier_{seq}",
    )(f, left, right)


def _merge_kernel(x_ref, mod_ref, g_ref, *refs):
    br_refs = refs[0:N_BRANCH]
    wbr_ref = refs[N_BRANCH]
    wg_refs = refs[N_BRANCH + 1:2 * N_BRANCH + 1]
    bg_refs = refs[2 * N_BRANCH + 1:3 * N_BRANCH + 1]
    wo_ref, o_ref, h_ref, acc_ref = refs[3 * N_BRANCH + 1:]
    j = pl.program_id(1)

    @pl.when(j == 0)
    def _():
        h_ref[...] = _norm_mod(x_ref[...], g_ref[...], mod_ref[0, 0:1, :],
                               mod_ref[0, 1:2, :]).astype(BF16)
        acc_ref[...] = jnp.zeros_like(acc_ref)

    h = h_ref[...]
    mix = None
    for n in range(N_BRANCH):
        gate = _sigmoid(jnp.dot(h, wg_refs[n][...], preferred_element_type=F32) + bg_refs[n][0])
        proj = jnp.dot(br_refs[n][...], wbr_ref[n], preferred_element_type=F32)
        term = gate * proj
        mix = term if mix is None else mix + term
    acc_ref[...] += jnp.dot(mix.astype(BF16), wo_ref[...], preferred_element_type=F32)

    @pl.when(j == pl.num_programs(1) - 1)
    def _():
        o_ref[...] = x_ref[...] + mod_ref[0, 2:3, :] * acc_ref[...]


def _merge(x2d, mod, g, branches, w_branch_bf, w_gate_bf, b_gate, w_out_bf, tokens_per_mod, layer):
    t = x2d.shape[0]
    tm, tn = TM_PROJ, TN_MERGE
    nch = D_MODEL // tn
    br_spec = pl.BlockSpec((tm, BR_W), lambda i, j: (i, 0))
    wg_specs = [pl.BlockSpec((None, D_MODEL, tn),
                             functools.partial(lambda i, j, n: (layer, 0, n * nch + j), n=n))
                for n in range(N_BRANCH)]
    bg_specs = [pl.BlockSpec((1, 1, tn), functools.partial(lambda i, j, n: (n * nch + j, 0, 0), n=n))
                for n in range(N_BRANCH)]
    bg = b_gate.reshape(N_BRANCH * nch, 1, tn)
    return pl.pallas_call(
        _merge_kernel,
        out_shape=jax.ShapeDtypeStruct((t, D_MODEL), F32),
        grid=(t // tm, nch),
        in_specs=[
            pl.BlockSpec((tm, D_MODEL), lambda i, j: (i, 0)),
            pl.BlockSpec((1, 6, D_MODEL), lambda i, j: ((i * tm) // tokens_per_mod, 0, 0)),
            pl.BlockSpec((1, D_MODEL), lambda i, j: (0, 0)),
            br_spec, br_spec, br_spec, br_spec,
            pl.BlockSpec((None, N_BRANCH, BR_W, tn), lambda i, j: (layer, 0, 0, j)),
            *wg_specs, *bg_specs,
            pl.BlockSpec((None, tn, D_MODEL), lambda i, j: (layer, j, 0)),
        ],
        out_specs=pl.BlockSpec((tm, D_MODEL), lambda i, j: (i, 0)),
        scratch_shapes=[pltpu.VMEM((tm, D_MODEL), BF16), pltpu.VMEM((tm, D_MODEL), F32)],
        compiler_params=_cparams(("parallel", "arbitrary")),
        name="merge",
    )(x2d, mod, g.reshape(1, D_MODEL), *branches, w_branch_bf,
      *([w_gate_bf] * N_BRANCH), *([bg] * N_BRANCH), w_out_bf)


def _route(logits):
    lane = lax.broadcasted_iota(jnp.int32, logits.shape, 1)
    ninf = -jnp.inf
    lg = jnp.where(lane < N_GROUPS, logits, ninf)
    mg = lg.max(axis=-1, keepdims=True)
    pg_top = 1.0 / jnp.exp(lg - mg).sum(axis=-1, keepdims=True)
    g_idx = jnp.where(lg == mg, lane, ROUTER_W).min(axis=-1, keepdims=True)
    lo = N_GROUPS + EXP_PER_GROUP * g_idx
    le = jnp.where((lane >= lo) & (lane < lo + EXP_PER_GROUP), logits, ninf)
    m1 = le.max(axis=-1, keepdims=True)
    e1 = jnp.where(le == m1, lane, ROUTER_W).min(axis=-1, keepdims=True)
    le2 = jnp.where(lane == e1, ninf, le)
    m2 = le2.max(axis=-1, keepdims=True)
    e2 = jnp.where(le2 == m2, lane, ROUTER_W).min(axis=-1, keepdims=True)
    se = jnp.exp(le - m1).sum(axis=-1, keepdims=True)
    pe1 = 1.0 / se
    pe2 = jnp.exp(m2 - m1) / se
    tot = pe1 + pe2
    w1 = pe1 / tot * pg_top
    w2 = pe2 / tot * pg_top
    return jnp.where(lane == e1, w1, 0.0) + jnp.where(lane == e2, w2, 0.0), g_idx


def _moe_kernel(x_ref, mod_ref, g_ref, wr_ref, br_ref, wg_ref, wu_ref, wd_ref, fg_ref,
                o_ref, h_ref, comb_ref, grp_ref, rank_ref, cnt_ref, *, final):
    grp = pl.program_id(1)
    tm = x_ref.shape[0]

    @pl.when(grp == 0)
    def _():
        h = _norm_mod(x_ref[...], g_ref[...], mod_ref[0, 3:4, :], mod_ref[0, 4:5, :])
        h_hi = h.astype(BF16)
        h_ref[...] = h_hi
        h_lo = (h - h_hi.astype(F32)).astype(BF16)
        w = wr_ref[...]
        w_hi = w.astype(BF16)
        w_lo = (w - w_hi.astype(F32)).astype(BF16)
        both = jnp.dot(h_hi, jnp.concatenate([w_hi, w_lo], axis=1), preferred_element_type=F32)
        logits = (both[:, :ROUTER_W] + both[:, ROUTER_W:]
                  + jnp.dot(h_lo, w_hi, preferred_element_type=F32)) + br_ref[...]
        comb, g_idx = _route(logits)
        comb_ref[...] = comb
        lane = lax.broadcasted_iota(jnp.int32, (tm, ROUTER_W), 1)
        member = lane == g_idx
        onehot = jnp.where(member, 1.0, 0.0).astype(BF16)
        earlier = jnp.where(lax.broadcasted_iota(jnp.int32, (tm, tm), 0)
                            > lax.broadcasted_iota(jnp.int32, (tm, tm), 1), 1.0, 0.0).astype(BF16)
        before = jnp.dot(earlier, onehot, preferred_element_type=F32)
        rank_ref[...] = jnp.where(member, before, 0.0).sum(axis=-1, keepdims=True).astype(jnp.int32)
        grp_ref[...] = g_idx
        cnt_ref[...] = jnp.where(member, 1.0, 0.0).sum(axis=0, keepdims=True)
        o_ref[...] = jnp.zeros_like(o_ref)

    lane_r = lax.broadcasted_iota(jnp.int32, (1, ROUTER_W), 1)
    count = jnp.where(lane_r == grp, cnt_ref[...], 0.0).sum().astype(jnp.int32)
    in_group = grp_ref[...] == grp
    rank = rank_ref[...]
    comb = comb_ref[...]
    comb_hi = comb.astype(BF16)
    comb_lo = (comb - comb_hi.astype(F32)).astype(BF16)
    slot_lane = lax.broadcasted_iota(jnp.int32, (tm, MOE_CHUNK), 1)
    lane_c = lax.broadcasted_iota(jnp.int32, (MOE_CHUNK, ROUTER_W), 1)

    def chunk(c, carry):
        sel = jnp.where(in_group & (rank - c * MOE_CHUNK == slot_lane), 1.0, 0.0).astype(BF16)
        xc = lax.dot_general(sel, h_ref[...], _TN, preferred_element_type=F32).astype(BF16)
        cwc = (lax.dot_general(sel, comb_hi, _TN, preferred_element_type=F32)
               + lax.dot_general(sel, comb_lo, _TN, preferred_element_type=F32))
        out = None
        for e in range(EXP_PER_GROUP):
            cw = jnp.where(lane_c == N_GROUPS + EXP_PER_GROUP * grp + e, cwc, 0.0).sum(
                axis=-1, keepdims=True)
            gate = jnp.dot(xc, wg_ref[e], preferred_element_type=F32)
            up = jnp.dot(xc, wu_ref[e], preferred_element_type=F32)
            hid = (gate * _sigmoid(gate)) * up * cw
            term = jnp.dot(hid.astype(BF16), wd_ref[grp, e], preferred_element_type=F32)
            out = term if out is None else out + term
        o_ref[...] += jnp.dot(sel, out.astype(BF16), preferred_element_type=F32)
        return carry

    lax.fori_loop(0, (count + MOE_CHUNK - 1) // MOE_CHUNK, chunk, 0)

    @pl.when(grp == pl.num_programs(1) - 1)
    def _():
        y = x_ref[...] + mod_ref[0, 5:6, :] * o_ref[...]
        if final:
            ms = jnp.mean(y * y, axis=-1, keepdims=True)
            y = y * lax.rsqrt(ms + EPS) * fg_ref[...]
        o_ref[...] = y


def _moe(x2d, mod, g, w_router, b_router, w_g_bf, w_u_bf, w_d_bf, final_g, tokens_per_mod, final,
         layer):
    t = x2d.shape[0]
    tm = TM_PROJ
    w_spec_in = pl.BlockSpec((None, None, EXP_PER_GROUP, D_MODEL, D_EXPERT),
                             lambda i, grp: (layer, grp, 0, 0, 0))
    return pl.pallas_call(
        functools.partial(_moe_kernel, final=final),
        out_shape=jax.ShapeDtypeStruct((t, D_MODEL), F32),
        grid=(t // tm, N_GROUPS),
        in_specs=[
            pl.BlockSpec((tm, D_MODEL), lambda i, grp: (i, 0)),
            pl.BlockSpec((1, 6, D_MODEL), lambda i, grp: ((i * tm) // tokens_per_mod, 0, 0)),
            pl.BlockSpec((1, D_MODEL), lambda i, grp: (0, 0)),
            pl.BlockSpec((D_MODEL, ROUTER_W), lambda i, grp: (0, 0)),
            pl.BlockSpec((1, ROUTER_W), lambda i, grp: (0, 0)),
            w_spec_in, w_spec_in,
            pl.BlockSpec((None, N_GROUPS, EXP_PER_GROUP, D_EXPERT, D_MODEL),
                         lambda i, grp: (layer, 0, 0, 0, 0), pipeline_mode=pl.Buffered(1)),
            pl.BlockSpec((1, D_MODEL), lambda i, grp: (0, 0)),
        ],
        out_specs=pl.BlockSpec((tm, D_MODEL), lambda i, grp: (i, 0)),
        scratch_shapes=[pltpu.VMEM((tm, D_MODEL), BF16), pltpu.VMEM((tm, ROUTER_W), F32),
                        pltpu.VMEM((tm, 1), jnp.int32), pltpu.VMEM((tm, 1), jnp.int32),
                        pltpu.VMEM((1, ROUTER_W), F32)],
        compiler_params=_cparams(("parallel", "arbitrary"), VMEM_LIMIT_MOE),
        name="moe_final" if final else "moe",
    )(x2d, mod, g.reshape(1, D_MODEL), w_router, b_router, w_g_bf, w_u_bf, w_d_bf,
      final_g.reshape(1, D_MODEL))


def _rope_tables(seq):
    nf = HEAD_DIM // 4
    t = jnp.arange(seq)
    inv = ROPE_BASE ** (-jnp.arange(nf, dtype=F32) / nf)
    pos = jnp.stack([t // GRID_W, t % GRID_W], -1).astype(F32)
    ang = pos[:, :, None] * inv
    cos, sin = jnp.cos(ang), jnp.sin(ang)
    cos64 = jnp.concatenate([cos[:, 0], cos[:, 0], cos[:, 1], cos[:, 1]], axis=-1)
    sin64 = jnp.concatenate([-sin[:, 0], sin[:, 0], -sin[:, 1], sin[:, 1]], axis=-1)
    return jnp.tile(cos64, (1, 2)), jnp.tile(sin64, (1, 2))


def _seq_minor_view(cache):
    nb, depth, seq = cache.shape[:3]
    nd = cache.ndim
    return jnp.transpose(cache, (0, 1, *range(3, nd), 2)).reshape(nb, depth, -1, seq)


def _seq_major_view(arr, head_dims):
    nb, depth, _, seq = arr.shape
    nd = 3 + len(head_dims)
    arr = arr.reshape(nb, depth, *head_dims, seq)
    return jnp.transpose(arr, (0, 1, nd - 1, *range(2, nd - 1)))


def kernel(x_prompt, x_sample, cache_attn_k, cache_attn_v, cache_diff_k, cache_diff_v, cache_na_k, cache_na_v, c, c_ctx, w_mod, b_mod, norm1_g, w_in, attn_sink, diff_lambda, diff_sub_g, na_rpb, w_branch, w_gate, b_gate, w_out, norm2_g, w_router_group, b_router_group, w_router_expert, b_router_expert, w_exp_gate, w_exp_up, w_exp_down, final_g):
    nbp, seq_p, _ = x_prompt.shape
    nbs, seq_s, _ = x_sample.shape
    xp = x_prompt.reshape(nbp * seq_p, D_MODEL)
    xs = x_sample.reshape(nbs * seq_s, D_MODEL)

    cond = jnp.concatenate([c_ctx[None, :], c], axis=0)
    n_cond = -(-cond.shape[0] // 8) * 8
    cond = jnp.pad(cond, ((0, n_cond - cond.shape[0]), (0, 0)))
    mods = _modulation(cond, w_mod, b_mod).reshape(DEPTH, n_cond, 6, D_MODEL)

    rope = _rope_tables(seq_s)
    cache_a = (_seq_minor_view(cache_attn_k), _seq_minor_view(cache_attn_v))
    cache_b = (_seq_minor_view(cache_diff_k), cache_diff_v)
    cache_c = (_seq_minor_view(cache_na_k), _seq_minor_view(cache_na_v))

    w_in_bf = w_in.astype(BF16)
    w_branch_bf = w_branch.astype(BF16)
    w_gate_bf = w_gate.astype(BF16)
    w_out_bf = w_out.astype(BF16)
    grouped = (DEPTH, N_GROUPS, EXP_PER_GROUP)
    w_eg_bf = w_exp_gate.astype(BF16).reshape(*grouped, D_MODEL, D_EXPERT)
    w_eu_bf = w_exp_up.astype(BF16).reshape(*grouped, D_MODEL, D_EXPERT)
    w_ed_bf = w_exp_down.astype(BF16).reshape(*grouped, D_EXPERT, D_MODEL)
    pad = ROUTER_W - N_GROUPS - N_EXPERTS
    w_router = jnp.concatenate(
        [w_router_group, w_router_expert, jnp.zeros((DEPTH, D_MODEL, pad), F32)], axis=-1)
    b_router = jnp.concatenate(
        [b_router_group, b_router_expert, jnp.zeros((DEPTH, pad), F32)], axis=-1)

    kv = None
    for l in range(DEPTH):
        lam_init = 0.8 - 0.6 * math.exp(-0.3 * l)
        mod_p = mods[l, 0:1]
        mod_s = mods[l, 1:1 + nbs]
        final = l == DEPTH - 1
        moe_w = (w_router[l], b_router[l].reshape(1, ROUTER_W), w_eg_bf, w_eu_bf, w_ed_bf)
        merge_w = (w_branch_bf, w_gate_bf, b_gate[l], w_out_bf)

        pr = _inproj(xp, mod_p, norm1_g[l], w_in_bf, xp.shape[0], seq=seq_p, layer=l, prev_kv=kv)
        kv = {name: pr[name] for name in _CTX_F32}
        o_a = _sink_attn(attn_sink[l], pr["a_q"], (kv["a_k"], kv["a_v"]), nb=nbp, seq=seq_p, layer=l)
        o_b = _diff_attn(diff_lambda[l], diff_sub_g[l], pr["b_q"], (kv["b_k"], kv["b_v"]),
                         nb=nbp, seq=seq_p, lam_init=lam_init, layer=l)
        o_c = _dense_attn(pr["c_q"], (kv["c_k"], kv["c_v"]), nb=nbp, seq=seq_p, layer=l)
        o_f = _fourier(pr["f"], nb=nbp, seq=seq_p)
        xp = _merge(xp, mod_p, norm1_g[l], (o_a, o_b, o_c, o_f), *merge_w, xp.shape[0], l)
        xp = _moe(xp, mod_p, norm2_g[l], *moe_w, final_g, xp.shape[0], final, l)

        pr = _inproj(xs, mod_s, norm1_g[l], w_in_bf, seq_s, rope=rope, layer=l)
        o_a = _sink_attn(attn_sink[l], pr["a_q"], cache_a, nb=nbs, seq=seq_s, layer=l,
                         kv_loc=(pr["a_k"], pr["a_v"]))
        o_b = _diff_attn(diff_lambda[l], diff_sub_g[l], pr["b_q"], cache_b, nb=nbs, seq=seq_s,
                         lam_init=lam_init, layer=l, kv_loc=(pr["b_k"], pr["b_v"]))
        o_c = _na_attn(pr["c_q"], (pr["c_k"], pr["c_v"]), cache_c,
                       _na_rpb_table(na_rpb[l]), nb=nbs, seq=seq_s, layer=l)
        o_f = _fourier(pr["f"], nb=nbs, seq=seq_s)
        xs = _merge(xs, mod_s, norm1_g[l], (o_a, o_b, o_c, o_f), *merge_w, seq_s, l)
        xs = _moe(xs, mod_s, norm2_g[l], *moe_w, final_g, seq_s, final, l)

    return (
        xp.reshape(x_prompt.shape),
        xs.reshape(x_sample.shape),
        _seq_major_view(kv["a_k"], (A_KV_HEADS, HEAD_DIM)),
        _seq_major_view(kv["a_v"], (A_KV_HEADS, HEAD_DIM)),
        _seq_major_view(kv["b_k"], (DIFF_HEADS, 2, HEAD_DIM)),
        kv["b_v"],
        _seq_major_view(kv["c_k"], (NA_HEADS, HEAD_DIM)),
        _seq_major_view(kv["c_v"], (NA_HEADS, HEAD_DIM)),
    )
```

```python
import functools
import math

import jax
import jax.numpy as jnp
import numpy as np
from jax import lax
from jax.experimental import pallas as pl
from jax.experimental.pallas import tpu as pltpu

F32 = jnp.float32
BF16 = jnp.bfloat16

D_MODEL = 2048
DEPTH = 2
GRID_W = 64
HEAD_DIM = 64
A_HEADS = 8
A_KV_HEADS = 2
A_GROUP = A_HEADS // A_KV_HEADS
WIN = 128
DIFF_HEADS = 4
DIFF_V_DIM = 2 * HEAD_DIM
NA_HEADS = 8
NA_ROWS = 8
NA_COLS = 16
F_GROUPS = 4
F_GROUP_W = 128
N_BRANCH = 4
BR_W = A_HEADS * HEAD_DIM
N_GROUPS = 4
EXP_PER_GROUP = 4
N_EXPERTS = N_GROUPS * EXP_PER_GROUP
D_EXPERT = 256
ROPE_BASE = 10000.0
EPS = 1e-6
NEG = -1e30
SCALE = HEAD_DIM ** -0.5

_IN_GROUPS = (
    ("a_q", 0, 512), ("a_k", 512, 128), ("a_v", 640, 128),
    ("b_q", 768, 512), ("b_k", 1280, 512), ("b_v", 1792, 512),
    ("c_q", 2304, 512), ("c_k", 2816, 512), ("c_v", 3328, 512),
    ("f", 3840, 512),
)
IN_W = 4352
_Q_GROUPS = ("a_q", "b_q", "c_q")
_ROPED = ("a_q", "a_k", "b_q", "b_k")
_CTX_F32 = ("a_k", "a_v", "b_k", "b_v", "c_k", "c_v")
_CTX_SEQ_MINOR = ("a_k", "a_v", "b_k", "c_k", "c_v")

LANES = 128
VMEM_LIMIT = 56 * 1024 * 1024
VMEM_LIMIT_MOE = 59 * 1024 * 1024

TM_PROJ = 512
TQ_SINK = 256
TQ_DIFF = 512
NA_QROWS = 4
NA_BAND = NA_ROWS + NA_QROWS - 1
TN_MERGE = 256
TN_MOD = 1024
ROUTER_W = 128
MOE_CHUNK = 192

_NT = (((1,), (1,)), ((), ()))
_TN = (((0,), (0,)), ((), ()))


def _cparams(sem, vmem_limit=VMEM_LIMIT):
    return pltpu.CompilerParams(dimension_semantics=sem, vmem_limit_bytes=vmem_limit)


def _sigmoid(x):
    return 1.0 / (1.0 + jnp.exp(-x))


def _norm_mod(x, g, shift, scale):
    ms = jnp.mean(x * x, axis=-1, keepdims=True)
    return (x * lax.rsqrt(ms + EPS) * g) * (1.0 + scale) + shift


def _mod_kernel(c_ref, w_ref, b_ref, o_ref):
    c = c_ref[...]
    s = (c * _sigmoid(c)).astype(BF16)
    o_ref[0] = jnp.dot(s, w_ref[0].astype(BF16), preferred_element_type=F32) + b_ref[0]


def _modulation(cond, w_mod, b_mod):
    n = cond.shape[0]
    out_w = w_mod.shape[-1]
    return pl.pallas_call(
        _mod_kernel,
        out_shape=jax.ShapeDtypeStruct((DEPTH, n, out_w), F32),
        grid=(DEPTH, out_w // TN_MOD),
        in_specs=[
            pl.BlockSpec((n, D_MODEL), lambda l, j: (0, 0)),
            pl.BlockSpec((1, D_MODEL, TN_MOD), lambda l, j: (l, 0, j)),
            pl.BlockSpec((1, 1, TN_MOD), lambda l, j: (l, 0, j)),
        ],
        out_specs=pl.BlockSpec((1, n, TN_MOD), lambda l, j: (l, 0, j)),
        compiler_params=_cparams(("parallel", "parallel")),
        name="modulation",
    )(cond, w_mod, b_mod.reshape(DEPTH, 1, out_w))


def _rope128(y, cos, sin):
    lane = lax.broadcasted_iota(jnp.int32, y.shape, 1)
    first = (lane % 32) < 16
    partner = jnp.where(first, pltpu.roll(y, LANES - 16, 1), pltpu.roll(y, 16, 1))
    return y * cos + partner * sin


def _inproj_kernel(*refs, latent, n_alias, seq, layer, fill_depth):
    if latent:
        x_ref, mod_ref, g_ref, w_ref, cos_ref, sin_ref = refs[:6]
        out_refs = refs[6:]
    else:
        x_ref, mod_ref, g_ref, w_ref = refs[:4]
        out_refs = refs[4 + n_alias:]
    h = _norm_mod(x_ref[...], g_ref[...], mod_ref[0, 0:1, :], mod_ref[0, 1:2, :]).astype(BF16)
    for (name, start, width), o_ref in zip(_IN_GROUPS, out_refs):
        y = jnp.dot(h, w_ref[:, start:start + width], preferred_element_type=F32)
        if name in _Q_GROUPS:
            y = y * SCALE
        if latent and name in _ROPED:
            cos = cos_ref[...]
            sin = sin_ref[...]
            for c0 in range(0, width, LANES):
                o_ref[:, c0:c0 + LANES] = _rope128(y[:, c0:c0 + LANES], cos, sin).astype(o_ref.dtype)
        elif not latent and name in _CTX_F32:
            for bb in range(o_ref.shape[0]):
                yb = y[bb * seq:(bb + 1) * seq]
                dst = o_ref.at[bb, layer] if fill_depth else o_ref.at[bb]
                if name in _CTX_SEQ_MINOR:
                    dst[...] = yb.T
                else:
                    for hd in range(DIFF_HEADS):
                        dst[:, hd, :] = _head(yb, hd, DIFF_V_DIM)
                if fill_depth:
                    for other in range(DEPTH):
                        if other != layer:
                            o_ref[bb, other] = jnp.zeros(o_ref.shape[2:], F32)
        else:
            o_ref[...] = y.astype(o_ref.dtype)


def _inproj(x2d, mod, g, w_in_bf, tokens_per_mod, *, rope=None, seq=None, layer=0, prev_kv=None):
    t = x2d.shape[0]
    tm = TM_PROJ
    latent = rope is not None
    fill_depth = not latent and prev_kv is None
    in_specs = [
        pl.BlockSpec((tm, D_MODEL), lambda i: (i, 0)),
        pl.BlockSpec((1, 6, D_MODEL), lambda i: ((i * tm) // tokens_per_mod, 0, 0)),
        pl.BlockSpec((1, D_MODEL), lambda i: (0, 0)),
        pl.BlockSpec((None, D_MODEL, IN_W), lambda i: (layer, 0, 0), pipeline_mode=pl.Buffered(1)),
    ]
    args = [x2d, mod, g.reshape(1, D_MODEL), w_in_bf]
    aliases = {}
    n_alias = 0
    if latent:
        nblk = rope[0].shape[0] // tm
        in_specs += [pl.BlockSpec((tm, LANES), lambda i: (i % nblk, 0))] * 2
        args += list(rope)
    elif prev_kv is not None:
        n_alias = len(_CTX_F32)
        in_specs += [pl.BlockSpec(memory_space=pl.ANY)] * n_alias
        args += [prev_kv[name] for name in _CTX_F32]
    out_shape, out_specs = [], []
    tb = tm // seq if not latent else None
    for oi, (name, _, width) in enumerate(_IN_GROUPS):
        if not latent and name in _CTX_F32:
            blk = (width, seq) if name in _CTX_SEQ_MINOR else (seq, DIFF_HEADS, DIFF_V_DIM)
            out_shape.append(jax.ShapeDtypeStruct((t // seq, DEPTH) + blk, F32))
            if fill_depth:
                out_specs.append(pl.BlockSpec((tb, DEPTH) + blk,
                                              lambda i, nd=len(blk): (i, 0) + (0,) * nd))
            else:
                out_specs.append(pl.BlockSpec((tb, None) + blk,
                                              lambda i, nd=len(blk): (i, layer) + (0,) * nd))
                aliases[4 + _CTX_F32.index(name)] = oi
        else:
            out_shape.append(jax.ShapeDtypeStruct((t, width), BF16))
            out_specs.append(pl.BlockSpec((tm, width), lambda i: (i, 0)))
    outs = pl.pallas_call(
        functools.partial(_inproj_kernel, latent=latent, n_alias=n_alias, seq=seq, layer=layer,
                          fill_depth=fill_depth),
        out_shape=out_shape,
        grid=(t // tm,),
        in_specs=in_specs,
        out_specs=out_specs,
        input_output_aliases=aliases,
        compiler_params=_cparams(("parallel",)),
        name="inproj_latent" if latent else "inproj_context",
    )(*args)
    return dict(zip([n for n, _, _ in _IN_GROUPS], outs))


def _attend(q, segs, sink_col=None):
    scores = []
    for k, _, bias, k_seq_minor, _ in segs:
        if k_seq_minor:
            s = jnp.dot(q, k, preferred_element_type=F32)
        else:
            s = lax.dot_general(q, k, _NT, preferred_element_type=F32)
        if bias is not None:
            s = s + bias
        scores.append(s)
    m = scores[0].max(axis=-1, keepdims=True)
    for s in scores[1:]:
        m = jnp.maximum(m, s.max(axis=-1, keepdims=True))
    if sink_col is not None:
        m = jnp.maximum(m, sink_col)
    acc = None
    den = None
    for s, (_, v, _, _, v_seq_minor) in zip(scores, segs):
        p = jnp.exp(s - m)
        ps = p.sum(axis=-1, keepdims=True)
        if v_seq_minor:
            o = lax.dot_general(p.astype(BF16), v, _NT, preferred_element_type=F32)
        else:
            o = jnp.dot(p.astype(BF16), v, preferred_element_type=F32)
        acc = o if acc is None else acc + o
        den = ps if den is None else den + ps
    if sink_col is not None:
        den = den + jnp.exp(sink_col - m)
    return acc, den


def _head(x, h, width=HEAD_DIM):
    return x[:, h * width:(h + 1) * width]


def _head_rows(x, h, width=HEAD_DIM):
    return x[h * width:(h + 1) * width, :]


def _pair(x, i):
    return x[:, i * LANES:(i + 1) * LANES]


def _pair_rows(x, i):
    return x[i * LANES:(i + 1) * LANES, :]


def _keep_half(x, half, axis):
    idx = lax.broadcasted_iota(jnp.int32, x.shape, axis)
    return jnp.where((idx // HEAD_DIM) == half, x, jnp.zeros_like(x))


def _place_rows(x, half):
    z = jnp.zeros_like(x)
    return jnp.concatenate([x, z] if half == 0 else [z, x], axis=0)


def _kv_spec(width, seq, layer):
    return pl.BlockSpec((None, None, width, seq), lambda b, i: (b, layer, 0, 0))


def _sink_attn_kernel(*refs, latent, tq, seq):
    if latent:
        sink_ref, q_ref, k_ref, v_ref, kc_ref, vc_ref, o_ref = refs
    else:
        sink_ref, q_ref, kc_ref, vc_ref, o_ref = refs
    q_all = q_ref[...]
    k_ctx = kc_ref[...].astype(BF16)
    v_ctx = vc_ref[...].astype(BF16)
    if latent:
        band = tq + 2 * WIN
        qb = pl.program_id(1)
        start = pl.multiple_of(jnp.clip(qb * tq - WIN, 0, seq - band), WIN)
        k_loc = k_ref[pl.ds(start, band), :]
        v_loc = v_ref[pl.ds(start, band), :]
        k_swp = jnp.concatenate([_head(k_loc, 1), _head(k_loc, 0)], axis=1)
        v_swp = jnp.concatenate([_head(v_loc, 1), _head(v_loc, 0)], axis=1)
        qpos = qb * tq + lax.broadcasted_iota(jnp.int32, (tq, band), 0)
        kpos = start + lax.broadcasted_iota(jnp.int32, (tq, band), 1)
        bias = jnp.where(jnp.abs(qpos - kpos) <= WIN, 0.0, NEG).astype(F32)
    outs = []
    for i in range(A_HEADS // 2):
        kv = (2 * i) // A_GROUP
        o_pair = None
        for half in range(2):
            h = 2 * i + half
            sink_col = jnp.full((tq, 1), sink_ref[0, h], F32)
            segs = [(_place_rows(_head_rows(k_ctx, kv), half),
                     _place_rows(_head_rows(v_ctx, kv), half), None, True, True)]
            if latent:
                k_src, v_src = (k_loc, v_loc) if kv == half else (k_swp, v_swp)
                segs.append((_keep_half(k_src, half, 1), _keep_half(v_src, half, 1),
                             bias, False, False))
            acc, den = _attend(_pair(q_all, i), segs, sink_col)
            o = acc / den
            o_pair = o if o_pair is None else o_pair + o
        outs.append(o_pair)
    o_ref[...] = jnp.concatenate(outs, axis=1).astype(o_ref.dtype)


def _sink_attn(sink, q, kv_ctx, *, nb, seq, layer, kv_loc=None):
    latent = kv_loc is not None
    tq = TQ_SINK if latent else seq
    nq = seq // tq
    kw = A_KV_HEADS * HEAD_DIM
    s_ctx = kv_ctx[0].shape[-1]
    in_specs = [
        pl.BlockSpec(memory_space=pltpu.SMEM),
        pl.BlockSpec((tq, BR_W), lambda b, i: (b * nq + i, 0)),
    ]
    args = [sink.reshape(1, A_HEADS), q]
    if latent:
        in_specs += [pl.BlockSpec((seq, kw), lambda b, i: (b, 0))] * 2
        args += list(kv_loc)
    in_specs += [_kv_spec(kw, s_ctx, layer)] * 2
    args += list(kv_ctx)
    return pl.pallas_call(
        functools.partial(_sink_attn_kernel, latent=latent, tq=tq, seq=seq),
        out_shape=jax.ShapeDtypeStruct(q.shape, BF16),
        grid=(nb, nq),
        in_specs=in_specs,
        out_specs=pl.BlockSpec((tq, BR_W), lambda b, i: (b * nq + i, 0)),
        compiler_params=_cparams(("parallel", "arbitrary")),
        name="sink_attn_latent" if latent else "sink_attn_context",
    )(*args)


def _diff_attn_kernel(*refs, latent, lam_init):
    if latent:
        lam_ref, g_ref, q_ref, k_ref, v_ref, kc_ref, vc_ref, o_ref = refs
    else:
        lam_ref, g_ref, q_ref, kc_ref, vc_ref, o_ref = refs
    lam = lam_ref[...]
    lam_a = jnp.sum(lam[0:1] * lam[1:2], axis=-1, keepdims=True)
    lam_b = jnp.sum(lam[2:3] * lam[3:4], axis=-1, keepdims=True)
    lam_full = jnp.exp(lam_a) - jnp.exp(lam_b) + lam_init
    q_all = q_ref[...]
    k_ctx = kc_ref[...].astype(BF16)
    if latent:
        k_loc = k_ref[...]
        v_loc = v_ref[...]
    outs = []
    for h in range(DIFF_HEADS):
        v_ctx = vc_ref[:, h, :].astype(BF16)
        res = []
        for sub in range(2):
            segs = [(_keep_half(_pair_rows(k_ctx, h), sub, 0), v_ctx, None, True, False)]
            if latent:
                segs.append((_keep_half(_pair(k_loc, h), sub, 1), _pair(v_loc, h),
                             None, False, False))
            acc, den = _attend(_pair(q_all, h), segs)
            res.append(acc / den)
        o = res[0] - lam_full * res[1]
        ms = jnp.mean(o * o, axis=-1, keepdims=True)
        outs.append(o * lax.rsqrt(ms + EPS) * g_ref[...] * (1.0 - lam_init))
    o_ref[...] = jnp.concatenate(outs, axis=1).astype(o_ref.dtype)


def _diff_attn(lam, sub_g, q, kv_ctx, *, nb, seq, lam_init, layer, kv_loc=None):
    latent = kv_loc is not None
    tq = TQ_DIFF if latent else seq
    nq = seq // tq
    kw = DIFF_HEADS * 2 * HEAD_DIM
    s_ctx = kv_ctx[0].shape[-1]
    in_specs = [
        pl.BlockSpec((4, HEAD_DIM), lambda b, i: (0, 0)),
        pl.BlockSpec((1, DIFF_V_DIM), lambda b, i: (0, 0)),
        pl.BlockSpec((tq, BR_W), lambda b, i: (b * nq + i, 0)),
    ]
    args = [lam, sub_g.reshape(1, DIFF_V_DIM), q]
    if latent:
        in_specs += [pl.BlockSpec((seq, kw), lambda b, i: (b, 0))] * 2
        args += list(kv_loc)
    in_specs += [_kv_spec(kw, s_ctx, layer),
                 pl.BlockSpec((None, None, s_ctx, DIFF_HEADS, DIFF_V_DIM),
                              lambda b, i: (b, layer, 0, 0, 0))]
    args += list(kv_ctx)
    return pl.pallas_call(
        functools.partial(_diff_attn_kernel, latent=latent, lam_init=lam_init),
        out_shape=jax.ShapeDtypeStruct(q.shape, BF16),
        grid=(nb, nq),
        in_specs=in_specs,
        out_specs=pl.BlockSpec((tq, BR_W), lambda b, i: (b * nq + i, 0)),
        compiler_params=_cparams(("parallel", "arbitrary")),
        name="diff_attn_latent" if latent else "diff_attn_context",
    )(*args)


def _dense_attn_kernel(q_ref, k_ref, v_ref, o_ref):
    q_all = q_ref[...]
    k = k_ref[...].astype(BF16)
    v = v_ref[...].astype(BF16)
    outs = []
    for i in range(NA_HEADS // 2):
        o_pair = None
        for half in range(2):
            segs = [(_keep_half(_pair_rows(k, i), half, 0), _keep_half(_pair_rows(v, i), half, 0),
                     None, True, True)]
            acc, den = _attend(_pair(q_all, i), segs)
            o = acc / den
            o_pair = o if o_pair is None else o_pair + o
        outs.append(o_pair)
    o_ref[...] = jnp.concatenate(outs, axis=1).astype(o_ref.dtype)


def _dense_attn(q, kv_ctx, *, nb, seq, layer):
    return pl.pallas_call(
        _dense_attn_kernel,
        out_shape=jax.ShapeDtypeStruct(q.shape, BF16),
        grid=(nb, 1),
        in_specs=[pl.BlockSpec((seq, BR_W), lambda b, i: (b, 0)),
                  _kv_spec(BR_W, seq, layer), _kv_spec(BR_W, seq, layer)],
        out_specs=pl.BlockSpec((seq, BR_W), lambda b, i: (b, 0)),
        compiler_params=_cparams(("parallel", "arbitrary")),
        name="dense_attn_context",
    )(q, *kv_ctx)


def _na_window_start(r, rows):
    return jnp.clip(r - NA_ROWS // 2, 0, rows - NA_ROWS)


def _na_band_start(step, rows):
    return jnp.clip(step * NA_QROWS - NA_ROWS // 2, 0, rows - NA_BAND)


def _na_attn_kernel(q_ref, k_ref, v_ref, kc_ref, vc_ref, tab_ref, o_ref, bias_ref, *, rows):
    step = pl.program_id(0)
    band_row = _na_band_start(step, rows)

    @pl.when(pl.program_id(1) == 0)
    def _():
        for ri in range(NA_QROWS):
            r = step * NA_QROWS + ri
            rs = _na_window_start(r, rows)
            for h in range(NA_HEADS):
                blocks = []
                for j in range(NA_BAND):
                    kr = band_row + j
                    ok = (kr >= rs) & (kr < rs + NA_ROWS)
                    dr = jnp.clip(kr - r + NA_ROWS - 1, 0, 2 * NA_ROWS - 2)
                    blocks.append(jnp.where(ok, tab_ref[h, dr], NEG))
                bias_ref[h, ri * GRID_W:(ri + 1) * GRID_W, :] = jnp.concatenate(blocks, axis=1)

    start = pl.multiple_of(band_row * GRID_W, GRID_W)
    band = NA_BAND * GRID_W
    q_all = q_ref[...]
    k_loc = k_ref[pl.ds(start, band), :]
    v_loc = v_ref[pl.ds(start, band), :]
    k_ctx = kc_ref[...].astype(BF16)
    v_ctx = vc_ref[...].astype(BF16)
    outs = []
    for i in range(NA_HEADS // 2):
        o_pair = None
        for half in range(2):
            segs = [(_keep_half(_pair_rows(k_ctx, i), half, 0),
                     _keep_half(_pair_rows(v_ctx, i), half, 0), None, True, True),
                    (_keep_half(_pair(k_loc, i), half, 1), _keep_half(_pair(v_loc, i), half, 1),
                     bias_ref[2 * i + half], False, False)]
            acc, den = _attend(_pair(q_all, i), segs)
            o = acc / den
            o_pair = o if o_pair is None else o_pair + o
        outs.append(o_pair)
    o_ref[...] = jnp.concatenate(outs, axis=1).astype(o_ref.dtype)


def _na_attn(q, kv_loc, kv_ctx, rpb_tab, *, nb, seq, layer):
    rows = seq // GRID_W
    nsteps = rows // NA_QROWS
    tq = NA_QROWS * GRID_W
    band = NA_BAND * GRID_W
    past = kv_ctx[0].shape[-1]
    for s in range(nsteps):
        us = min(max(s * NA_QROWS - NA_ROWS // 2, 0), rows - NA_BAND)
        for r in range(s * NA_QROWS, (s + 1) * NA_QROWS):
            rs = min(max(r - NA_ROWS // 2, 0), rows - NA_ROWS)
            assert us <= rs and rs + NA_ROWS <= us + NA_BAND
    ctx_spec = pl.BlockSpec((None, None, BR_W, past), lambda s, b: (b, layer, 0, 0))
    return pl.pallas_call(
        functools.partial(_na_attn_kernel, rows=rows),
        out_shape=jax.ShapeDtypeStruct(q.shape, BF16),
        grid=(nsteps, nb),
        in_specs=[
            pl.BlockSpec((tq, BR_W), lambda s, b: (b * nsteps + s, 0)),
            pl.BlockSpec((seq, BR_W), lambda s, b: (b, 0)),
            pl.BlockSpec((seq, BR_W), lambda s, b: (b, 0)),
            ctx_spec, ctx_spec,
            pl.BlockSpec(rpb_tab.shape, lambda s, b: (0, 0, 0, 0)),
        ],
        out_specs=pl.BlockSpec((tq, BR_W), lambda s, b: (b * nsteps + s, 0)),
        scratch_shapes=[pltpu.VMEM((NA_HEADS, tq, band), F32)],
        compiler_params=_cparams(("arbitrary", "arbitrary")),
        name="na_attn_latent",
    )(q, *kv_loc, *kv_ctx, rpb_tab)


def _rpb_expand_kernel(rpb_ref, onehot_ref, o_ref):
    o_ref[...] = jnp.dot(rpb_ref[...], onehot_ref[...], preferred_element_type=F32,
                         precision=lax.Precision.HIGHEST)


def _na_rpb_table(rpb):
    n_dr, n_dc = 2 * NA_ROWS - 1, 2 * NA_COLS - 1
    qcol = np.arange(GRID_W)
    kcol = np.arange(GRID_W)
    cs = np.clip(qcol - NA_COLS // 2, 0, GRID_W - NA_COLS)
    col_ok = (kcol[None, :] >= cs[:, None]) & (kcol[None, :] < cs[:, None] + NA_COLS)
    dc = np.clip(kcol[None, :] - qcol[:, None] + NA_COLS - 1, 0, n_dc - 1)
    onehot = np.zeros((32, GRID_W * GRID_W), np.float32)
    onehot[dc.reshape(-1), np.arange(GRID_W * GRID_W)] = 1.0
    rpb2 = jnp.pad(rpb.reshape(NA_HEADS * n_dr, n_dc), ((0, 0), (0, 32 - n_dc)))
    full = pl.pallas_call(
        _rpb_expand_kernel,
        out_shape=jax.ShapeDtypeStruct((NA_HEADS * n_dr, GRID_W * GRID_W), F32),
        name="rpb_expand",
    )(rpb2, jnp.asarray(onehot))
    full = full.reshape(NA_HEADS, n_dr, GRID_W, GRID_W)
    return full + jnp.asarray(np.where(col_ok, 0.0, NEG).astype(np.float32))


def _dft_tables(seq):
    pos = np.arange(seq)
    ang = 2.0 * np.pi * ((pos[:, None] * pos[None, :]) % seq) / seq
    left = np.concatenate([np.cos(ang), -np.sin(ang)], axis=1).astype(np.float32)
    ch = np.arange(F_GROUP_W)
    angw = 2.0 * np.pi * ((ch[:, None] * ch[None, :]) % F_GROUP_W) / F_GROUP_W
    right = np.concatenate([np.cos(angw), np.sin(angw)], axis=1).astype(np.float32)
    return jnp.asarray(left).astype(BF16), jnp.asarray(right).astype(BF16)


def _fourier_kernel(f_ref, left_ref, right_ref, o_ref, stack_ref, *, seq):
    f = f_ref[...]
    right = right_ref[...]
    for g in range(F_GROUPS):
        t = jnp.dot(_head(f, g, F_GROUP_W), right, preferred_element_type=F32).astype(BF16)
        stack_ref[0:seq, g * F_GROUP_W:(g + 1) * F_GROUP_W] = t[:, :F_GROUP_W]
        stack_ref[seq:2 * seq, g * F_GROUP_W:(g + 1) * F_GROUP_W] = t[:, F_GROUP_W:]
    y = jnp.dot(left_ref[...], stack_ref[...], preferred_element_type=F32)
    o_ref[...] = (y * (1.0 / math.sqrt(seq * F_GROUP_W))).astype(o_ref.dtype)


def _fourier(f, *, nb, seq):
    left, right = _dft_tables(seq)
    w = F_GROUPS * F_GROUP_W
    return pl.pallas_call(
        functools.partial(_fourier_kernel, seq=seq),
        out_shape=jax.ShapeDtypeStruct(f.shape, BF16),
        grid=(nb,),
        in_specs=[
            pl.BlockSpec((seq, w), lambda b: (b, 0)),
            pl.BlockSpec((seq, 2 * seq), lambda b: (0, 0)),
            pl.BlockSpec((F_GROUP_W, 2 * F_GROUP_W), lambda b: (0, 0)),
        ],
        out_specs=pl.BlockSpec((seq, w), lambda b: (b, 0)),
        scratch_shapes=[pltpu.VMEM((2 * seq, w), BF16)],
        compiler_params=_cparams(("parallel",)),
        name=f"fourier_{seq}",
    )(f, left, right)


def _merge_kernel(x_ref, mod_ref, g_ref, *refs):
    br_refs = refs[0:N_BRANCH]
    wbr_ref = refs[N_BRANCH]
    wg_refs = refs[N_BRANCH + 1:2 * N_BRANCH + 1]
    bg_refs = refs[2 * N_BRANCH + 1:3 * N_BRANCH + 1]
    wo_ref, o_ref, h_ref, acc_ref = refs[3 * N_BRANCH + 1:]
    j = pl.program_id(1)

    @pl.when(j == 0)
    def _():
        h_ref[...] = _norm_mod(x_ref[...], g_ref[...], mod_ref[0, 0:1, :],
                               mod_ref[0, 1:2, :]).astype(BF16)
        acc_ref[...] = jnp.zeros_like(acc_ref)

    h = h_ref[...]
    mix = None
    for n in range(N_BRANCH):
        gate = _sigmoid(jnp.dot(h, wg_refs[n][...], preferred_element_type=F32) + bg_refs[n][0])
        proj = jnp.dot(br_refs[n][...], wbr_ref[n], preferred_element_type=F32)
        term = gate * proj
        mix = term if mix is None else mix + term
    acc_ref[...] += jnp.dot(mix.astype(BF16), wo_ref[...], preferred_element_type=F32)

    @pl.when(j == pl.num_programs(1) - 1)
    def _():
        o_ref[...] = x_ref[...] + mod_ref[0, 2:3, :] * acc_ref[...]


def _merge(x2d, mod, g, branches, w_branch_bf, w_gate_bf, b_gate, w_out_bf, tokens_per_mod, layer):
    t = x2d.shape[0]
    tm, tn = TM_PROJ, TN_MERGE
    nch = D_MODEL // tn
    br_spec = pl.BlockSpec((tm, BR_W), lambda i, j: (i, 0))
    wg_specs = [pl.BlockSpec((None, D_MODEL, tn),
                             functools.partial(lambda i, j, n: (layer, 0, n * nch + j), n=n))
                for n in range(N_BRANCH)]
    bg_specs = [pl.BlockSpec((1, 1, tn), functools.partial(lambda i, j, n: (n * nch + j, 0, 0), n=n))
                for n in range(N_BRANCH)]
    bg = b_gate.reshape(N_BRANCH * nch, 1, tn)
    return pl.pallas_call(
        _merge_kernel,
        out_shape=jax.ShapeDtypeStruct((t, D_MODEL), F32),
        grid=(t // tm, nch),
        in_specs=[
            pl.BlockSpec((tm, D_MODEL), lambda i, j: (i, 0)),
            pl.BlockSpec((1, 6, D_MODEL), lambda i, j: ((i * tm) // tokens_per_mod, 0, 0)),
            pl.BlockSpec((1, D_MODEL), lambda i, j: (0, 0)),
            br_spec, br_spec, br_spec, br_spec,
            pl.BlockSpec((None, N_BRANCH, BR_W, tn), lambda i, j: (layer, 0, 0, j)),
            *wg_specs, *bg_specs,
            pl.BlockSpec((None, tn, D_MODEL), lambda i, j: (layer, j, 0)),
        ],
        out_specs=pl.BlockSpec((tm, D_MODEL), lambda i, j: (i, 0)),
        scratch_shapes=[pltpu.VMEM((tm, D_MODEL), BF16), pltpu.VMEM((tm, D_MODEL), F32)],
        compiler_params=_cparams(("parallel", "arbitrary")),
        name="merge",
    )(x2d, mod, g.reshape(1, D_MODEL), *branches, w_branch_bf,
      *([w_gate_bf] * N_BRANCH), *([bg] * N_BRANCH), w_out_bf)


def _route(logits):
    lane = lax.broadcasted_iota(jnp.int32, logits.shape, 1)
    ninf = -jnp.inf
    lg = jnp.where(lane < N_GROUPS, logits, ninf)
    mg = lg.max(axis=-1, keepdims=True)
    pg_top = 1.0 / jnp.exp(lg - mg).sum(axis=-1, keepdims=True)
    g_idx = jnp.where(lg == mg, lane, ROUTER_W).min(axis=-1, keepdims=True)
    lo = N_GROUPS + EXP_PER_GROUP * g_idx
    le = jnp.where((lane >= lo) & (lane < lo + EXP_PER_GROUP), logits, ninf)
    m1 = le.max(axis=-1, keepdims=True)
    e1 = jnp.where(le == m1, lane, ROUTER_W).min(axis=-1, keepdims=True)
    le2 = jnp.where(lane == e1, ninf, le)
    m2 = le2.max(axis=-1, keepdims=True)
    e2 = jnp.where(le2 == m2, lane, ROUTER_W).min(axis=-1, keepdims=True)
    se = jnp.exp(le - m1).sum(axis=-1, keepdims=True)
    pe1 = 1.0 / se
    pe2 = jnp.exp(m2 - m1) / se
    tot = pe1 + pe2
    w1 = pe1 / tot * pg_top
    w2 = pe2 / tot * pg_top
    return jnp.where(lane == e1, w1, 0.0) + jnp.where(lane == e2, w2, 0.0), g_idx


def _moe_kernel(x_ref, mod_ref, g_ref, wr_ref, br_ref, wg_ref, wu_ref, wd_ref, fg_ref,
                o_ref, h_ref, comb_ref, grp_ref, rank_ref, cnt_ref, *, final):
    grp = pl.program_id(1)
    tm = x_ref.shape[0]

    @pl.when(grp == 0)
    def _():
        h = _norm_mod(x_ref[...], g_ref[...], mod_ref[0, 3:4, :], mod_ref[0, 4:5, :])
        h_hi = h.astype(BF16)
        h_ref[...] = h_hi
        h_lo = (h - h_hi.astype(F32)).astype(BF16)
        w = wr_ref[...]
        w_hi = w.astype(BF16)
        w_lo = (w - w_hi.astype(F32)).astype(BF16)
        both = jnp.dot(h_hi, jnp.concatenate([w_hi, w_lo], axis=1), preferred_element_type=F32)
        logits = (both[:, :ROUTER_W] + both[:, ROUTER_W:]
                  + jnp.dot(h_lo, w_hi, preferred_element_type=F32)) + br_ref[...]
        comb, g_idx = _route(logits)
        comb_ref[...] = comb
        lane = lax.broadcasted_iota(jnp.int32, (tm, ROUTER_W), 1)
        member = lane == g_idx
        onehot = jnp.where(member, 1.0, 0.0).astype(BF16)
        earlier = jnp.where(lax.broadcasted_iota(jnp.int32, (tm, tm), 0)
                            > lax.broadcasted_iota(jnp.int32, (tm, tm), 1), 1.0, 0.0).astype(BF16)
        before = jnp.dot(earlier, onehot, preferred_element_type=F32)
        rank_ref[...] = jnp.where(member, before, 0.0).sum(axis=-1, keepdims=True).astype(jnp.int32)
        grp_ref[...] = g_idx
        cnt_ref[...] = jnp.where(member, 1.0, 0.0).sum(axis=0, keepdims=True)
        o_ref[...] = jnp.zeros_like(o_ref)

    lane_r = lax.broadcasted_iota(jnp.int32, (1, ROUTER_W), 1)
    count = jnp.where(lane_r == grp, cnt_ref[...], 0.0).sum().astype(jnp.int32)
    in_group = grp_ref[...] == grp
    rank = rank_ref[...]
    comb = comb_ref[...]
    comb_hi = comb.astype(BF16)
    comb_lo = (comb - comb_hi.astype(F32)).astype(BF16)
    slot_lane = lax.broadcasted_iota(jnp.int32, (tm, MOE_CHUNK), 1)
    lane_c = lax.broadcasted_iota(jnp.int32, (MOE_CHUNK, ROUTER_W), 1)

    def chunk(c, carry):
        sel = jnp.where(in_group & (rank - c * MOE_CHUNK == slot_lane), 1.0, 0.0).astype(BF16)
        xc = lax.dot_general(sel, h_ref[...], _TN, preferred_element_type=F32).astype(BF16)
        cwc = (lax.dot_general(sel, comb_hi, _TN, preferred_element_type=F32)
               + lax.dot_general(sel, comb_lo, _TN, preferred_element_type=F32))
        out = None
        for e in range(EXP_PER_GROUP):
            cw = jnp.where(lane_c == N_GROUPS + EXP_PER_GROUP * grp + e, cwc, 0.0).sum(
                axis=-1, keepdims=True)
            gate = jnp.dot(xc, wg_ref[e], preferred_element_type=F32)
            up = jnp.dot(xc, wu_ref[e], preferred_element_type=F32)
            hid = (gate * _sigmoid(gate)) * up * cw
            term = jnp.dot(hid.astype(BF16), wd_ref[grp, e], preferred_element_type=F32)
            out = term if out is None else out + term
        o_ref[...] += jnp.dot(sel, out.astype(BF16), preferred_element_type=F32)
        return carry

    lax.fori_loop(0, (count + MOE_CHUNK - 1) // MOE_CHUNK, chunk, 0)

    @pl.when(grp == pl.num_programs(1) - 1)
    def _():
        y = x_ref[...] + mod_ref[0, 5:6, :] * o_ref[...]
        if final:
            ms = jnp.mean(y * y, axis=-1, keepdims=True)
            y = y * lax.rsqrt(ms + EPS) * fg_ref[...]
        o_ref[...] = y


def _moe(x2d, mod, g, w_router, b_router, w_g_bf, w_u_bf, w_d_bf, final_g, tokens_per_mod, final,
         layer):
    t = x2d.shape[0]
    tm = TM_PROJ
    w_spec_in = pl.BlockSpec((None, None, EXP_PER_GROUP, D_MODEL, D_EXPERT),
                             lambda i, grp: (layer, grp, 0, 0, 0))
    return pl.pallas_call(
        functools.partial(_moe_kernel, final=final),
        out_shape=jax.ShapeDtypeStruct((t, D_MODEL), F32),
        grid=(t // tm, N_GROUPS),
        in_specs=[
            pl.BlockSpec((tm, D_MODEL), lambda i, grp: (i, 0)),
            pl.BlockSpec((1, 6, D_MODEL), lambda i, grp: ((i * tm) // tokens_per_mod, 0, 0)),
            pl.BlockSpec((1, D_MODEL), lambda i, grp: (0, 0)),
            pl.BlockSpec((D_MODEL, ROUTER_W), lambda i, grp: (0, 0)),
            pl.BlockSpec((1, ROUTER_W), lambda i, grp: (0, 0)),
            w_spec_in, w_spec_in,
            pl.BlockSpec((None, N_GROUPS, EXP_PER_GROUP, D_EXPERT, D_MODEL),
                         lambda i, grp: (layer, 0, 0, 0, 0), pipeline_mode=pl.Buffered(1)),
            pl.BlockSpec((1, D_MODEL), lambda i, grp: (0, 0)),
        ],
        out_specs=pl.BlockSpec((tm, D_MODEL), lambda i, grp: (i, 0)),
        scratch_shapes=[pltpu.VMEM((tm, D_MODEL), BF16), pltpu.VMEM((tm, ROUTER_W), F32),
                        pltpu.VMEM((tm, 1), jnp.int32), pltpu.VMEM((tm, 1), jnp.int32),
                        pltpu.VMEM((1, ROUTER_W), F32)],
        compiler_params=_cparams(("parallel", "arbitrary"), VMEM_LIMIT_MOE),
        name="moe_final" if final else "moe",
    )(x2d, mod, g.reshape(1, D_MODEL), w_router, b_router, w_g_bf, w_u_bf, w_d_bf,
      final_g.reshape(1, D_MODEL))


def _rope_tables(seq):
    nf = HEAD_DIM // 4
    t = jnp.arange(seq)
    inv = ROPE_BASE ** (-jnp.arange(nf, dtype=F32) / nf)
    pos = jnp.stack([t // GRID_W, t % GRID_W], -1).astype(F32)
    ang = pos[:, :, None] * inv
    cos, sin = jnp.cos(ang), jnp.sin(ang)
    cos64 = jnp.concatenate([cos[:, 0], cos[:, 0], cos[:, 1], cos[:, 1]], axis=-1)
    sin64 = jnp.concatenate([-sin[:, 0], sin[:, 0], -sin[:, 1], sin[:, 1]], axis=-1)
    return jnp.tile(cos64, (1, 2)), jnp.tile(sin64, (1, 2))


def _seq_minor_view(cache):
    nb, depth, seq = cache.shape[:3]
    nd = cache.ndim
    return jnp.transpose(cache, (0, 1, *range(3, nd), 2)).reshape(nb, depth, -1, seq)


def _seq_major_view(arr, head_dims):
    nb, depth, _, seq = arr.shape
    nd = 3 + len(head_dims)
    arr = arr.reshape(nb, depth, *head_dims, seq)
    return jnp.transpose(arr, (0, 1, nd - 1, *range(2, nd - 1)))


def kernel(x_prompt, x_sample, cache_attn_k, cache_attn_v, cache_diff_k, cache_diff_v, cache_na_k, cache_na_v, c, c_ctx, w_mod, b_mod, norm1_g, w_in, attn_sink, diff_lambda, diff_sub_g, na_rpb, w_branch, w_gate, b_gate, w_out, norm2_g, w_router_group, b_router_group, w_router_expert, b_router_expert, w_exp_gate, w_exp_up, w_exp_down, final_g):
    nbp, seq_p, _ = x_prompt.shape
    nbs, seq_s, _ = x_sample.shape
    xp = x_prompt.reshape(nbp * seq_p, D_MODEL)
    xs = x_sample.reshape(nbs * seq_s, D_MODEL)

    cond = jnp.concatenate([c_ctx[None, :], c], axis=0)
    n_cond = -(-cond.shape[0] // 8) * 8
    cond = jnp.pad(cond, ((0, n_cond - cond.shape[0]), (0, 0)))
    mods = _modulation(cond, w_mod, b_mod).reshape(DEPTH, n_cond, 6, D_MODEL)

    rope = _rope_tables(seq_s)
    cache_a = (_seq_minor_view(cache_attn_k), _seq_minor_view(cache_attn_v))
    cache_b = (_seq_minor_view(cache_diff_k), cache_diff_v)
    cache_c = (_seq_minor_view(cache_na_k), _seq_minor_view(cache_na_v))

    w_in_bf = w_in.astype(BF16)
    w_branch_bf = w_branch.astype(BF16)
    w_gate_bf = w_gate.astype(BF16)
    w_out_bf = w_out.astype(BF16)
    grouped = (DEPTH, N_GROUPS, EXP_PER_GROUP)
    w_eg_bf = w_exp_gate.astype(BF16).reshape(*grouped, D_MODEL, D_EXPERT)
    w_eu_bf = w_exp_up.astype(BF16).reshape(*grouped, D_MODEL, D_EXPERT)
    w_ed_bf = w_exp_down.astype(BF16).reshape(*grouped, D_EXPERT, D_MODEL)
    pad = ROUTER_W - N_GROUPS - N_EXPERTS
    w_router = jnp.concatenate(
        [w_router_group, w_router_expert, jnp.zeros((DEPTH, D_MODEL, pad), F32)], axis=-1)
    b_router = jnp.concatenate(
        [b_router_group, b_router_expert, jnp.zeros((DEPTH, pad), F32)], axis=-1)

    kv = None
    for l in range(DEPTH):
        lam_init = 0.8 - 0.6 * math.exp(-0.3 * l)
        mod_p = mods[l, 0:1]
        mod_s = mods[l, 1:1 + nbs]
        final = l == DEPTH - 1
        moe_w = (w_router[l], b_router[l].reshape(1, ROUTER_W), w_eg_bf, w_eu_bf, w_ed_bf)
        merge_w = (w_branch_bf, w_gate_bf, b_gate[l], w_out_bf)

        pr = _inproj(xp, mod_p, norm1_g[l], w_in_bf, xp.shape[0], seq=seq_p, layer=l, prev_kv=kv)
        kv = {name: pr[name] for name in _CTX_F32}
        o_a = _sink_attn(attn_sink[l], pr["a_q"], (kv["a_k"], kv["a_v"]), nb=nbp, seq=seq_p, layer=l)
        o_b = _diff_attn(diff_lambda[l], diff_sub_g[l], pr["b_q"], (kv["b_k"], kv["b_v"]),
                         nb=nbp, seq=seq_p, lam_init=lam_init, layer=l)
        o_c = _dense_attn(pr["c_q"], (kv["c_k"], kv["c_v"]), nb=nbp, seq=seq_p, layer=l)
        o_f = _fourier(pr["f"], nb=nbp, seq=seq_p)
        xp = _merge(xp, mod_p, norm1_g[l], (o_a, o_b, o_c, o_f), *merge_w, xp.shape[0], l)
        xp = _moe(xp, mod_p, norm2_g[l], *moe_w, final_g, xp.shape[0], final, l)

        pr = _inproj(xs, mod_s, norm1_g[l], w_in_bf, seq_s, rope=rope, layer=l)
        o_a = _sink_attn(attn_sink[l], pr["a_q"], cache_a, nb=nbs, seq=seq_s, layer=l,
                         kv_loc=(pr["a_k"], pr["a_v"]))
        o_b = _diff_attn(diff_lambda[l], diff_sub_g[l], pr["b_q"], cache_b, nb=nbs, seq=seq_s,
                         lam_init=lam_init, layer=l, kv_loc=(pr["b_k"], pr["b_v"]))
        o_c = _na_attn(pr["c_q"], (pr["c_k"], pr["c_v"]), cache_c,
                       _na_rpb_table(na_rpb[l]), nb=nbs, seq=seq_s, layer=l)
        o_f = _fourier(pr["f"], nb=nbs, seq=seq_s)
        xs = _merge(xs, mod_s, norm1_g[l], (o_a, o_b, o_c, o_f), *merge_w, seq_s, l)
        xs = _moe(xs, mod_s, norm2_g[l], *moe_w, final_g, seq_s, final, l)

    return (
        xp.reshape(x_prompt.shape),
        xs.reshape(x_sample.shape),
        _seq_major_view(kv["a_k"], (A_KV_HEADS, HEAD_DIM)),
        _seq_major_view(kv["a_v"], (A_KV_HEADS, HEAD_DIM)),
        _seq_major_view(kv["b_k"], (DIFF_HEADS, 2, HEAD_DIM)),
        kv["b_v"],
        _seq_major_view(kv["c_k"], (NA_HEADS, HEAD_DIM)),
        _seq_major_view(kv["c_v"], (NA_HEADS, HEAD_DIM)),
    )
```
